```python
import jax, jax.numpy as jnp
from jax import lax
import numpy as np

D_MODEL = 4096
BATCH = 16
SEQ = 256
DEPTH = 2
DEC_BATCH = 4
DEC_SEQ = 2048
PAST_LEN = 512

GRID_W = 64
HEAD_DIM = 128
A_HEADS = 16
B_HEADS = 16
B_KV_HEADS = 4
C_HEADS = 32
C_KV_HEADS = 4
WIN_H_MAX = 8
WIN_W = 16
NA_QCOLS = 16
NA_BAND = 32
Q_BLOCK = 128
C_WINDOW = 128
N_EXPERTS = 32
TOP_K = 4
MOE_D_FF = 2048
MOE_BLOCK = 128
SWIGLU_LIMIT = 7.0
SWIGLU_ALPHA = 1.702
ROPE_THETA = 10000.0
RMS_EPS = 1e-6
ATTN_SCALE = HEAD_DIM ** -0.5
N_EVEN = (DEPTH + 1) // 2
N_ODD = DEPTH // 2
EVEN_GROUPS = (A_HEADS, A_HEADS, A_HEADS, B_HEADS, B_KV_HEADS, B_KV_HEADS)
ODD_GROUPS = (C_HEADS, C_KV_HEADS, C_KV_HEADS)
EVEN_IN = sum(EVEN_GROUPS) * HEAD_DIM
ODD_IN = sum(ODD_GROUPS) * HEAD_DIM
MIX_OUT_EVEN = (A_HEADS + B_HEADS) * HEAD_DIM
MIX_OUT_ODD = C_HEADS * HEAD_DIM

kernel_name = 'hybrid_flow_backbone_step'


def rms_norm(x, g):
    xf = x.astype(jnp.float32)
    y = xf * lax.rsqrt(jnp.mean(xf * xf, axis=-1, keepdims=True) + RMS_EPS)
    return (y * g.astype(jnp.float32)).astype(x.dtype)


def modulation(cond, w_ada, b_ada):
    mod = jax.nn.silu(cond) @ w_ada + b_ada
    return jnp.split(mod[:, None, :], 6, axis=-1)


def modulate(x, g, shift, scale):
    return rms_norm(x, g) * (1 + scale) + shift


def axial_angles(n_tokens):
    t = jnp.arange(n_tokens)
    row = (t // GRID_W).astype(jnp.float32)
    col = (t % GRID_W).astype(jnp.float32)
    n_freq = HEAD_DIM // 4
    inv = ROPE_THETA ** (-jnp.arange(n_freq, dtype=jnp.float32) / n_freq)
    return row[:, None] * inv, col[:, None] * inv


def rotate(x, ang):
    x1, x2 = jnp.split(x, 2, axis=-1)
    cos = jnp.cos(ang)[None, :, None, :].astype(x.dtype)
    sin = jnp.sin(ang)[None, :, None, :].astype(x.dtype)
    return jnp.concatenate([x1 * cos - x2 * sin, x1 * sin + x2 * cos], axis=-1)


def axial_rope(x, ang_row, ang_col):
    half = HEAD_DIM // 2
    return jnp.concatenate([rotate(x[..., :half], ang_row), rotate(x[..., half:], ang_col)], axis=-1)


def split_heads(proj, head_counts):
    sizes = [h * HEAD_DIM for h in head_counts]
    cuts = [sum(sizes[:i + 1]) for i in range(len(sizes) - 1)]
    parts = jnp.split(proj, cuts, axis=-1)
    return [p.reshape(p.shape[:-1] + (h, HEAD_DIM)) for p, h in zip(parts, head_counts)]


def context_attention(q, k, v, sink=None):
    b, l, hq, hd = q.shape
    hkv = k.shape[2]
    g = hq // hkv
    qg = q.reshape(b, l, hkv, g, hd)
    s = jnp.einsum('bqkgd,bskd->bkgqs', qg, k).astype(jnp.float32) * ATTN_SCALE
    if sink is None:
        p = jax.nn.softmax(s, axis=-1)
    else:
        sk = jnp.broadcast_to(sink.reshape(hkv, g, 1, 1).astype(jnp.float32), s.shape[:-1] + (1,))
        p = jax.nn.softmax(jnp.concatenate([s, sk], axis=-1), axis=-1)[..., :-1]
    o = jnp.einsum('bkgqs,bskd->bqkgd', p.astype(v.dtype), v)
    return o.reshape(b, l, hq * hd)


def latent_dense_attention(q, k, v, k_ctx, v_ctx):
    b, t, hq, hd = q.shape
    hkv = k.shape[2]
    g = hq // hkv
    k_all = jnp.concatenate([k, k_ctx], axis=1)
    v_all = jnp.concatenate([v, v_ctx], axis=1)
    qb = q.reshape(b, t // Q_BLOCK, Q_BLOCK, hkv, g, hd).transpose(1, 0, 2, 3, 4, 5)

    def block(qi):
        s = jnp.einsum('bqkgd,bskd->bkgqs', qi, k_all).astype(jnp.float32) * ATTN_SCALE
        p = jax.nn.softmax(s, axis=-1)
        return jnp.einsum('bkgqs,bskd->bqkgd', p.astype(v_all.dtype), v_all)

    o = lax.map(block, qb)
    return o.transpose(1, 0, 2, 3, 4, 5).reshape(b, t, hq * hd)


def latent_window_attention(q, k, v, k_ctx, v_ctx, sink):
    b, t, hq, hd = q.shape
    hkv = k.shape[2]
    g = hq // hkv
    nb = t // Q_BLOCK
    pad = ((0, 0), (Q_BLOCK, Q_BLOCK), (0, 0), (0, 0))
    kp = jnp.pad(k, pad)
    vp = jnp.pad(v, pad)
    qb = q.reshape(b, nb, Q_BLOCK, hkv, g, hd).transpose(1, 0, 2, 3, 4, 5)
    sink_col = sink.reshape(hkv, g, 1, 1).astype(jnp.float32)
    n_band = 3 * Q_BLOCK

    def block(args):
        qi, bi = args
        start = bi * Q_BLOCK
        kb = lax.dynamic_slice_in_dim(kp, start, n_band, axis=1)
        vb = lax.dynamic_slice_in_dim(vp, start, n_band, axis=1)
        q_pos = start + jnp.arange(Q_BLOCK)
        k_pos = start - Q_BLOCK + jnp.arange(n_band)
        ok = (jnp.abs(q_pos[:, None] - k_pos[None, :]) <= C_WINDOW) & (k_pos[None, :] >= 0) & (k_pos[None, :] < t)
        s_lat = jnp.einsum('bqkgd,bskd->bkgqs', qi, kb).astype(jnp.float32) * ATTN_SCALE
        s_lat = jnp.where(ok, s_lat, -jnp.inf)
        s_ctx = jnp.einsum('bqkgd,bskd->bkgqs', qi, k_ctx).astype(jnp.float32) * ATTN_SCALE
        s_snk = jnp.broadcast_to(sink_col, s_lat.shape[:-1] + (1,))
        p = jax.nn.softmax(jnp.concatenate([s_lat, s_ctx, s_snk], axis=-1), axis=-1).astype(v.dtype)
        return (jnp.einsum('bkgqs,bskd->bqkgd', p[..., :n_band], vb)
                + jnp.einsum('bkgqs,bskd->bqkgd', p[..., n_band:-1], v_ctx))

    o = lax.map(block, (qb, jnp.arange(nb)))
    return o.transpose(1, 0, 2, 3, 4, 5).reshape(b, t, hq * hd)


def latent_neighborhood_attention(q, k, v, k_ctx, v_ctx, rpb):
    b, t, h, hd = q.shape
    rows = t // GRID_W
    kh = min(WIN_H_MAX, rows)
    ncb = GRID_W // NA_QCOLS
    qg = q.reshape(b, rows, ncb, NA_QCOLS, h, hd).transpose(1, 0, 2, 3, 4, 5)
    kg = k.reshape(b, rows, GRID_W, h, hd)
    vg = v.reshape(b, rows, GRID_W, h, hd)
    q_col = jnp.arange(ncb)[:, None] * NA_QCOLS + jnp.arange(NA_QCOLS)[None, :]
    band_start = jnp.clip(jnp.arange(ncb) * NA_QCOLS - WIN_W // 2, 0, GRID_W - NA_BAND)
    band_col = band_start[:, None] + jnp.arange(NA_BAND)[None, :]
    win_start = jnp.clip(q_col - WIN_W // 2, 0, GRID_W - WIN_W)
    bc = band_col[:, None, :]
    col_ok = (bc >= win_start[..., None]) & (bc < win_start[..., None] + WIN_W)
    dc_idx = jnp.clip(bc - q_col[..., None] + WIN_W - 1, 0, 2 * WIN_W - 2)
    col_bias = jnp.take(rpb, dc_idx, axis=2)
    ok = jnp.broadcast_to(col_ok[:, :, None, :], (ncb, NA_QCOLS, kh, NA_BAND)).reshape(ncb, NA_QCOLS, kh * NA_BAND)
    n_lat = kh * NA_BAND

    def row(args):
        qi, i = args
        rs = jnp.clip(i - kh // 2, 0, rows - kh)
        kr = lax.dynamic_slice_in_dim(kg, rs, kh, axis=1)
        vr = lax.dynamic_slice_in_dim(vg, rs, kh, axis=1)
        kband = kr[:, :, band_col].transpose(0, 2, 1, 3, 4, 5).reshape(b, ncb, n_lat, h, hd)
        vband = vr[:, :, band_col].transpose(0, 2, 1, 3, 4, 5).reshape(b, ncb, n_lat, h, hd)
        dr_idx = rs + jnp.arange(kh) - i + WIN_H_MAX - 1
        bias = jnp.take(col_bias, dr_idx, axis=1).transpose(0, 2, 3, 1, 4).reshape(h, ncb, NA_QCOLS, n_lat)
        s_lat = jnp.einsum('bcqhd,bckhd->bhcqk', qi, kband).astype(jnp.float32) * ATTN_SCALE + bias.astype(jnp.float32)
        s_lat = jnp.where(ok, s_lat, -jnp.inf)
        s_ctx = jnp.einsum('bcqhd,blhd->bhcql', qi, k_ctx).astype(jnp.float32) * ATTN_SCALE
        p = jax.nn.softmax(jnp.concatenate([s_lat, s_ctx], axis=-1), axis=-1).astype(v.dtype)
        return (jnp.einsum('bhcqk,bckhd->bcqhd', p[..., :n_lat], vband)
                + jnp.einsum('bhcql,blhd->bcqhd', p[..., n_lat:], v_ctx))

    o = lax.map(row, (qg, jnp.arange(rows)))
    return o.transpose(1, 0, 2, 3, 4, 5).reshape(b, t, h * hd)


def even_context(h, w_in, w_out, qn_g, kn_g):
    qa, ka, va, qb, kb, vb = split_heads(h @ w_in, EVEN_GROUPS)
    qb = rms_norm(qb, qn_g)
    kb = rms_norm(kb, kn_g)
    o = jnp.concatenate([context_attention(qa, ka, va), context_attention(qb, kb, vb)], axis=-1)
    return o @ w_out, ka, va, kb, vb


def even_latent(h, cak, cav, cbk, cbv, w_in, w_out, qn_g, kn_g, rpb, ang_r, ang_c):
    qa, ka, va, qb, kb, vb = split_heads(h @ w_in, EVEN_GROUPS)
    qb = axial_rope(rms_norm(qb, qn_g), ang_r, ang_c)
    kb = axial_rope(rms_norm(kb, kn_g), ang_r, ang_c)
    oa = latent_neighborhood_attention(qa, ka, va, cak, cav, rpb)
    ob = latent_dense_attention(qb, kb, vb, cbk, cbv)
    return jnp.concatenate([oa, ob], axis=-1) @ w_out


def odd_context(h, w_in, w_out, sink):
    qc, kc, vc = split_heads(h @ w_in, ODD_GROUPS)
    return context_attention(qc, kc, vc, sink) @ w_out, kc, vc


def odd_latent(h, cck, ccv, w_in, w_out, sink, ang_r, ang_c):
    qc, kc, vc = split_heads(h @ w_in, ODD_GROUPS)
    qc = axial_rope(qc, ang_r, ang_c)
    kc = axial_rope(kc, ang_r, ang_c)
    return latent_window_attention(qc, kc, vc, cck, ccv, sink) @ w_out


def moe_ffn(x, w_r, b_r, w_gu, b_gu, w_dn, b_dn):
    shp = x.shape
    xt = x.reshape(-1, shp[-1])
    n = xt.shape[0]
    logits = (xt @ w_r + b_r).astype(jnp.float32)
    top_val, top_idx = lax.top_k(logits, TOP_K)
    gates = jax.nn.softmax(top_val, axis=-1)
    n_assign = n * TOP_K
    flat_e = top_idx.reshape(-1)
    flat_tok = jnp.arange(n_assign, dtype=jnp.int32) // TOP_K
    flat_g = gates.reshape(-1)
    order = jnp.argsort(flat_e)
    sorted_e = flat_e[order]
    counts = jnp.bincount(flat_e, length=N_EXPERTS)
    starts = jnp.cumsum(counts) - counts
    pad_counts = (counts + MOE_BLOCK - 1) // MOE_BLOCK * MOE_BLOCK
    pad_ends = jnp.cumsum(pad_counts)
    pad_starts = pad_ends - pad_counts
    dest = pad_starts[sorted_e] + jnp.arange(n_assign) - starts[sorted_e]
    n_blocks = -(-n_assign // MOE_BLOCK) + N_EXPERTS
    cap = n_blocks * MOE_BLOCK
    buf_tok = jnp.full((cap,), n, jnp.int32).at[dest].set(flat_tok[order])
    buf_gate = jnp.zeros((cap,), jnp.float32).at[dest].set(flat_g[order])
    block_e = jnp.minimum(jnp.searchsorted(pad_ends, jnp.arange(n_blocks) * MOE_BLOCK, side='right'), N_EXPERTS - 1)
    x_pad = jnp.concatenate([xt, jnp.zeros((1, shp[-1]), xt.dtype)], axis=0)

    def expert_block(args):
        tok, e, gate = args
        xb = x_pad[tok]
        hgl = xb @ w_gu[e] + b_gu[e]
        glu, lin = jnp.split(hgl, 2, axis=-1)
        glu = jnp.minimum(glu, SWIGLU_LIMIT)
        lin = jnp.clip(lin, -SWIGLU_LIMIT, SWIGLU_LIMIT)
        act = glu * jax.nn.sigmoid(SWIGLU_ALPHA * glu) * (lin + 1)
        return (act @ w_dn[e] + b_dn[e]) * gate[:, None].astype(xb.dtype)

    outs = lax.map(expert_block, (buf_tok.reshape(n_blocks, MOE_BLOCK), block_e, buf_gate.reshape(n_blocks, MOE_BLOCK)))
    y = jax.ops.segment_sum(outs.reshape(cap, shp[-1]), buf_tok, num_segments=n + 1)[:n]
    return y.reshape(shp)


def setup_inputs(seed: int = 0) -> dict:
    key = jax.random.key(seed)
    ks = jax.random.split(key, 32)
    f32 = jnp.float32
    d = D_MODEL
    hd = HEAD_DIM

    def nrm(k, shape, s):
        return jax.random.normal(k, shape, f32) * s

    return {
        'x_prompt': nrm(ks[0], (BATCH, SEQ, d), 1.0),
        'x_sample': nrm(ks[1], (DEC_BATCH, DEC_SEQ, d), 1.0),
        'cache_a_k': nrm(ks[2], (DEC_BATCH, N_EVEN, PAST_LEN, A_HEADS, hd), 1.0),
        'cache_a_v': nrm(ks[3], (DEC_BATCH, N_EVEN, PAST_LEN, A_HEADS, hd), 1.0),
        'cache_b_k': nrm(ks[4], (DEC_BATCH, N_EVEN, PAST_LEN, B_KV_HEADS, hd), 1.0),
        'cache_b_v': nrm(ks[5], (DEC_BATCH, N_EVEN, PAST_LEN, B_KV_HEADS, hd), 1.0),
        'cache_c_k': nrm(ks[6], (DEC_BATCH, N_ODD, PAST_LEN, C_KV_HEADS, hd), 1.0),
        'cache_c_v': nrm(ks[7], (DEC_BATCH, N_ODD, PAST_LEN, C_KV_HEADS, hd), 1.0),
        'c': nrm(ks[8], (DEC_BATCH, d), 1.0),
        'c_ctx': nrm(ks[9], (d,), 1.0),
        'w_ada': nrm(ks[10], (DEPTH, d, 6 * d), 0.5 * d ** -0.5),
        'b_ada': nrm(ks[11], (DEPTH, 6 * d), 0.02),
        'g_pre_mix': 1.0 + nrm(ks[12], (DEPTH, d), 0.05),
        'g_post_mix': 1.0 + nrm(ks[13], (DEPTH, d), 0.05),
        'g_pre_ffn': 1.0 + nrm(ks[14], (DEPTH, d), 0.05),
        'g_post_ffn': 1.0 + nrm(ks[15], (DEPTH, d), 0.05),
        'w_in_even': nrm(ks[16], (N_EVEN, d, EVEN_IN), d ** -0.5),
        'w_out_even': nrm(ks[17], (N_EVEN, MIX_OUT_EVEN, d), MIX_OUT_EVEN ** -0.5),
        'rpb_a': nrm(ks[18], (N_EVEN, A_HEADS, 2 * WIN_H_MAX - 1, 2 * WIN_W - 1), 0.1),
        'q_norm_b': 1.0 + nrm(ks[19], (N_EVEN, hd), 0.05),
        'k_norm_b': 1.0 + nrm(ks[20], (N_EVEN, hd), 0.05),
        'w_in_odd': nrm(ks[21], (N_ODD, d, ODD_IN), d ** -0.5),
        'w_out_odd': nrm(ks[22], (N_ODD, MIX_OUT_ODD, d), MIX_OUT_ODD ** -0.5),
        'sink_c': nrm(ks[23], (N_ODD, C_HEADS), 0.5),
        'w_router': nrm(ks[24], (DEPTH, d, N_EXPERTS), d ** -0.5),
        'b_router': nrm(ks[25], (DEPTH, N_EXPERTS), 0.01),
        'w_gate_up': nrm(ks[26], (DEPTH, N_EXPERTS, d, 2 * MOE_D_FF), d ** -0.5),
        'b_gate_up': nrm(ks[27], (DEPTH, N_EXPERTS, 2 * MOE_D_FF), 0.01),
        'w_down': nrm(ks[28], (DEPTH, N_EXPERTS, MOE_D_FF, d), MOE_D_FF ** -0.5),
        'b_down': nrm(ks[29], (DEPTH, N_EXPERTS, d), 0.01),
    }


def reference(x_prompt, x_sample, cache_a_k, cache_a_v, cache_b_k, cache_b_v, cache_c_k, cache_c_v,
              c, c_ctx, w_ada, b_ada, g_pre_mix, g_post_mix, g_pre_ffn, g_post_ffn,
              w_in_even, w_out_even, rpb_a, q_norm_b, k_norm_b, w_in_odd, w_out_odd, sink_c,
              w_router, b_router, w_gate_up, b_gate_up, w_down, b_down):
    ang_r, ang_c = axial_angles(x_sample.shape[1])
    xp, xs = x_prompt, x_sample
    st_ak, st_av, st_bk, st_bv, st_ck, st_cv = [], [], [], [], [], []
    for l in range(DEPTH):
        j = l // 2
        mp = modulation(c_ctx[None, :], w_ada[l], b_ada[l])
        ms = modulation(c, w_ada[l], b_ada[l])
        hp = modulate(xp, g_pre_mix[l], mp[0], mp[1])
        hs = modulate(xs, g_pre_mix[l], ms[0], ms[1])
        if l % 2 == 0:
            yp, ka, va, kb, vb = even_context(hp, w_in_even[j], w_out_even[j], q_norm_b[j], k_norm_b[j])
            st_ak.append(ka)
            st_av.append(va)
            st_bk.append(kb)
            st_bv.append(vb)
            ys = even_latent(hs, cache_a_k[:, j], cache_a_v[:, j], cache_b_k[:, j], cache_b_v[:, j],
                             w_in_even[j], w_out_even[j], q_norm_b[j], k_norm_b[j], rpb_a[j], ang_r, ang_c)
        else:
            yp, kc, vc = odd_context(hp, w_in_odd[j], w_out_odd[j], sink_c[j])
            st_ck.append(kc)
            st_cv.append(vc)
            ys = odd_latent(hs, cache_c_k[:, j], cache_c_v[:, j], w_in_odd[j], w_out_odd[j], sink_c[j], ang_r, ang_c)
        xp = xp + mp[2] * rms_norm(yp, g_post_mix[l])
        xs = xs + ms[2] * rms_norm(ys, g_post_mix[l])
        hp = modulate(xp, g_pre_ffn[l], mp[3], mp[4])
        hs = modulate(xs, g_pre_ffn[l], ms[3], ms[4])
        fp = moe_ffn(hp, w_router[l], b_router[l], w_gate_up[l], b_gate_up[l], w_down[l], b_down[l])
        fs = moe_ffn(hs, w_router[l], b_router[l], w_gate_up[l], b_gate_up[l], w_down[l], b_down[l])
        xp = xp + mp[5] * rms_norm(fp, g_post_ffn[l])
        xs = xs + ms[5] * rms_norm(fs, g_post_ffn[l])
    y_prompt = xp
    y_sample = xs
    state_a_k = jnp.stack(st_ak, axis=1)
    state_a_v = jnp.stack(st_av, axis=1)
    state_b_k = jnp.stack(st_bk, axis=1)
    state_b_v = jnp.stack(st_bv, axis=1)
    state_c_k = jnp.stack(st_ck, axis=1)
    state_c_v = jnp.stack(st_cv, axis=1)
    return (y_prompt, y_sample, state_a_k, state_a_v, state_b_k, state_b_v, state_c_k, state_c_v)
```

```python
import functools

import jax
import jax.numpy as jnp
from jax import lax
from jax.experimental import pallas as pl
from jax.experimental.pallas import tpu as pltpu

F32 = jnp.float32
BF16 = jnp.bfloat16
I32 = jnp.int32

D_MODEL = 4096
BATCH = 16
SEQ = 256
DEPTH = 2
DEC_BATCH = 4
DEC_SEQ = 2048
PAST_LEN = 512
GRID_W = 64
GRID_ROWS = DEC_SEQ // GRID_W
HEAD_DIM = 128
A_HEADS = 16
B_HEADS = 16
B_KV_HEADS = 4
C_HEADS = 32
C_KV_HEADS = 4
WIN_H = 8
WIN_W = 16
C_WINDOW = 128
N_EXPERTS = 32
TOP_K = 4
MOE_D_FF = 2048
SWIGLU_LIMIT = 7.0
SWIGLU_ALPHA = 1.702
ROPE_THETA = 10000.0
RMS_EPS = 1e-6
ATTN_SCALE = HEAD_DIM ** -0.5
EVEN_IN = (3 * A_HEADS + B_HEADS + 2 * B_KV_HEADS) * HEAD_DIM
ODD_IN = (C_HEADS + 2 * C_KV_HEADS) * HEAD_DIM

N_PROMPT = BATCH * SEQ
N_SAMPLE = DEC_BATCH * DEC_SEQ
N_TOK = N_PROMPT + N_SAMPLE
PB_TOK = DEC_SEQ
N_PB = N_TOK // PB_TOK
N_PROMPT_PB = N_PROMPT // PB_TOK

LANES = 128
MIB = 1024 * 1024

MOE_R = 1024
MOE_SUB = 256
N_ASSIGN = N_TOK * TOP_K
MOE_NSB = N_ASSIGN // MOE_R + N_EXPERTS
MOE_CAP = MOE_NSB * MOE_R
MOE_FC = 256
MOE_NC = 512

NEG_BIG = -1e30


def _params(sem, vmem_mib):
    return pltpu.CompilerParams(dimension_semantics=sem, vmem_limit_bytes=vmem_mib * MIB)


def _ada_kernel(c_ref, w_ref, b_ref, o_ref):
    c = c_ref[...]
    s = (c / (1.0 + jnp.exp(-c))).astype(BF16)
    o_ref[...] = jnp.dot(s, w_ref[...].astype(BF16), preferred_element_type=F32) + b_ref[...]


def _ada_mod(cond8, w_ada, b_ada):
    bn = 512
    n_out = 6 * D_MODEL
    return pl.pallas_call(
        _ada_kernel,
        grid=(DEPTH, n_out // bn),
        in_specs=[
            pl.BlockSpec((8, D_MODEL), lambda l, j: (0, 0)),
            pl.BlockSpec((None, D_MODEL, bn), lambda l, j: (l, 0, j)),
            pl.BlockSpec((None, 1, bn), lambda l, j: (l, 0, j)),
        ],
        out_specs=pl.BlockSpec((None, 8, bn), lambda l, j: (l, 0, j)),
        out_shape=jax.ShapeDtypeStruct((DEPTH, 8, n_out), F32),
        compiler_params=_params(("arbitrary", "arbitrary"), 40),
    )(cond8, w_ada, b_ada.reshape(DEPTH, 1, n_out))


IN_BM = 512
IN_BN = 512
MODE_PLAIN, MODE_NORM_Q, MODE_NORM_K, MODE_ROPE = 0, 1, 2, 3


def _inproj_kernel(mode_ref, x_ref, g_ref, sh_ref, sc_ref, w_ref, gain_ref, cos_ref, sin_ref,
                   o_ref, h_ref):
    j = pl.program_id(1)

    @pl.when(j == 0)
    def _():
        x = x_ref[...]
        r = lax.rsqrt(jnp.mean(x * x, axis=-1, keepdims=True) + RMS_EPS)
        h = (x * r * g_ref[...]) * (1.0 + sc_ref[...]) + sh_ref[...]
        h_ref[...] = h.astype(BF16)

    acc = jnp.dot(h_ref[...], w_ref[...], preferred_element_type=F32)
    mode = mode_ref[j]

    @pl.when(mode == MODE_PLAIN)
    def _():
        o_ref[...] = acc

    @pl.when(mode != MODE_PLAIN)
    def _():
        gain = jnp.where(mode == MODE_NORM_Q, gain_ref[0:1, :], gain_ref[1:2, :])
        cos = cos_ref[...]
        sin = sin_ref[...]
        lane = lax.broadcasted_iota(I32, (IN_BM, LANES), 1)
        first = (lane % 64) < 32
        for hh in range(IN_BN // HEAD_DIM):
            xh = acc[:, hh * HEAD_DIM:(hh + 1) * HEAD_DIM]
            rs = lax.rsqrt(jnp.mean(xh * xh, axis=-1, keepdims=True) + RMS_EPS)
            y = jnp.where(mode == MODE_ROPE, xh, xh * rs * gain)
            sw = jnp.where(first, pltpu.roll(y, 96, 1), pltpu.roll(y, 32, 1))
            o_ref[:, hh * HEAD_DIM:(hh + 1) * HEAD_DIM] = y * cos + sw * sin


def _in_proj(x, g_pre, shift, scale, w_bf16, modes, gains, cos_tab, sin_tab):
    n = w_bf16.shape[1]
    n_i = N_TOK // IN_BM
    n_prompt_blocks = N_PROMPT // IN_BM
    blocks_per_seq = DEC_SEQ // IN_BM

    def tab_idx(i, j, m):
        return (jnp.where(i < n_prompt_blocks, 0, 1 + (i - n_prompt_blocks) % blocks_per_seq), 0)

    grid_spec = pltpu.PrefetchScalarGridSpec(
        num_scalar_prefetch=1,
        grid=(n_i, n // IN_BN),
        in_specs=[
            pl.BlockSpec((IN_BM, D_MODEL), lambda i, j, m: (i, 0)),
            pl.BlockSpec((1, D_MODEL), lambda i, j, m: (0, 0)),
            pl.BlockSpec((None, 1, D_MODEL), lambda i, j, m: (i * IN_BM // PB_TOK, 0, 0)),
            pl.BlockSpec((None, 1, D_MODEL), lambda i, j, m: (i * IN_BM // PB_TOK, 0, 0)),
            pl.BlockSpec((D_MODEL, IN_BN), lambda i, j, m: (0, j)),
            pl.BlockSpec((8, LANES), lambda i, j, m: (0, 0)),
            pl.BlockSpec((IN_BM, LANES), tab_idx),
            pl.BlockSpec((IN_BM, LANES), tab_idx),
        ],
        out_specs=pl.BlockSpec((IN_BM, IN_BN), lambda i, j, m: (i, j)),
        scratch_shapes=[pltpu.VMEM((IN_BM, D_MODEL), BF16)],
    )
    return pl.pallas_call(
        _inproj_kernel,
        grid_spec=grid_spec,
        out_shape=jax.ShapeDtypeStruct((N_TOK, n), F32),
        compiler_params=_params(("arbitrary", "arbitrary"), 48),
    )(modes, x, g_pre.reshape(1, D_MODEL), shift, scale, w_bf16, gains, cos_tab, sin_tab)


def _rope_tables():
    t = jnp.arange(DEC_SEQ)
    row = (t // GRID_W).astype(F32)
    col = (t % GRID_W).astype(F32)
    n_freq = HEAD_DIM // 4
    inv = ROPE_THETA ** (-jnp.arange(n_freq, dtype=F32) / n_freq)
    ar = row[:, None] * inv
    ac = col[:, None] * inv
    cos = jnp.concatenate([jnp.cos(ar), jnp.cos(ar), jnp.cos(ac), jnp.cos(ac)], axis=-1)
    sin = jnp.concatenate([-jnp.sin(ar), jnp.sin(ar), -jnp.sin(ac), jnp.sin(ac)], axis=-1)
    cos = jnp.concatenate([jnp.ones((IN_BM, HEAD_DIM), F32), cos], axis=0)
    sin = jnp.concatenate([jnp.zeros((IN_BM, HEAD_DIM), F32), sin], axis=0)
    return cos, sin


MM_BM = 1024
MM_BN = 512


def _mm_kernel(a_ref, w_ref, o_ref):
    o_ref[...] = jnp.dot(a_ref[...], w_ref[...], preferred_element_type=F32)


def _matmul(a_bf16, w_bf16):
    m, k = a_bf16.shape
    n = w_bf16.shape[1]
    return pl.pallas_call(
        _mm_kernel,
        grid=(m // MM_BM, n // MM_BN),
        in_specs=[pl.BlockSpec((MM_BM, k), lambda i, j: (i, 0)),
                  pl.BlockSpec((k, MM_BN), lambda i, j: (0, j))],
        out_specs=pl.BlockSpec((MM_BM, MM_BN), lambda i, j: (i, j)),
        out_shape=jax.ShapeDtypeStruct((m, n), F32),
        compiler_params=_params(("arbitrary", "arbitrary"), 48),
    )(a_bf16, w_bf16)


def _softmax_pv(s_list, v_list, sink=None):
    m = jnp.max(s_list[0], axis=-1, keepdims=True)
    for s in s_list[1:]:
        m = jnp.maximum(m, jnp.max(s, axis=-1, keepdims=True))
    if sink is not None:
        m = jnp.maximum(m, sink)
    den = None
    out = None
    for s, v in zip(s_list, v_list):
        p = jnp.exp(s - m)
        d = jnp.sum(p, axis=-1, keepdims=True)
        o = jnp.dot(p.astype(BF16), v, preferred_element_type=F32)
        den = d if den is None else den + d
        out = o if out is None else out + o
    if sink is not None:
        den = den + jnp.exp(sink - m)
    return out / den


def _qk(q, k):
    return lax.dot_general(q, k, (((1,), (1,)), ((), ())), preferred_element_type=F32) * ATTN_SCALE


def _ctx_attn_kernel(sink_ref, q_ref, k_ref, v_ref, o_ref, *, nkv, g):
    kb = pl.program_id(1)
    for kv in range(nkv):
        k = k_ref[:, kv * HEAD_DIM:(kv + 1) * HEAD_DIM].astype(BF16)
        v = v_ref[:, kv * HEAD_DIM:(kv + 1) * HEAD_DIM].astype(BF16)
        for gi in range(g):
            c0 = (kv * g + gi) * HEAD_DIM
            q = q_ref[:, c0:c0 + HEAD_DIM].astype(BF16)
            sink = sink_ref[(kb * nkv + kv) * g + gi]
            o = _softmax_pv([_qk(q, k)], [v], sink)
            o_ref[:, c0:c0 + HEAD_DIM] = o.astype(o_ref.dtype)


def _ctx_attention(proj, sink, q_col, k_col, v_col, hq, hkv, nkv):
    g = hq // hkv
    qw = nkv * g * HEAD_DIM
    kw = nkv * HEAD_DIM
    return pl.pallas_call(
        functools.partial(_ctx_attn_kernel, nkv=nkv, g=g),
        grid=(BATCH, hkv // nkv),
        in_specs=[
            pl.BlockSpec(memory_space=pltpu.SMEM),
            pl.BlockSpec((SEQ, qw), lambda b, h: (b, q_col // qw + h)),
            pl.BlockSpec((SEQ, kw), lambda b, h: (b, k_col // kw + h)),
            pl.BlockSpec((SEQ, kw), lambda b, h: (b, v_col // kw + h)),
        ],
        out_specs=pl.BlockSpec((SEQ, qw), lambda b, h: (b, h)),
        out_shape=jax.ShapeDtypeStruct((N_PROMPT, hq * HEAD_DIM), BF16),
        compiler_params=_params(("arbitrary", "arbitrary"), 40),
    )(sink, proj, proj, proj)


DENSE_BQ = 256


def _dense_attn_kernel(q_ref, k_ref, v_ref, kc_ref, vc_ref, o_ref, *, g):
    k = k_ref[...].astype(BF16)
    v = v_ref[...].astype(BF16)
    kc = kc_ref[...].astype(BF16)
    vc = vc_ref[...].astype(BF16)
    for gi in range(g):
        c0 = gi * HEAD_DIM
        q = q_ref[:, c0:c0 + HEAD_DIM].astype(BF16)
        o = _softmax_pv([_qk(q, k), _qk(q, kc)], [v, vc])
        o_ref[:, c0:c0 + HEAD_DIM] = o.astype(o_ref.dtype)


def _dense_attention(proj, cache_k, cache_v, layer_j):
    g = B_HEADS // B_KV_HEADS
    qw = g * HEAD_DIM
    nq = DEC_SEQ // DENSE_BQ
    q_col = 3 * A_HEADS * HEAD_DIM
    k_col = q_col + B_HEADS * HEAD_DIM
    v_col = k_col + B_KV_HEADS * HEAD_DIM
    row0 = N_PROMPT // DENSE_BQ
    ck = cache_k.reshape(DEC_BATCH, -1, PAST_LEN, B_KV_HEADS * HEAD_DIM)
    cv = cache_v.reshape(DEC_BATCH, -1, PAST_LEN, B_KV_HEADS * HEAD_DIM)
    return pl.pallas_call(
        functools.partial(_dense_attn_kernel, g=g),
        grid=(DEC_BATCH, B_KV_HEADS, nq),
        in_specs=[
            pl.BlockSpec((DENSE_BQ, qw), lambda b, h, i: (row0 + b * nq + i, q_col // qw + h)),
            pl.BlockSpec((DEC_SEQ, HEAD_DIM), lambda b, h, i: (N_PROMPT_PB + b, k_col // HEAD_DIM + h)),
            pl.BlockSpec((DEC_SEQ, HEAD_DIM), lambda b, h, i: (N_PROMPT_PB + b, v_col // HEAD_DIM + h)),
            pl.BlockSpec((None, None, PAST_LEN, HEAD_DIM), lambda b, h, i: (b, layer_j, 0, h)),
            pl.BlockSpec((None, None, PAST_LEN, HEAD_DIM), lambda b, h, i: (b, layer_j, 0, h)),
        ],
        out_specs=pl.BlockSpec((DENSE_BQ, qw), lambda b, h, i: (b * nq + i, h)),
        out_shape=jax.ShapeDtypeStruct((N_SAMPLE, B_HEADS * HEAD_DIM), BF16),
        compiler_params=_params(("arbitrary", "arbitrary", "arbitrary"), 48),
    )(proj, proj, proj, ck, cv)


WIN_BQ = 256
WIN_KEYS = WIN_BQ + 2 * C_WINDOW


def _window_attn_kernel(sink_ref, q_ref, k_ref, v_ref, kc_ref, vc_ref, o_ref, *, g):
    h = pl.program_id(1)
    i = pl.program_id(2)
    q0 = i * WIN_BQ
    k0 = pl.multiple_of(jnp.clip(q0 - C_WINDOW, 0, DEC_SEQ - WIN_KEYS), C_WINDOW)
    k = k_ref[pl.ds(k0, WIN_KEYS), :].astype(BF16)
    v = v_ref[pl.ds(k0, WIN_KEYS), :].astype(BF16)
    kc = kc_ref[...].astype(BF16)
    vc = vc_ref[...].astype(BF16)
    q_pos = q0 + lax.broadcasted_iota(I32, (WIN_BQ, WIN_KEYS), 0)
    k_pos = k0 + lax.broadcasted_iota(I32, (WIN_BQ, WIN_KEYS), 1)
    ok = jnp.abs(q_pos - k_pos) <= C_WINDOW
    for gi in range(g):
        c0 = gi * HEAD_DIM
        q = q_ref[:, c0:c0 + HEAD_DIM].astype(BF16)
        s_lat = jnp.where(ok, _qk(q, k), -jnp.inf)
        o = _softmax_pv([s_lat, _qk(q, kc)], [v, vc], sink_ref[h * g + gi])
        o_ref[:, c0:c0 + HEAD_DIM] = o.astype(o_ref.dtype)


def _window_attention(proj, cache_k, cache_v, sink, layer_j):
    g = C_HEADS // C_KV_HEADS
    qw = g * HEAD_DIM
    nq = DEC_SEQ // WIN_BQ
    k_col = C_HEADS * HEAD_DIM
    v_col = k_col + C_KV_HEADS * HEAD_DIM
    row0 = N_PROMPT // WIN_BQ
    ck = cache_k.reshape(DEC_BATCH, -1, PAST_LEN, C_KV_HEADS * HEAD_DIM)
    cv = cache_v.reshape(DEC_BATCH, -1, PAST_LEN, C_KV_HEADS * HEAD_DIM)
    return pl.pallas_call(
        functools.partial(_window_attn_kernel, g=g),
        grid=(DEC_BATCH, C_KV_HEADS, nq),
        in_specs=[
            pl.BlockSpec(memory_space=pltpu.SMEM),
            pl.BlockSpec((WIN_BQ, qw), lambda b, h, i: (row0 + b * nq + i, h)),
            pl.BlockSpec((DEC_SEQ, HEAD_DIM), lambda b, h, i: (N_PROMPT_PB + b, k_col // HEAD_DIM + h)),
            pl.BlockSpec((DEC_SEQ, HEAD_DIM), lambda b, h, i: (N_PROMPT_PB + b, v_col // HEAD_DIM + h)),
            pl.BlockSpec((None, None, PAST_LEN, HEAD_DIM), lambda b, h, i: (b, layer_j, 0, h)),
            pl.BlockSpec((None, None, PAST_LEN, HEAD_DIM), lambda b, h, i: (b, layer_j, 0, h)),
        ],
        out_specs=pl.BlockSpec((WIN_BQ, qw), lambda b, h, i: (b * nq + i, h)),
        out_shape=jax.ShapeDtypeStruct((N_SAMPLE, C_HEADS * HEAD_DIM), BF16),
        compiler_params=_params(("arbitrary", "arbitrary", "arbitrary"), 48),
    )(sink, proj, proj, proj, ck, cv)


NA_HC = 4
NA_KEYS = WIN_H * GRID_W


def _na_row_start(i):
    return jnp.clip(i - WIN_H // 2, 0, GRID_ROWS - WIN_H)


def _na_attn_kernel(q_ref, k_ref, v_ref, kc_ref, vc_ref, bias_ref, o_ref):
    i = pl.program_id(2)
    r0 = pl.multiple_of(_na_row_start(i) * GRID_W, GRID_W)
    for hh in range(NA_HC):
        cs = slice(hh * HEAD_DIM, (hh + 1) * HEAD_DIM)
        q = q_ref[:, cs].astype(BF16)
        k = k_ref[pl.ds(r0, NA_KEYS), cs].astype(BF16)
        v = v_ref[pl.ds(r0, NA_KEYS), cs].astype(BF16)
        kc = kc_ref[:, cs].astype(BF16)
        vc = vc_ref[:, cs].astype(BF16)
        s_lat = _qk(q, k) + bias_ref[hh]
        o = _softmax_pv([s_lat, _qk(q, kc)], [v, vc])
        o_ref[:, cs] = o.astype(o_ref.dtype)


def _na_bias_table(rpb):
    qc = jnp.arange(GRID_W)[:, None]
    kc = jnp.arange(GRID_W)[None, :]
    ws = jnp.clip(qc - WIN_W // 2, 0, GRID_W - WIN_W)
    ok = (kc >= ws) & (kc < ws + WIN_W)
    dc = jnp.clip(kc - qc + WIN_W - 1, 0, 2 * WIN_W - 2)
    t = rpb[:, :, dc]
    t = jnp.where(ok[None, None], t, -jnp.inf)
    dr = jnp.arange(WIN_H)[:, None] + jnp.arange(WIN_H)[None, :]
    b = t[:, dr]
    b = b.transpose(0, 1, 3, 2, 4).reshape(A_HEADS, WIN_H, GRID_W, NA_KEYS)
    return b.astype(F32)


def _na_attention(proj, cache_k, cache_v, rpb, layer_j):
    cw = NA_HC * HEAD_DIM
    k_col = A_HEADS * HEAD_DIM
    v_col = 2 * A_HEADS * HEAD_DIM
    row0 = N_PROMPT // GRID_W
    ck = cache_k.reshape(DEC_BATCH, -1, PAST_LEN, A_HEADS * HEAD_DIM)
    cv = cache_v.reshape(DEC_BATCH, -1, PAST_LEN, A_HEADS * HEAD_DIM)
    bias = _na_bias_table(rpb)

    def bias_idx(b, h, i):
        return (h, _na_row_start(i) - i + WIN_H - 1, 0, 0)

    return pl.pallas_call(
        _na_attn_kernel,
        grid=(DEC_BATCH, A_HEADS // NA_HC, GRID_ROWS),
        in_specs=[
            pl.BlockSpec((GRID_W, cw), lambda b, h, i: (row0 + b * GRID_ROWS + i, h)),
            pl.BlockSpec((DEC_SEQ, cw), lambda b, h, i: (N_PROMPT_PB + b, k_col // cw + h)),
            pl.BlockSpec((DEC_SEQ, cw), lambda b, h, i: (N_PROMPT_PB + b, v_col // cw + h)),
            pl.BlockSpec((None, None, PAST_LEN, cw), lambda b, h, i: (b, layer_j, 0, h)),
            pl.BlockSpec((None, None, PAST_LEN, cw), lambda b, h, i: (b, layer_j, 0, h)),
            pl.BlockSpec((NA_HC, None, GRID_W, NA_KEYS), bias_idx),
        ],
        out_specs=pl.BlockSpec((GRID_W, cw), lambda b, h, i: (b * GRID_ROWS + i, h)),
        out_shape=jax.ShapeDtypeStruct((N_SAMPLE, A_HEADS * HEAD_DIM), BF16),
        compiler_params=_params(("arbitrary", "arbitrary", "arbitrary"), 48),
    )(proj, proj, proj, ck, cv, bias)


EP_BM = 256


def _rms(x):
    return x * lax.rsqrt(jnp.mean(x * x, axis=-1, keepdims=True) + RMS_EPS)


def _post_mix_kernel(y_ref, x_ref, gate_ref, gpost_ref, gpre_ref, sh_ref, sc_ref,
                     wr_hi_ref, wr_lo_ref, br_ref, xo_ref, h_ref, ti_ref, tg_ref):
    x_new = x_ref[...] + gate_ref[...] * (_rms(y_ref[...]) * gpost_ref[...])
    xo_ref[...] = x_new
    h = (_rms(x_new) * gpre_ref[...]) * (1.0 + sc_ref[...]) + sh_ref[...]
    h_hi = h.astype(BF16)
    h_ref[...] = h_hi
    h_lo = (h - h_hi.astype(F32)).astype(BF16)
    w_hi = wr_hi_ref[...]
    logits = (jnp.dot(h_hi, w_hi, preferred_element_type=F32)
              + jnp.dot(h_lo, w_hi, preferred_element_type=F32)
              + jnp.dot(h_hi, wr_lo_ref[...], preferred_element_type=F32)) + br_ref[...]
    lane = lax.broadcasted_iota(I32, (EP_BM, LANES), 1).astype(F32)
    vals, idxs = [], []
    l = logits
    for _ in range(TOP_K):
        m = jnp.max(l, axis=-1, keepdims=True)
        idx = jnp.min(jnp.where(l == m, lane, float(LANES)), axis=-1, keepdims=True)
        vals.append(m)
        idxs.append(idx)
        l = jnp.where(lane == idx, -jnp.inf, l)
    es = [jnp.exp(v - vals[0]) for v in vals]
    den = es[0] + es[1] + es[2] + es[3]
    ti = jnp.zeros((EP_BM, LANES), F32)
    tg = jnp.zeros((EP_BM, LANES), F32)
    for k in range(TOP_K):
        ti = jnp.where(lane == float(k), idxs[k], ti)
        tg = jnp.where(lane == float(k), es[k] / den, tg)
    ti_ref[...] = ti.astype(I32)
    tg_ref[...] = tg


def _post_mix(y, x, gate, g_post, g_pre, shift, scale, wr_hi, wr_lo, br):
    row = pl.BlockSpec((EP_BM, D_MODEL), lambda i: (i, 0))
    vec = pl.BlockSpec((1, D_MODEL), lambda i: (0, 0))
    mod = pl.BlockSpec((None, 1, D_MODEL), lambda i: (i * EP_BM // PB_TOK, 0, 0))
    wr = pl.BlockSpec((D_MODEL, LANES), lambda i: (0, 0))
    small = pl.BlockSpec((EP_BM, LANES), lambda i: (i, 0))
    return pl.pallas_call(
        _post_mix_kernel,
        grid=(N_TOK // EP_BM,),
        in_specs=[row, row, mod, vec, vec, mod, mod, wr, wr, pl.BlockSpec((1, LANES), lambda i: (0, 0))],
        out_specs=[row, row, small, small],
        out_shape=[jax.ShapeDtypeStruct((N_TOK, D_MODEL), F32),
                   jax.ShapeDtypeStruct((N_TOK, D_MODEL), BF16),
                   jax.ShapeDtypeStruct((N_TOK, LANES), I32),
                   jax.ShapeDtypeStruct((N_TOK, LANES), F32)],
        compiler_params=_params(("arbitrary",), 48),
    )(y, x, gate, g_post.reshape(1, D_MODEL), g_pre.reshape(1, D_MODEL), shift, scale, wr_hi, wr_lo, br)


def _post_ffn_kernel(y_ref, x_ref, gate_ref, gpost_ref, xo_ref):
    xo_ref[...] = x_ref[...] + gate_ref[...] * (_rms(y_ref[...]) * gpost_ref[...])


def _post_ffn(y, x, gate, g_post):
    row = pl.BlockSpec((EP_BM, D_MODEL), lambda i: (i, 0))
    return pl.pallas_call(
        _post_ffn_kernel,
        grid=(N_TOK // EP_BM,),
        in_specs=[row, row, pl.BlockSpec((None, 1, D_MODEL), lambda i: (i * EP_BM // PB_TOK, 0, 0)),
                  pl.BlockSpec((1, D_MODEL), lambda i: (0, 0))],
        out_specs=row,
        out_shape=jax.ShapeDtypeStruct((N_TOK, D_MODEL), F32),
        compiler_params=_params(("arbitrary",), 40),
    )(y, x, gate, g_post.reshape(1, D_MODEL))


def _moe_up_kernel(sbe_ref, sbi_ref, nsub_ref, x_ref, wg_ref, wl_ref, bg_ref, bl_ref, o_ref, wgb, wlb):
    s = pl.program_id(0)
    nsub = nsub_ref[s]

    @pl.when(nsub > 0)
    def _():
        wgb[...] = wg_ref[...].astype(BF16)
        wlb[...] = wl_ref[...].astype(BF16)

        def body(r, carry):
            rows = pl.ds(pl.multiple_of(r * MOE_SUB, MOE_SUB), MOE_SUB)
            xs = x_ref[rows, :]
            glu = jnp.dot(xs, wgb[...], preferred_element_type=F32) + bg_ref[...]
            lin = jnp.dot(xs, wlb[...], preferred_element_type=F32) + bl_ref[...]
            glu = jnp.minimum(glu, SWIGLU_LIMIT)
            lin = jnp.clip(lin, -SWIGLU_LIMIT, SWIGLU_LIMIT)
            act = glu / (1.0 + jnp.exp(-SWIGLU_ALPHA * glu)) * (lin + 1.0)
            o_ref[rows, :] = act.astype(BF16)
            return carry

        lax.fori_loop(0, nsub, body, 0)


def _moe_up(x_sorted, sbe, sbi, nsub, w_gate_up, b_gate_up, layer):
    n_c = MOE_D_FF // MOE_FC
    last_c = n_c - 1

    def c_eff(s, c, nsub_ref):
        return jnp.where(nsub_ref[s] > 0, c, last_c)

    grid_spec = pltpu.PrefetchScalarGridSpec(
        num_scalar_prefetch=3,
        grid=(MOE_NSB, n_c),
        in_specs=[
            pl.BlockSpec((MOE_R, D_MODEL), lambda s, c, e, i, n: (i[s], 0)),
            pl.BlockSpec((None, None, D_MODEL, MOE_FC), lambda s, c, e, i, n: (layer, e[s], 0, c_eff(s, c, n))),
            pl.BlockSpec((None, None, D_MODEL, MOE_FC),
                         lambda s, c, e, i, n: (layer, e[s], 0, n_c + c_eff(s, c, n))),
            pl.BlockSpec((None, None, 1, MOE_FC), lambda s, c, e, i, n: (layer, e[s], 0, c_eff(s, c, n))),
            pl.BlockSpec((None, None, 1, MOE_FC), lambda s, c, e, i, n: (layer, e[s], 0, n_c + c_eff(s, c, n))),
        ],
        out_specs=pl.BlockSpec((MOE_R, MOE_FC), lambda s, c, e, i, n: (i[s], c_eff(s, c, n))),
        scratch_shapes=[pltpu.VMEM((D_MODEL, MOE_FC), BF16), pltpu.VMEM((D_MODEL, MOE_FC), BF16)],
    )
    bgu = b_gate_up.reshape(DEPTH, N_EXPERTS, 1, 2 * MOE_D_FF)
    return pl.pallas_call(
        _moe_up_kernel,
        grid_spec=grid_spec,
        out_shape=jax.ShapeDtypeStruct((MOE_CAP, MOE_D_FF), BF16),
        compiler_params=_params(("arbitrary", "arbitrary"), 52),
    )(sbe, sbi, nsub, x_sorted, w_gate_up, w_gate_up, bgu, bgu)


def _moe_down_kernel(sbe_ref, sbi_ref, nsub_ref, h_ref, w_ref, b_ref, gate_ref, o_ref, wb):
    s = pl.program_id(0)
    nsub = nsub_ref[s]

    @pl.when(nsub > 0)
    def _():
        wb[...] = w_ref[...].astype(BF16)

        def body(r, carry):
            rows = pl.ds(pl.multiple_of(r * MOE_SUB, MOE_SUB), MOE_SUB)
            y = jnp.dot(h_ref[rows, :], wb[...], preferred_element_type=F32) + b_ref[...]
            o_ref[rows, :] = y * gate_ref[rows, :]
            return carry

        lax.fori_loop(0, nsub, body, 0)


def _moe_down(h_sorted, gate_sorted, sbe, sbi, nsub, w_down, b_down, layer):
    n_c = D_MODEL // MOE_NC
    last_c = n_c - 1

    def c_eff(s, c, nsub_ref):
        return jnp.where(nsub_ref[s] > 0, c, last_c)

    grid_spec = pltpu.PrefetchScalarGridSpec(
        num_scalar_prefetch=3,
        grid=(MOE_NSB, n_c),
        in_specs=[
            pl.BlockSpec((MOE_R, MOE_D_FF), lambda s, c, e, i, n: (i[s], 0)),
            pl.BlockSpec((None, None, MOE_D_FF, MOE_NC), lambda s, c, e, i, n: (layer, e[s], 0, c_eff(s, c, n))),
            pl.BlockSpec((None, None, 1, MOE_NC), lambda s, c, e, i, n: (layer, e[s], 0, c_eff(s, c, n))),
            pl.BlockSpec((MOE_R, 1), lambda s, c, e, i, n: (i[s], 0)),
        ],
        out_specs=pl.BlockSpec((MOE_R, MOE_NC), lambda s, c, e, i, n: (i[s], c_eff(s, c, n))),
        scratch_shapes=[pltpu.VMEM((MOE_D_FF, MOE_NC), BF16)],
    )
    return pl.pallas_call(
        _moe_down_kernel,
        grid_spec=grid_spec,
        out_shape=jax.ShapeDtypeStruct((MOE_CAP, D_MODEL), F32),
        compiler_params=_params(("arbitrary", "arbitrary"), 48),
    )(sbe, sbi, nsub, h_sorted, w_down, b_down.reshape(DEPTH, N_EXPERTS, 1, D_MODEL), gate_sorted)


def _route(top_idx):
    flat_e = top_idx.reshape(-1)
    onehot = (flat_e[:, None] == jnp.arange(N_EXPERTS, dtype=I32)[None, :]).astype(I32)
    csum = jnp.cumsum(onehot, axis=0)
    rank = jnp.take_along_axis(csum, flat_e[:, None], axis=1)[:, 0] - 1
    counts = csum[-1]
    nsb = (counts + MOE_R - 1) // MOE_R
    sb_end = jnp.cumsum(nsb)
    sb_off = sb_end - nsb
    dest = sb_off[flat_e] * MOE_R + rank
    n_real = sb_end[-1]
    s = jnp.arange(MOE_NSB, dtype=I32)
    s_eff = jnp.minimum(s, n_real - 1)
    sbe = jnp.minimum(jnp.searchsorted(sb_end, s_eff, side='right'), N_EXPERTS - 1).astype(I32)
    valid = jnp.clip(counts[sbe] - (s_eff - sb_off[sbe]) * MOE_R, 0, MOE_R)
    valid = jnp.where(s < n_real, valid, 0)
    nsub = ((valid + MOE_SUB - 1) // MOE_SUB).astype(I32)
    return dest.astype(I32), sbe, s_eff.astype(I32), nsub


def _moe(h_bf16, top_idx, top_gate, w_gate_up, b_gate_up, w_down, b_down, layer):
    dest, sbe, sbi, nsub = _route(top_idx)
    flat_tok = jnp.arange(N_ASSIGN, dtype=I32) // TOP_K
    buf_tok = jnp.full((MOE_CAP,), N_TOK, I32).at[dest].set(flat_tok)
    buf_gate = jnp.zeros((MOE_CAP,), F32).at[dest].set(top_gate.reshape(-1))
    x_sorted = jnp.take(h_bf16, buf_tok, axis=0, mode='fill', fill_value=0)
    h_sorted = _moe_up(x_sorted, sbe, sbi, nsub, w_gate_up, b_gate_up, layer)
    out_sorted = _moe_down(h_sorted, buf_gate.reshape(MOE_CAP, 1), sbe, sbi, nsub, w_down, b_down, layer)
    return jnp.take(out_sorted, dest.reshape(N_TOK, TOP_K), axis=0).sum(axis=1)


def _pb_rows(mod_piece):
    idx = jnp.array([0] * N_PROMPT_PB + list(range(1, 1 + DEC_BATCH)), I32)
    return mod_piece[idx][:, None, :]


def kernel(x_prompt, x_sample, cache_a_k, cache_a_v, cache_b_k, cache_b_v, cache_c_k, cache_c_v, c, c_ctx, w_ada, b_ada, g_pre_mix, g_post_mix, g_pre_ffn, g_post_ffn, w_in_even, w_out_even, rpb_a, q_norm_b, k_norm_b, w_in_odd, w_out_odd, sink_c, w_router, b_router, w_gate_up, b_gate_up, w_down, b_down):
    x = jnp.concatenate([x_prompt.reshape(N_PROMPT, D_MODEL), x_sample.reshape(N_SAMPLE, D_MODEL)], axis=0)
    cond8 = jnp.concatenate([c_ctx[None, :], c, jnp.zeros((8 - 1 - DEC_BATCH, D_MODEL), F32)], axis=0)
    mods = _ada_mod(cond8, w_ada, b_ada)
    cos_tab, sin_tab = _rope_tables()
    no_sink = jnp.full((max(A_HEADS, B_HEADS),), -jnp.inf, F32)
    states = {}

    for l in range(DEPTH):
        j = l // 2
        m = [_pb_rows(mods[l, :, k * D_MODEL:(k + 1) * D_MODEL]) for k in range(6)]
        if l % 2 == 0:
            w_in = w_in_even[j].astype(BF16)
            w_out = w_out_even[j].astype(BF16)
            nb = EVEN_IN // IN_BN
            qb0 = 3 * A_HEADS * HEAD_DIM // IN_BN
            kb0 = qb0 + B_HEADS * HEAD_DIM // IN_BN
            vb0 = kb0 + B_KV_HEADS * HEAD_DIM // IN_BN
            modes = jnp.array([MODE_PLAIN] * qb0 + [MODE_NORM_Q] * (kb0 - qb0) + [MODE_NORM_K] * (vb0 - kb0)
                              + [MODE_PLAIN] * (nb - vb0), I32)
            gains = jnp.concatenate([q_norm_b[j][None], k_norm_b[j][None], jnp.ones((6, HEAD_DIM), F32)], axis=0)
        else:
            w_in = w_in_odd[j].astype(BF16)
            w_out = w_out_odd[j].astype(BF16)
            nb = ODD_IN // IN_BN
            vc0 = (C_HEADS + C_KV_HEADS) * HEAD_DIM // IN_BN
            modes = jnp.array([MODE_ROPE] * vc0 + [MODE_PLAIN] * (nb - vc0), I32)
            gains = jnp.ones((8, HEAD_DIM), F32)

        proj = _in_proj(x, g_pre_mix[l], m[0], m[1], w_in, modes, gains, cos_tab, sin_tab)

        if l % 2 == 0:
            ah = A_HEADS * HEAD_DIM
            ctx_a = _ctx_attention(proj, no_sink, 0, ah, 2 * ah, A_HEADS, A_HEADS, 8)
            ctx_b = _ctx_attention(proj, no_sink, 3 * ah, 3 * ah + B_HEADS * HEAD_DIM,
                                   3 * ah + (B_HEADS + B_KV_HEADS) * HEAD_DIM, B_HEADS, B_KV_HEADS, B_KV_HEADS)
            lat_a = _na_attention(proj, cache_a_k, cache_a_v, rpb_a[j], j)
            lat_b = _dense_attention(proj, cache_b_k, cache_b_v, j)
            attn = jnp.concatenate([jnp.concatenate([ctx_a, ctx_b], axis=1),
                                    jnp.concatenate([lat_a, lat_b], axis=1)], axis=0)
            pp = proj[:N_PROMPT]
            kb_c = 3 * ah + B_HEADS * HEAD_DIM
            vb_c = kb_c + B_KV_HEADS * HEAD_DIM
            states.setdefault('ak', []).append(pp[:, ah:2 * ah].reshape(BATCH, SEQ, A_HEADS, HEAD_DIM))
            states.setdefault('av', []).append(pp[:, 2 * ah:3 * ah].reshape(BATCH, SEQ, A_HEADS, HEAD_DIM))
            states.setdefault('bk', []).append(pp[:, kb_c:vb_c].reshape(BATCH, SEQ, B_KV_HEADS, HEAD_DIM))
            states.setdefault('bv', []).append(pp[:, vb_c:].reshape(BATCH, SEQ, B_KV_HEADS, HEAD_DIM))
        else:
            qc = C_HEADS * HEAD_DIM
            kvw = C_KV_HEADS * HEAD_DIM
            ctx_c = _ctx_attention(proj, sink_c[j], 0, qc, qc + kvw, C_HEADS, C_KV_HEADS, C_KV_HEADS)
            lat_c = _window_attention(proj, cache_c_k, cache_c_v, sink_c[j], j)
            attn = jnp.concatenate([ctx_c, lat_c], axis=0)
            pp = proj[:N_PROMPT]
            states.setdefault('ck', []).append(pp[:, qc:qc + kvw].reshape(BATCH, SEQ, C_KV_HEADS, HEAD_DIM))
            states.setdefault('cv', []).append(pp[:, qc + kvw:].reshape(BATCH, SEQ, C_KV_HEADS, HEAD_DIM))

        y = _matmul(attn, w_out)
        wr = jnp.pad(w_router[l], ((0, 0), (0, LANES - N_EXPERTS)))
        wr_hi = wr.astype(BF16)
        wr_lo = (wr - wr_hi.astype(F32)).astype(BF16)
        br = jnp.concatenate([b_router[l], jnp.full((LANES - N_EXPERTS,), NEG_BIG, F32)])[None, :]
        x, h_ffn, top_i, top_g = _post_mix(y, x, m[2], g_post_mix[l], g_pre_ffn[l], m[3], m[4], wr_hi, wr_lo, br)
        f = _moe(h_ffn, top_i[:, :TOP_K], top_g[:, :TOP_K], w_gate_up, b_gate_up, w_down, b_down, l)
        x = _post_ffn(f, x, m[5], g_post_ffn[l])

    y_prompt = x[:N_PROMPT].reshape(BATCH, SEQ, D_MODEL)
    y_sample = x[N_PROMPT:].reshape(DEC_BATCH, DEC_SEQ, D_MODEL)
    st = {k: jnp.stack(v, axis=1) for k, v in states.items()}
    return (y_prompt, y_sample, st['ak'], st['av'], st['bk'], st['bv'], st['ck'], st['cv'])
```

```python
import functools

import jax
import jax.numpy as jnp
import numpy as np
from jax import lax
from jax.experimental import pallas as pl
from jax.experimental.pallas import tpu as pltpu

F32 = jnp.float32
BF16 = jnp.bfloat16
I32 = jnp.int32
U32 = jnp.uint32

D_MODEL = 4096
BATCH = 16
SEQ = 256
DEPTH = 2
DEC_BATCH = 4
DEC_SEQ = 2048
PAST_LEN = 512
GRID_W = 64
GRID_ROWS = DEC_SEQ // GRID_W
HEAD_DIM = 128
A_HEADS = 16
B_HEADS = 16
B_KV_HEADS = 4
C_HEADS = 32
C_KV_HEADS = 4
WIN_H = 8
WIN_W = 16
C_WINDOW = 128
N_EXPERTS = 32
TOP_K = 4
MOE_D_FF = 2048
SWIGLU_LIMIT = 7.0
SWIGLU_ALPHA = 1.702
ROPE_THETA = 10000.0
RMS_EPS = 1e-6
ATTN_SCALE = HEAD_DIM ** -0.5
EVEN_IN = (3 * A_HEADS + B_HEADS + 2 * B_KV_HEADS) * HEAD_DIM
ODD_IN = (C_HEADS + 2 * C_KV_HEADS) * HEAD_DIM

N_PROMPT = BATCH * SEQ
N_SAMPLE = DEC_BATCH * DEC_SEQ
N_TOK = N_PROMPT + N_SAMPLE
PB_TOK = DEC_SEQ
N_PB = N_TOK // PB_TOK
N_PROMPT_PB = N_PROMPT // PB_TOK

LANES = 128
MIB = 1024 * 1024

MOE_R = 1024
MOE_SUB = 256
N_ASSIGN = N_TOK * TOP_K
MOE_NSB = N_ASSIGN // MOE_R + N_EXPERTS
MOE_CAP = MOE_NSB * MOE_R
MOE_FC = 256
MOE_NC = 512

NEG_BIG = -1e30
HALF_D = D_MODEL // 2
HI_MASK = np.uint32(0xFFFF0000)
SHIFT16 = np.uint32(16)


def _params(sem, vmem_mib):
    return pltpu.CompilerParams(dimension_semantics=sem, vmem_limit_bytes=vmem_mib * MIB)


def _ada_kernel(c_ref, w_ref, b_ref, o_ref):
    c = c_ref[...]
    s = (c / (1.0 + jnp.exp(-c))).astype(BF16)
    o_ref[...] = jnp.dot(s, w_ref[...].astype(BF16), preferred_element_type=F32) + b_ref[...]


def _ada_mod(cond8, w_ada, b_ada):
    bn = 512
    n_out = 6 * D_MODEL
    return pl.pallas_call(
        _ada_kernel,
        name="ada_mod",
        grid=(DEPTH, n_out // bn),
        in_specs=[
            pl.BlockSpec((8, D_MODEL), lambda l, j: (0, 0)),
            pl.BlockSpec((None, D_MODEL, bn), lambda l, j: (l, 0, j)),
            pl.BlockSpec((None, 1, bn), lambda l, j: (l, 0, j)),
        ],
        out_specs=pl.BlockSpec((None, 8, bn), lambda l, j: (l, 0, j)),
        out_shape=jax.ShapeDtypeStruct((DEPTH, 8, n_out), F32),
        compiler_params=_params(("arbitrary", "arbitrary"), 40),
    )(cond8, w_ada, b_ada.reshape(DEPTH, 1, n_out))


IN_BM = 512
IN_BN = 512
MODE_PLAIN, MODE_NORM_Q, MODE_NORM_K, MODE_ROPE = 0, 1, 2, 3


def _inproj_kernel(mode_ref, x_ref, g_ref, sh_ref, sc_ref, w_ref, gain_ref, cos_ref, sin_ref,
                   o_ref, h_ref):
    j = pl.program_id(1)

    @pl.when(j == 0)
    def _():
        x = x_ref[...]
        r = lax.rsqrt(jnp.mean(x * x, axis=-1, keepdims=True) + RMS_EPS)
        h = (x * r * g_ref[...]) * (1.0 + sc_ref[...]) + sh_ref[...]
        h_ref[...] = h.astype(BF16)

    acc = jnp.dot(h_ref[...], w_ref[...], preferred_element_type=F32)
    mode = mode_ref[j]

    @pl.when(mode == MODE_PLAIN)
    def _():
        o_ref[...] = acc

    @pl.when(mode != MODE_PLAIN)
    def _():
        gain = jnp.where(mode == MODE_NORM_Q, gain_ref[0:1, :], gain_ref[1:2, :])
        cos = cos_ref[...]
        sin = sin_ref[...]
        lane = lax.broadcasted_iota(I32, (IN_BM, LANES), 1)
        first = (lane % 64) < 32
        for hh in range(IN_BN // HEAD_DIM):
            xh = acc[:, hh * HEAD_DIM:(hh + 1) * HEAD_DIM]
            rs = lax.rsqrt(jnp.mean(xh * xh, axis=-1, keepdims=True) + RMS_EPS)
            y = jnp.where(mode == MODE_ROPE, xh, xh * rs * gain)
            sw = jnp.where(first, pltpu.roll(y, 96, 1), pltpu.roll(y, 32, 1))
            o_ref[:, hh * HEAD_DIM:(hh + 1) * HEAD_DIM] = y * cos + sw * sin


def _in_proj(x, g_pre, shift, scale, w_bf16, modes, gains, cos_tab, sin_tab):
    n = w_bf16.shape[1]
    n_i = N_TOK // IN_BM
    n_prompt_blocks = N_PROMPT // IN_BM
    blocks_per_seq = DEC_SEQ // IN_BM

    def tab_idx(i, j, m):
        return (jnp.where(i < n_prompt_blocks, 0, 1 + (i - n_prompt_blocks) % blocks_per_seq), 0)

    grid_spec = pltpu.PrefetchScalarGridSpec(
        num_scalar_prefetch=1,
        grid=(n_i, n // IN_BN),
        in_specs=[
            pl.BlockSpec((IN_BM, D_MODEL), lambda i, j, m: (i, 0)),
            pl.BlockSpec((1, D_MODEL), lambda i, j, m: (0, 0)),
            pl.BlockSpec((None, 1, D_MODEL), lambda i, j, m: (i * IN_BM // PB_TOK, 0, 0)),
            pl.BlockSpec((None, 1, D_MODEL), lambda i, j, m: (i * IN_BM // PB_TOK, 0, 0)),
            pl.BlockSpec((D_MODEL, IN_BN), lambda i, j, m: (0, j)),
            pl.BlockSpec((8, LANES), lambda i, j, m: (0, 0)),
            pl.BlockSpec((IN_BM, LANES), tab_idx),
            pl.BlockSpec((IN_BM, LANES), tab_idx),
        ],
        out_specs=pl.BlockSpec((IN_BM, IN_BN), lambda i, j, m: (i, j)),
        scratch_shapes=[pltpu.VMEM((IN_BM, D_MODEL), BF16)],
    )
    return pl.pallas_call(
        _inproj_kernel,
        name="in_proj",
        grid_spec=grid_spec,
        out_shape=jax.ShapeDtypeStruct((N_TOK, n), F32),
        compiler_params=_params(("arbitrary", "arbitrary"), 48),
    )(modes, x, g_pre.reshape(1, D_MODEL), shift, scale, w_bf16, gains, cos_tab, sin_tab)


def _rope_tables():
    t = jnp.arange(DEC_SEQ)
    row = (t // GRID_W).astype(F32)
    col = (t % GRID_W).astype(F32)
    n_freq = HEAD_DIM // 4
    inv = ROPE_THETA ** (-jnp.arange(n_freq, dtype=F32) / n_freq)
    ar = row[:, None] * inv
    ac = col[:, None] * inv
    cos = jnp.concatenate([jnp.cos(ar), jnp.cos(ar), jnp.cos(ac), jnp.cos(ac)], axis=-1)
    sin = jnp.concatenate([-jnp.sin(ar), jnp.sin(ar), -jnp.sin(ac), jnp.sin(ac)], axis=-1)
    cos = jnp.concatenate([jnp.ones((IN_BM, HEAD_DIM), F32), cos], axis=0)
    sin = jnp.concatenate([jnp.zeros((IN_BM, HEAD_DIM), F32), sin], axis=0)
    return cos, sin


MM_BM = 1024
MM_BN = 512


def _mm_kernel(a_ref, w_ref, o_ref):
    o_ref[...] = jnp.dot(a_ref[...], w_ref[...], preferred_element_type=F32)


def _matmul(a_bf16, w_bf16):
    m, k = a_bf16.shape
    n = w_bf16.shape[1]
    return pl.pallas_call(
        _mm_kernel,
        name="out_proj",
        grid=(m // MM_BM, n // MM_BN),
        in_specs=[pl.BlockSpec((MM_BM, k), lambda i, j: (i, 0)),
                  pl.BlockSpec((k, MM_BN), lambda i, j: (0, j))],
        out_specs=pl.BlockSpec((MM_BM, MM_BN), lambda i, j: (i, j)),
        out_shape=jax.ShapeDtypeStruct((m, n), F32),
        compiler_params=_params(("arbitrary", "arbitrary"), 48),
    )(a_bf16, w_bf16)


def _softmax_pv(s_list, v_list, sink=None):
    m = jnp.max(s_list[0], axis=-1, keepdims=True)
    for s in s_list[1:]:
        m = jnp.maximum(m, jnp.max(s, axis=-1, keepdims=True))
    if sink is not None:
        m = jnp.maximum(m, sink)
    den = None
    out = None
    for s, v in zip(s_list, v_list):
        p = jnp.exp(s - m)
        d = jnp.sum(p, axis=-1, keepdims=True)
        o = jnp.dot(p.astype(BF16), v, preferred_element_type=F32)
        den = d if den is None else den + d
        out = o if out is None else out + o
    if sink is not None:
        den = den + jnp.exp(sink - m)
    return out / den


def _qk(q, k):
    return lax.dot_general(q, k, (((1,), (1,)), ((), ())), preferred_element_type=F32) * ATTN_SCALE


def _ctx_attn_kernel(sink_ref, q_ref, k_ref, v_ref, o_ref, *, nkv, g):
    kb = pl.program_id(1)
    for kv in range(nkv):
        k = k_ref[:, kv * HEAD_DIM:(kv + 1) * HEAD_DIM].astype(BF16)
        v = v_ref[:, kv * HEAD_DIM:(kv + 1) * HEAD_DIM].astype(BF16)
        for gi in range(g):
            c0 = (kv * g + gi) * HEAD_DIM
            q = q_ref[:, c0:c0 + HEAD_DIM].astype(BF16)
            sink = sink_ref[(kb * nkv + kv) * g + gi]
            o = _softmax_pv([_qk(q, k)], [v], sink)
            o_ref[:, c0:c0 + HEAD_DIM] = o.astype(o_ref.dtype)


def _ctx_attention(proj, sink, q_col, k_col, v_col, hq, hkv, nkv):
    g = hq // hkv
    qw = nkv * g * HEAD_DIM
    kw = nkv * HEAD_DIM
    return pl.pallas_call(
        functools.partial(_ctx_attn_kernel, nkv=nkv, g=g),
        name="ctx_attn",
        grid=(BATCH, hkv // nkv),
        in_specs=[
            pl.BlockSpec(memory_space=pltpu.SMEM),
            pl.BlockSpec((SEQ, qw), lambda b, h: (b, q_col // qw + h)),
            pl.BlockSpec((SEQ, kw), lambda b, h: (b, k_col // kw + h)),
            pl.BlockSpec((SEQ, kw), lambda b, h: (b, v_col // kw + h)),
        ],
        out_specs=pl.BlockSpec((SEQ, qw), lambda b, h: (b, h)),
        out_shape=jax.ShapeDtypeStruct((N_PROMPT, hq * HEAD_DIM), BF16),
        compiler_params=_params(("arbitrary", "arbitrary"), 40),
    )(sink, proj, proj, proj)


DENSE_BQ = 256


def _dense_attn_kernel(q_ref, k_ref, v_ref, kc_ref, vc_ref, o_ref, *, g):
    k = k_ref[...].astype(BF16)
    v = v_ref[...].astype(BF16)
    kc = kc_ref[...].astype(BF16)
    vc = vc_ref[...].astype(BF16)
    for gi in range(g):
        c0 = gi * HEAD_DIM
        q = q_ref[:, c0:c0 + HEAD_DIM].astype(BF16)
        o = _softmax_pv([_qk(q, k), _qk(q, kc)], [v, vc])
        o_ref[:, c0:c0 + HEAD_DIM] = o.astype(o_ref.dtype)


def _dense_attention(proj, cache_k, cache_v, layer_j):
    g = B_HEADS // B_KV_HEADS
    qw = g * HEAD_DIM
    nq = DEC_SEQ // DENSE_BQ
    q_col = 3 * A_HEADS * HEAD_DIM
    k_col = q_col + B_HEADS * HEAD_DIM
    v_col = k_col + B_KV_HEADS * HEAD_DIM
    row0 = N_PROMPT // DENSE_BQ
    ck = cache_k.reshape(DEC_BATCH, -1, PAST_LEN, B_KV_HEADS * HEAD_DIM)
    cv = cache_v.reshape(DEC_BATCH, -1, PAST_LEN, B_KV_HEADS * HEAD_DIM)
    return pl.pallas_call(
        functools.partial(_dense_attn_kernel, g=g),
        name="dense_attn",
        grid=(DEC_BATCH, B_KV_HEADS, nq),
        in_specs=[
            pl.BlockSpec((DENSE_BQ, qw), lambda b, h, i: (row0 + b * nq + i, q_col // qw + h)),
            pl.BlockSpec((DEC_SEQ, HEAD_DIM), lambda b, h, i: (N_PROMPT_PB + b, k_col // HEAD_DIM + h)),
            pl.BlockSpec((DEC_SEQ, HEAD_DIM), lambda b, h, i: (N_PROMPT_PB + b, v_col // HEAD_DIM + h)),
            pl.BlockSpec((None, None, PAST_LEN, HEAD_DIM), lambda b, h, i: (b, layer_j, 0, h)),
            pl.BlockSpec((None, None, PAST_LEN, HEAD_DIM), lambda b, h, i: (b, layer_j, 0, h)),
        ],
        out_specs=pl.BlockSpec((DENSE_BQ, qw), lambda b, h, i: (b * nq + i, h)),
        out_shape=jax.ShapeDtypeStruct((N_SAMPLE, B_HEADS * HEAD_DIM), BF16),
        compiler_params=_params(("arbitrary", "arbitrary", "arbitrary"), 48),
    )(proj, proj, proj, ck, cv)


WIN_BQ = 256
WIN_KEYS = WIN_BQ + 2 * C_WINDOW


def _window_attn_kernel(sink_ref, q_ref, k_ref, v_ref, kc_ref, vc_ref, o_ref, *, g):
    h = pl.program_id(1)
    i = pl.program_id(2)
    q0 = i * WIN_BQ
    k0 = pl.multiple_of(jnp.clip(q0 - C_WINDOW, 0, DEC_SEQ - WIN_KEYS), C_WINDOW)
    k = k_ref[pl.ds(k0, WIN_KEYS), :].astype(BF16)
    v = v_ref[pl.ds(k0, WIN_KEYS), :].astype(BF16)
    kc = kc_ref[...].astype(BF16)
    vc = vc_ref[...].astype(BF16)
    q_pos = q0 + lax.broadcasted_iota(I32, (WIN_BQ, WIN_KEYS), 0)
    k_pos = k0 + lax.broadcasted_iota(I32, (WIN_BQ, WIN_KEYS), 1)
    ok = jnp.abs(q_pos - k_pos) <= C_WINDOW
    for gi in range(g):
        c0 = gi * HEAD_DIM
        q = q_ref[:, c0:c0 + HEAD_DIM].astype(BF16)
        s_lat = jnp.where(ok, _qk(q, k), -jnp.inf)
        o = _softmax_pv([s_lat, _qk(q, kc)], [v, vc], sink_ref[h * g + gi])
        o_ref[:, c0:c0 + HEAD_DIM] = o.astype(o_ref.dtype)


def _window_attention(proj, cache_k, cache_v, sink, layer_j):
    g = C_HEADS // C_KV_HEADS
    qw = g * HEAD_DIM
    nq = DEC_SEQ // WIN_BQ
    k_col = C_HEADS * HEAD_DIM
    v_col = k_col + C_KV_HEADS * HEAD_DIM
    row0 = N_PROMPT // WIN_BQ
    ck = cache_k.reshape(DEC_BATCH, -1, PAST_LEN, C_KV_HEADS * HEAD_DIM)
    cv = cache_v.reshape(DEC_BATCH, -1, PAST_LEN, C_KV_HEADS * HEAD_DIM)
    return pl.pallas_call(
        functools.partial(_window_attn_kernel, g=g),
        name="window_attn",
        grid=(DEC_BATCH, C_KV_HEADS, nq),
        in_specs=[
            pl.BlockSpec(memory_space=pltpu.SMEM),
            pl.BlockSpec((WIN_BQ, qw), lambda b, h, i: (row0 + b * nq + i, h)),
            pl.BlockSpec((DEC_SEQ, HEAD_DIM), lambda b, h, i: (N_PROMPT_PB + b, k_col // HEAD_DIM + h)),
            pl.BlockSpec((DEC_SEQ, HEAD_DIM), lambda b, h, i: (N_PROMPT_PB + b, v_col // HEAD_DIM + h)),
            pl.BlockSpec((None, None, PAST_LEN, HEAD_DIM), lambda b, h, i: (b, layer_j, 0, h)),
            pl.BlockSpec((None, None, PAST_LEN, HEAD_DIM), lambda b, h, i: (b, layer_j, 0, h)),
        ],
        out_specs=pl.BlockSpec((WIN_BQ, qw), lambda b, h, i: (b * nq + i, h)),
        out_shape=jax.ShapeDtypeStruct((N_SAMPLE, C_HEADS * HEAD_DIM), BF16),
        compiler_params=_params(("arbitrary", "arbitrary", "arbitrary"), 48),
    )(sink, proj, proj, proj, ck, cv)


NA_HC = 4
NA_KEYS = WIN_H * GRID_W


def _na_row_start(i):
    return jnp.clip(i - WIN_H // 2, 0, GRID_ROWS - WIN_H)


def _na_attn_kernel(q_ref, k_ref, v_ref, kc_ref, vc_ref, bias_ref, o_ref):
    i = pl.program_id(2)
    r0 = pl.multiple_of(_na_row_start(i) * GRID_W, GRID_W)
    for hh in range(NA_HC):
        cs = slice(hh * HEAD_DIM, (hh + 1) * HEAD_DIM)
        q = q_ref[:, cs].astype(BF16)
        k = k_ref[pl.ds(r0, NA_KEYS), cs].astype(BF16)
        v = v_ref[pl.ds(r0, NA_KEYS), cs].astype(BF16)
        kc = kc_ref[:, cs].astype(BF16)
        vc = vc_ref[:, cs].astype(BF16)
        s_lat = _qk(q, k) + bias_ref[hh]
        o = _softmax_pv([s_lat, _qk(q, kc)], [v, vc])
        o_ref[:, cs] = o.astype(o_ref.dtype)


def _na_bias_table(rpb):
    qc = jnp.arange(GRID_W)[:, None]
    kc = jnp.arange(GRID_W)[None, :]
    ws = jnp.clip(qc - WIN_W // 2, 0, GRID_W - WIN_W)
    ok = (kc >= ws) & (kc < ws + WIN_W)
    dc = jnp.clip(kc - qc + WIN_W - 1, 0, 2 * WIN_W - 2)
    t = rpb[:, :, dc]
    t = jnp.where(ok[None, None], t, -jnp.inf)
    dr = jnp.arange(WIN_H)[:, None] + jnp.arange(WIN_H)[None, :]
    b = t[:, dr]
    b = b.transpose(0, 1, 3, 2, 4).reshape(A_HEADS, WIN_H, GRID_W, NA_KEYS)
    return b.astype(F32)


def _na_attention(proj, cache_k, cache_v, rpb, layer_j):
    cw = NA_HC * HEAD_DIM
    k_col = A_HEADS * HEAD_DIM
    v_col = 2 * A_HEADS * HEAD_DIM
    row0 = N_PROMPT // GRID_W
    ck = cache_k.reshape(DEC_BATCH, -1, PAST_LEN, A_HEADS * HEAD_DIM)
    cv = cache_v.reshape(DEC_BATCH, -1, PAST_LEN, A_HEADS * HEAD_DIM)
    bias = _na_bias_table(rpb)

    def bias_idx(b, h, i):
        return (h, _na_row_start(i) - i + WIN_H - 1, 0, 0)

    return pl.pallas_call(
        _na_attn_kernel,
        name="na_attn",
        grid=(DEC_BATCH, A_HEADS // NA_HC, GRID_ROWS),
        in_specs=[
            pl.BlockSpec((GRID_W, cw), lambda b, h, i: (row0 + b * GRID_ROWS + i, h)),
            pl.BlockSpec((DEC_SEQ, cw), lambda b, h, i: (N_PROMPT_PB + b, k_col // cw + h)),
            pl.BlockSpec((DEC_SEQ, cw), lambda b, h, i: (N_PROMPT_PB + b, v_col // cw + h)),
            pl.BlockSpec((None, None, PAST_LEN, cw), lambda b, h, i: (b, layer_j, 0, h)),
            pl.BlockSpec((None, None, PAST_LEN, cw), lambda b, h, i: (b, layer_j, 0, h)),
            pl.BlockSpec((NA_HC, None, GRID_W, NA_KEYS), bias_idx),
        ],
        out_specs=pl.BlockSpec((GRID_W, cw), lambda b, h, i: (b * GRID_ROWS + i, h)),
        out_shape=jax.ShapeDtypeStruct((N_SAMPLE, A_HEADS * HEAD_DIM), BF16),
        compiler_params=_params(("arbitrary", "arbitrary", "arbitrary"), 48),
    )(proj, proj, proj, ck, cv, bias)


EP_BM = 256


def _rms(x):
    return x * lax.rsqrt(jnp.mean(x * x, axis=-1, keepdims=True) + RMS_EPS)


def _post_mix_kernel(y_ref, x_ref, gate_ref, gpost_ref, gpre_ref, sh_ref, sc_ref,
                     wr_hi_ref, wr_lo_ref, br_ref, xo_ref, h_ref, ti_ref, tg_ref):
    x_new = x_ref[...] + gate_ref[...] * (_rms(y_ref[...]) * gpost_ref[...])
    xo_ref[...] = x_new
    h = (_rms(x_new) * gpre_ref[...]) * (1.0 + sc_ref[...]) + sh_ref[...]
    h_hi = h.astype(BF16)
    h_hi32 = h_hi.astype(F32)
    bits = pltpu.bitcast(h_hi32, U32)
    h_ref[...] = (bits[:, :HALF_D] >> SHIFT16) | (bits[:, HALF_D:] & HI_MASK)
    h_lo = (h - h_hi32).astype(BF16)
    w_hi = wr_hi_ref[...]
    logits = (jnp.dot(h_hi, w_hi, preferred_element_type=F32)
              + jnp.dot(h_lo, w_hi, preferred_element_type=F32)
              + jnp.dot(h_hi, wr_lo_ref[...], preferred_element_type=F32)) + br_ref[...]
    lane = lax.broadcasted_iota(I32, (EP_BM, LANES), 1).astype(F32)
    vals, idxs = [], []
    l = logits
    for _ in range(TOP_K):
        m = jnp.max(l, axis=-1, keepdims=True)
        idx = jnp.min(jnp.where(l == m, lane, float(LANES)), axis=-1, keepdims=True)
        vals.append(m)
        idxs.append(idx)
        l = jnp.where(lane == idx, -jnp.inf, l)
    es = [jnp.exp(v - vals[0]) for v in vals]
    den = es[0] + es[1] + es[2] + es[3]
    ti = jnp.zeros((EP_BM, LANES), F32)
    tg = jnp.zeros((EP_BM, LANES), F32)
    for k in range(TOP_K):
        ti = jnp.where(lane == float(k), idxs[k], ti)
        tg = jnp.where(lane == float(k), es[k] / den, tg)
    ti_ref[...] = ti.astype(I32)
    tg_ref[...] = tg


def _post_mix(y, x, gate, g_post, g_pre, shift, scale, wr_hi, wr_lo, br):
    row = pl.BlockSpec((EP_BM, D_MODEL), lambda i: (i, 0))
    vec = pl.BlockSpec((1, D_MODEL), lambda i: (0, 0))
    mod = pl.BlockSpec((None, 1, D_MODEL), lambda i: (i * EP_BM // PB_TOK, 0, 0))
    wr = pl.BlockSpec((D_MODEL, LANES), lambda i: (0, 0))
    small = pl.BlockSpec((EP_BM, LANES), lambda i: (i, 0))
    packed = pl.BlockSpec((EP_BM, HALF_D), lambda i: (i, 0))
    return pl.pallas_call(
        _post_mix_kernel,
        name="post_mix",
        grid=(N_TOK // EP_BM,),
        in_specs=[row, row, mod, vec, vec, mod, mod, wr, wr, pl.BlockSpec((1, LANES), lambda i: (0, 0))],
        out_specs=[row, packed, small, small],
        out_shape=[jax.ShapeDtypeStruct((N_TOK, D_MODEL), F32),
                   jax.ShapeDtypeStruct((N_TOK, HALF_D), U32),
                   jax.ShapeDtypeStruct((N_TOK, LANES), I32),
                   jax.ShapeDtypeStruct((N_TOK, LANES), F32)],
        compiler_params=_params(("arbitrary",), 48),
    )(y, x, gate, g_post.reshape(1, D_MODEL), g_pre.reshape(1, D_MODEL), shift, scale, wr_hi, wr_lo, br)


CB_TOK = 128
CB_ROWS = CB_TOK * TOP_K
CB_STEPS = N_TOK // CB_TOK


def _combine_kernel(dest_ref, dest_next_ref, e_hbm, tg_ref, x_ref, gate_ref, gpost_ref, xo_ref, buf, sem):
    i = pl.program_id(0)
    slot = i % 2

    def gather(idx_ref, dst_slot):
        def body(a, carry):
            row = (a % TOP_K) * CB_TOK + a // TOP_K
            pltpu.make_async_copy(e_hbm.at[pl.ds(idx_ref[0, a], 1)], buf.at[dst_slot, pl.ds(row, 1)],
                                  sem.at[dst_slot]).start()
            return carry
        lax.fori_loop(0, CB_ROWS, body, 0, unroll=8)

    @pl.when(i == 0)
    def _():
        gather(dest_ref, 0)

    @pl.when(i + 1 < CB_STEPS)
    def _():
        gather(dest_next_ref, 1 - slot)

    pltpu.make_async_copy(e_hbm.at[pl.ds(0, CB_ROWS)], buf.at[slot], sem.at[slot]).wait()
    tg = tg_ref[...]
    y = buf[slot, 0:CB_TOK, :] * tg[:, 0:1]
    for k in range(1, TOP_K):
        y = y + buf[slot, k * CB_TOK:(k + 1) * CB_TOK, :] * tg[:, k:k + 1]
    xo_ref[...] = x_ref[...] + gate_ref[...] * (_rms(y) * gpost_ref[...])


def _combine_post_ffn(expert_out, dest, top_gate, x, gate, g_post):
    row = pl.BlockSpec((CB_TOK, D_MODEL), lambda i: (i, 0))
    dest3 = dest.reshape(CB_STEPS, 1, CB_ROWS)
    return pl.pallas_call(
        _combine_kernel,
        name="moe_combine",
        grid=(CB_STEPS,),
        in_specs=[
            pl.BlockSpec((None, 1, CB_ROWS), lambda i: (i, 0, 0), memory_space=pltpu.SMEM),
            pl.BlockSpec((None, 1, CB_ROWS), lambda i: (jnp.minimum(i + 1, CB_STEPS - 1), 0, 0),
                         memory_space=pltpu.SMEM),
            pl.BlockSpec(memory_space=pl.ANY),
            pl.BlockSpec((CB_TOK, LANES), lambda i: (i, 0)),
            row,
            pl.BlockSpec((None, 1, D_MODEL), lambda i: (i * CB_TOK // PB_TOK, 0, 0)),
            pl.BlockSpec((1, D_MODEL), lambda i: (0, 0)),
        ],
        out_specs=row,
        out_shape=jax.ShapeDtypeStruct((N_TOK, D_MODEL), F32),
        scratch_shapes=[pltpu.VMEM((2, CB_ROWS, D_MODEL), F32), pltpu.SemaphoreType.DMA((2,))],
        compiler_params=_params(("arbitrary",), 40),
    )(dest3, dest3, expert_out, top_gate, x, gate, g_post.reshape(1, D_MODEL))


DISP_CH = 2048
SUBLANES = 8
DISP_ZROWS = MOE_SUB + SUBLANES


def _dispatch_kernel(tail_ref, dest_ref, h_hbm, o_hbm, zbuf, sem, zsem):
    g = pl.program_id(0)

    @pl.when(g == 0)
    def _():
        zbuf[...] = jnp.zeros(zbuf.shape, U32)
        for e in range(N_EXPERTS):
            start = pl.multiple_of((tail_ref[e] // SUBLANES) * SUBLANES, SUBLANES)
            pltpu.make_async_copy(zbuf, o_hbm.at[pl.ds(start, DISP_ZROWS)], zsem).start()
        for e in range(N_EXPERTS):
            pltpu.make_async_copy(zbuf, o_hbm.at[pl.ds(0, DISP_ZROWS)], zsem).wait()

    t0 = g * (DISP_CH // TOP_K)

    def body(a, carry):
        pltpu.make_async_copy(h_hbm.at[pl.ds(t0 + a // TOP_K, 1)], o_hbm.at[pl.ds(dest_ref[a], 1)], sem).start()
        return carry

    lax.fori_loop(0, DISP_CH, body, 0, unroll=8)
    pltpu.make_async_copy(h_hbm.at[pl.ds(0, DISP_CH)], o_hbm.at[pl.ds(0, DISP_CH)], sem).wait()


def _dispatch(h_packed, dest, tails):
    return pl.pallas_call(
        _dispatch_kernel,
        name="moe_dispatch",
        grid=(N_ASSIGN // DISP_CH,),
        in_specs=[
            pl.BlockSpec(memory_space=pltpu.SMEM),
            pl.BlockSpec((DISP_CH,), lambda g: (g,), memory_space=pltpu.SMEM),
            pl.BlockSpec(memory_space=pl.ANY),
        ],
        out_specs=pl.BlockSpec(memory_space=pl.ANY),
        out_shape=jax.ShapeDtypeStruct((MOE_CAP, HALF_D), U32),
        scratch_shapes=[pltpu.VMEM((DISP_ZROWS, HALF_D), U32), pltpu.SemaphoreType.DMA, pltpu.SemaphoreType.DMA],
        compiler_params=_params(("arbitrary",), 32),
    )(tails, dest, h_packed)


def _for_valid_rows(nsub, compute):
    pair = 2 * MOE_SUB

    def body(p, carry):
        compute(pl.ds(pl.multiple_of(p * pair, pair), pair))
        return carry

    lax.fori_loop(0, nsub // 2, body, 0)

    @pl.when(nsub % 2 == 1)
    def _():
        compute(pl.ds(pl.multiple_of((nsub - 1) * MOE_SUB, MOE_SUB), MOE_SUB))


def _moe_up_kernel(sbe_ref, sbi_ref, nsub_ref, x_ref, wg_ref, wl_ref, bg_ref, bl_ref, o_ref, xb, wgb, wlb):
    s = pl.program_id(0)
    c = pl.program_id(1)
    nsub = nsub_ref[s]

    @pl.when((nsub > 0) & (c == 0))
    def _():
        def unpack(r, carry):
            rows = pl.ds(pl.multiple_of(r * MOE_SUB, MOE_SUB), MOE_SUB)
            w = x_ref[rows, :]
            xb[rows, 0:HALF_D] = pltpu.bitcast(w << SHIFT16, F32).astype(BF16)
            xb[rows, HALF_D:D_MODEL] = pltpu.bitcast(w & HI_MASK, F32).astype(BF16)
            return carry

        lax.fori_loop(0, nsub, unpack, 0)

    @pl.when(nsub > 0)
    def _():
        wgb[...] = wg_ref[...].astype(BF16)
        wlb[...] = wl_ref[...].astype(BF16)

        def compute(rows):
            xs = xb[rows, :]
            glu = jnp.dot(xs, wgb[...], preferred_element_type=F32) + bg_ref[...]
            lin = jnp.dot(xs, wlb[...], preferred_element_type=F32) + bl_ref[...]
            glu = jnp.minimum(glu, SWIGLU_LIMIT)
            lin = jnp.clip(lin, -SWIGLU_LIMIT, SWIGLU_LIMIT)
            act = glu / (1.0 + jnp.exp(-SWIGLU_ALPHA * glu)) * (lin + 1.0)
            o_ref[rows, :] = act.astype(BF16)

        _for_valid_rows(nsub, compute)


def _moe_up(x_sorted, sbe, sbi, nsub, w_gate_up, b_gate_up, layer):
    n_c = MOE_D_FF // MOE_FC
    last_c = n_c - 1

    def c_eff(s, c, nsub_ref):
        return jnp.where(nsub_ref[s] > 0, c, last_c)

    grid_spec = pltpu.PrefetchScalarGridSpec(
        num_scalar_prefetch=3,
        grid=(MOE_NSB, n_c),
        in_specs=[
            pl.BlockSpec((MOE_R, HALF_D), lambda s, c, e, i, n: (i[s], 0)),
            pl.BlockSpec((None, None, D_MODEL, MOE_FC), lambda s, c, e, i, n: (layer, e[s], 0, c_eff(s, c, n))),
            pl.BlockSpec((None, None, D_MODEL, MOE_FC),
                         lambda s, c, e, i, n: (layer, e[s], 0, n_c + c_eff(s, c, n))),
            pl.BlockSpec((None, None, 1, MOE_FC), lambda s, c, e, i, n: (layer, e[s], 0, c_eff(s, c, n))),
            pl.BlockSpec((None, None, 1, MOE_FC), lambda s, c, e, i, n: (layer, e[s], 0, n_c + c_eff(s, c, n))),
        ],
        out_specs=pl.BlockSpec((MOE_R, MOE_FC), lambda s, c, e, i, n: (i[s], c_eff(s, c, n))),
        scratch_shapes=[pltpu.VMEM((MOE_R, D_MODEL), BF16),
                        pltpu.VMEM((D_MODEL, MOE_FC), BF16), pltpu.VMEM((D_MODEL, MOE_FC), BF16)],
    )
    bgu = b_gate_up.reshape(DEPTH, N_EXPERTS, 1, 2 * MOE_D_FF)
    return pl.pallas_call(
        _moe_up_kernel,
        name="moe_up",
        grid_spec=grid_spec,
        out_shape=jax.ShapeDtypeStruct((MOE_CAP, MOE_D_FF), BF16),
        compiler_params=_params(("arbitrary", "arbitrary"), 56),
    )(sbe, sbi, nsub, x_sorted, w_gate_up, w_gate_up, bgu, bgu)


def _moe_down_kernel(sbe_ref, sbi_ref, nsub_ref, h_ref, w_ref, b_ref, o_ref, wb):
    s = pl.program_id(0)
    nsub = nsub_ref[s]

    @pl.when(nsub > 0)
    def _():
        wb[...] = w_ref[...].astype(BF16)

        def compute(rows):
            o_ref[rows, :] = jnp.dot(h_ref[rows, :], wb[...], preferred_element_type=F32) + b_ref[...]

        _for_valid_rows(nsub, compute)


def _moe_down(h_sorted, sbe, sbi, nsub, w_down, b_down, layer):
    n_c = D_MODEL // MOE_NC
    last_c = n_c - 1

    def c_eff(s, c, nsub_ref):
        return jnp.where(nsub_ref[s] > 0, c, last_c)

    grid_spec = pltpu.PrefetchScalarGridSpec(
        num_scalar_prefetch=3,
        grid=(MOE_NSB, n_c),
        in_specs=[
            pl.BlockSpec((MOE_R, MOE_D_FF), lambda s, c, e, i, n: (i[s], 0)),
            pl.BlockSpec((None, None, MOE_D_FF, MOE_NC), lambda s, c, e, i, n: (layer, e[s], 0, c_eff(s, c, n))),
            pl.BlockSpec((None, None, 1, MOE_NC), lambda s, c, e, i, n: (layer, e[s], 0, c_eff(s, c, n))),
        ],
        out_specs=pl.BlockSpec((MOE_R, MOE_NC), lambda s, c, e, i, n: (i[s], c_eff(s, c, n))),
        scratch_shapes=[pltpu.VMEM((MOE_D_FF, MOE_NC), BF16)],
    )
    return pl.pallas_call(
        _moe_down_kernel,
        name="moe_down",
        grid_spec=grid_spec,
        out_shape=jax.ShapeDtypeStruct((MOE_CAP, D_MODEL), F32),
        compiler_params=_params(("arbitrary", "arbitrary"), 48),
    )(sbe, sbi, nsub, h_sorted, w_down, b_down.reshape(DEPTH, N_EXPERTS, 1, D_MODEL))


def _route(top_idx):
    flat_e = top_idx.reshape(-1)
    onehot = (flat_e[:, None] == jnp.arange(N_EXPERTS, dtype=I32)[None, :]).astype(I32)
    csum = jnp.cumsum(onehot, axis=0)
    rank = jnp.take_along_axis(csum, flat_e[:, None], axis=1)[:, 0] - 1
    counts = csum[-1]
    nsb = (counts + MOE_R - 1) // MOE_R
    sb_end = jnp.cumsum(nsb)
    sb_off = sb_end - nsb
    dest = sb_off[flat_e] * MOE_R + rank
    n_real = sb_end[-1]
    s = jnp.arange(MOE_NSB, dtype=I32)
    s_eff = jnp.minimum(s, n_real - 1)
    sbe = jnp.minimum(jnp.searchsorted(sb_end, s_eff, side='right'), N_EXPERTS - 1).astype(I32)
    valid = jnp.clip(counts[sbe] - (s_eff - sb_off[sbe]) * MOE_R, 0, MOE_R)
    valid = jnp.where(s < n_real, valid, 0)
    nsub = ((valid + MOE_SUB - 1) // MOE_SUB).astype(I32)
    tails = (sb_off * MOE_R + counts).astype(I32)
    return dest.astype(I32), sbe, s_eff.astype(I32), nsub, tails


def _moe_experts(h_packed, top_idx, w_gate_up, b_gate_up, w_down, b_down, layer):
    dest, sbe, sbi, nsub, tails = _route(top_idx)
    x_sorted = _dispatch(h_packed, dest, tails)
    h_sorted = _moe_up(x_sorted, sbe, sbi, nsub, w_gate_up, b_gate_up, layer)
    return _moe_down(h_sorted, sbe, sbi, nsub, w_down, b_down, layer), dest


def _pb_rows(mod_piece):
    idx = jnp.array([0] * N_PROMPT_PB + list(range(1, 1 + DEC_BATCH)), I32)
    return mod_piece[idx][:, None, :]


def kernel(x_prompt, x_sample, cache_a_k, cache_a_v, cache_b_k, cache_b_v, cache_c_k, cache_c_v, c, c_ctx, w_ada, b_ada, g_pre_mix, g_post_mix, g_pre_ffn, g_post_ffn, w_in_even, w_out_even, rpb_a, q_norm_b, k_norm_b, w_in_odd, w_out_odd, sink_c, w_router, b_router, w_gate_up, b_gate_up, w_down, b_down):
    x = jnp.concatenate([x_prompt.reshape(N_PROMPT, D_MODEL), x_sample.reshape(N_SAMPLE, D_MODEL)], axis=0)
    cond8 = jnp.concatenate([c_ctx[None, :], c, jnp.zeros((8 - 1 - DEC_BATCH, D_MODEL), F32)], axis=0)
    mods = _ada_mod(cond8, w_ada, b_ada)
    cos_tab, sin_tab = _rope_tables()
    no_sink = jnp.full((max(A_HEADS, B_HEADS),), -jnp.inf, F32)
    states = {}

    for l in range(DEPTH):
        j = l // 2
        m = [_pb_rows(mods[l, :, k * D_MODEL:(k + 1) * D_MODEL]) for k in range(6)]
        if l % 2 == 0:
            w_in = w_in_even[j].astype(BF16)
            w_out = w_out_even[j].astype(BF16)
            nb = EVEN_IN // IN_BN
            qb0 = 3 * A_HEADS * HEAD_DIM // IN_BN
            kb0 = qb0 + B_HEADS * HEAD_DIM // IN_BN
            vb0 = kb0 + B_KV_HEADS * HEAD_DIM // IN_BN
            modes = jnp.array([MODE_PLAIN] * qb0 + [MODE_NORM_Q] * (kb0 - qb0) + [MODE_NORM_K] * (vb0 - kb0)
                              + [MODE_PLAIN] * (nb - vb0), I32)
            gains = jnp.concatenate([q_norm_b[j][None], k_norm_b[j][None], jnp.ones((6, HEAD_DIM), F32)], axis=0)
        else:
            w_in = w_in_odd[j].astype(BF16)
            w_out = w_out_odd[j].astype(BF16)
            nb = ODD_IN // IN_BN
            vc0 = (C_HEADS + C_KV_HEADS) * HEAD_DIM // IN_BN
            modes = jnp.array([MODE_ROPE] * vc0 + [MODE_PLAIN] * (nb - vc0), I32)
            gains = jnp.ones((8, HEAD_DIM), F32)

        proj = _in_proj(x, g_pre_mix[l], m[0], m[1], w_in, modes, gains, cos_tab, sin_tab)

        if l % 2 == 0:
            ah = A_HEADS * HEAD_DIM
            ctx_a = _ctx_attention(proj, no_sink, 0, ah, 2 * ah, A_HEADS, A_HEADS, 8)
            ctx_b = _ctx_attention(proj, no_sink, 3 * ah, 3 * ah + B_HEADS * HEAD_DIM,
                                   3 * ah + (B_HEADS + B_KV_HEADS) * HEAD_DIM, B_HEADS, B_KV_HEADS, B_KV_HEADS)
            lat_a = _na_attention(proj, cache_a_k, cache_a_v, rpb_a[j], j)
            lat_b = _dense_attention(proj, cache_b_k, cache_b_v, j)
            attn = jnp.concatenate([jnp.concatenate([ctx_a, ctx_b], axis=1),
                                    jnp.concatenate([lat_a, lat_b], axis=1)], axis=0)
            pp = proj[:N_PROMPT]
            kb_c = 3 * ah + B_HEADS * HEAD_DIM
            vb_c = kb_c + B_KV_HEADS * HEAD_DIM
            states.setdefault('ak', []).append(pp[:, ah:2 * ah].reshape(BATCH, SEQ, A_HEADS, HEAD_DIM))
            states.setdefault('av', []).append(pp[:, 2 * ah:3 * ah].reshape(BATCH, SEQ, A_HEADS, HEAD_DIM))
            states.setdefault('bk', []).append(pp[:, kb_c:vb_c].reshape(BATCH, SEQ, B_KV_HEADS, HEAD_DIM))
            states.setdefault('bv', []).append(pp[:, vb_c:].reshape(BATCH, SEQ, B_KV_HEADS, HEAD_DIM))
        else:
            qc = C_HEADS * HEAD_DIM
            kvw = C_KV_HEADS * HEAD_DIM
            ctx_c = _ctx_attention(proj, sink_c[j], 0, qc, qc + kvw, C_HEADS, C_KV_HEADS, C_KV_HEADS)
            lat_c = _window_attention(proj, cache_c_k, cache_c_v, sink_c[j], j)
            attn = jnp.concatenate([ctx_c, lat_c], axis=0)
            pp = proj[:N_PROMPT]
            states.setdefault('ck', []).append(pp[:, qc:qc + kvw].reshape(BATCH, SEQ, C_KV_HEADS, HEAD_DIM))
            states.setdefault('cv', []).append(pp[:, qc + kvw:].reshape(BATCH, SEQ, C_KV_HEADS, HEAD_DIM))

        y = _matmul(attn, w_out)
        wr = jnp.pad(w_router[l], ((0, 0), (0, LANES - N_EXPERTS)))
        wr_hi = wr.astype(BF16)
        wr_lo = (wr - wr_hi.astype(F32)).astype(BF16)
        br = jnp.concatenate([b_router[l], jnp.full((LANES - N_EXPERTS,), NEG_BIG, F32)])[None, :]
        x, h_ffn, top_i, top_g = _post_mix(y, x, m[2], g_post_mix[l], g_pre_ffn[l], m[3], m[4], wr_hi, wr_lo, br)
        expert_out, dest = _moe_experts(h_ffn, top_i[:, :TOP_K], w_gate_up, b_gate_up, w_down, b_down, l)
        x = _combine_post_ffn(expert_out, dest, top_g, x, m[5], g_post_ffn[l])

    y_prompt = x[:N_PROMPT].reshape(BATCH, SEQ, D_MODEL)
    y_sample = x[N_PROMPT:].reshape(DEC_BATCH, DEC_SEQ, D_MODEL)
    st = {k: jnp.stack(v, axis=1) for k, v in states.items()}
    return (y_prompt, y_sample, st['ak'], st['av'], st['bk'], st['bv'], st['ck'], st['cv'])
```

```python
import functools

import jax
import jax.numpy as jnp
import numpy as np
from jax import lax
from jax.experimental import pallas as pl
from jax.experimental.pallas import tpu as pltpu

F32 = jnp.float32
BF16 = jnp.bfloat16
I32 = jnp.int32
U32 = jnp.uint32

D_MODEL = 4096
BATCH = 16
SEQ = 256
DEPTH = 2
DEC_BATCH = 4
DEC_SEQ = 2048
PAST_LEN = 512
GRID_W = 64
GRID_ROWS = DEC_SEQ // GRID_W
HEAD_DIM = 128
A_HEADS = 16
B_HEADS = 16
B_KV_HEADS = 4
C_HEADS = 32
C_KV_HEADS = 4
WIN_H = 8
WIN_W = 16
C_WINDOW = 128
N_EXPERTS = 32
TOP_K = 4
MOE_D_FF = 2048
SWIGLU_LIMIT = 7.0
SWIGLU_ALPHA = 1.702
ROPE_THETA = 10000.0
RMS_EPS = 1e-6
ATTN_SCALE = HEAD_DIM ** -0.5
EVEN_IN = (3 * A_HEADS + B_HEADS + 2 * B_KV_HEADS) * HEAD_DIM
ODD_IN = (C_HEADS + 2 * C_KV_HEADS) * HEAD_DIM

N_PROMPT = BATCH * SEQ
N_SAMPLE = DEC_BATCH * DEC_SEQ
N_TOK = N_PROMPT + N_SAMPLE
PB_TOK = DEC_SEQ
N_PB = N_TOK // PB_TOK
N_PROMPT_PB = N_PROMPT // PB_TOK

LANES = 128
MIB = 1024 * 1024

MOE_R = 1024
MOE_SUB = 256
N_ASSIGN = N_TOK * TOP_K
MOE_NSB = N_ASSIGN // MOE_R + N_EXPERTS
MOE_CAP = MOE_NSB * MOE_R
MOE_FC = 256
MOE_NC = 512

NEG_BIG = -1e30
HALF_D = D_MODEL // 2
HI_MASK = np.uint32(0xFFFF0000)
SHIFT16 = np.uint32(16)


def _params(sem, vmem_mib):
    return pltpu.CompilerParams(dimension_semantics=sem, vmem_limit_bytes=vmem_mib * MIB)


def _ada_kernel(c_ref, w_ref, b_ref, o_ref):
    c = c_ref[...]
    s = (c / (1.0 + jnp.exp(-c))).astype(BF16)
    o_ref[...] = jnp.dot(s, w_ref[...].astype(BF16), preferred_element_type=F32) + b_ref[...]


def _ada_mod(cond8, w_ada, b_ada):
    bn = 512
    n_out = 6 * D_MODEL
    return pl.pallas_call(
        _ada_kernel,
        name="ada_mod",
        grid=(DEPTH, n_out // bn),
        in_specs=[
            pl.BlockSpec((8, D_MODEL), lambda l, j: (0, 0)),
            pl.BlockSpec((None, D_MODEL, bn), lambda l, j: (l, 0, j)),
            pl.BlockSpec((None, 1, bn), lambda l, j: (l, 0, j)),
        ],
        out_specs=pl.BlockSpec((None, 8, bn), lambda l, j: (l, 0, j)),
        out_shape=jax.ShapeDtypeStruct((DEPTH, 8, n_out), F32),
        compiler_params=_params(("arbitrary", "arbitrary"), 40),
    )(cond8, w_ada, b_ada.reshape(DEPTH, 1, n_out))


IN_BM = 512
IN_BN = 512
MODE_PLAIN, MODE_NORM_Q, MODE_NORM_K, MODE_ROPE = 0, 1, 2, 3


def _inproj_kernel(mode_ref, x_ref, g_ref, sh_ref, sc_ref, w_ref, gain_ref, cos_ref, sin_ref,
                   o_ref, h_ref):
    j = pl.program_id(1)

    @pl.when(j == 0)
    def _():
        x = x_ref[...]
        r = lax.rsqrt(jnp.mean(x * x, axis=-1, keepdims=True) + RMS_EPS)
        h = (x * r * g_ref[...]) * (1.0 + sc_ref[...]) + sh_ref[...]
        h_ref[...] = h.astype(BF16)

    acc = jnp.dot(h_ref[...], w_ref[...], preferred_element_type=F32)
    mode = mode_ref[j]

    @pl.when(mode == MODE_PLAIN)
    def _():
        o_ref[...] = acc

    @pl.when(mode != MODE_PLAIN)
    def _():
        gain = jnp.where(mode == MODE_NORM_Q, gain_ref[0:1, :], gain_ref[1:2, :])
        cos = cos_ref[...]
        sin = sin_ref[...]
        lane = lax.broadcasted_iota(I32, (IN_BM, LANES), 1)
        first = (lane % 64) < 32
        for hh in range(IN_BN // HEAD_DIM):
            xh = acc[:, hh * HEAD_DIM:(hh + 1) * HEAD_DIM]
            rs = lax.rsqrt(jnp.mean(xh * xh, axis=-1, keepdims=True) + RMS_EPS)
            y = jnp.where(mode == MODE_ROPE, xh, xh * rs * gain)
            sw = jnp.where(first, pltpu.roll(y, 96, 1), pltpu.roll(y, 32, 1))
            o_ref[:, hh * HEAD_DIM:(hh + 1) * HEAD_DIM] = y * cos + sw * sin


def _in_proj(x, g_pre, shift, scale, w_bf16, modes, gains, cos_tab, sin_tab):
    n = w_bf16.shape[1]
    n_i = N_TOK // IN_BM
    n_prompt_blocks = N_PROMPT // IN_BM
    blocks_per_seq = DEC_SEQ // IN_BM

    def tab_idx(i, j, m):
        return (jnp.where(i < n_prompt_blocks, 0, 1 + (i - n_prompt_blocks) % blocks_per_seq), 0)

    grid_spec = pltpu.PrefetchScalarGridSpec(
        num_scalar_prefetch=1,
        grid=(n_i, n // IN_BN),
        in_specs=[
            pl.BlockSpec((IN_BM, D_MODEL), lambda i, j, m: (i, 0)),
            pl.BlockSpec((1, D_MODEL), lambda i, j, m: (0, 0)),
            pl.BlockSpec((None, 1, D_MODEL), lambda i, j, m: (i * IN_BM // PB_TOK, 0, 0)),
            pl.BlockSpec((None, 1, D_MODEL), lambda i, j, m: (i * IN_BM // PB_TOK, 0, 0)),
            pl.BlockSpec((D_MODEL, IN_BN), lambda i, j, m: (0, j)),
            pl.BlockSpec((8, LANES), lambda i, j, m: (0, 0)),
            pl.BlockSpec((IN_BM, LANES), tab_idx),
            pl.BlockSpec((IN_BM, LANES), tab_idx),
        ],
        out_specs=pl.BlockSpec((IN_BM, IN_BN), lambda i, j, m: (i, j)),
        scratch_shapes=[pltpu.VMEM((IN_BM, D_MODEL), BF16)],
    )
    return pl.pallas_call(
        _inproj_kernel,
        name="in_proj",
        grid_spec=grid_spec,
        out_shape=jax.ShapeDtypeStruct((N_TOK, n), F32),
        compiler_params=_params(("arbitrary", "arbitrary"), 48),
    )(modes, x, g_pre.reshape(1, D_MODEL), shift, scale, w_bf16, gains, cos_tab, sin_tab)


def _rope_tables():
    t = jnp.arange(DEC_SEQ)
    row = (t // GRID_W).astype(F32)
    col = (t % GRID_W).astype(F32)
    n_freq = HEAD_DIM // 4
    inv = ROPE_THETA ** (-jnp.arange(n_freq, dtype=F32) / n_freq)
    ar = row[:, None] * inv
    ac = col[:, None] * inv
    cos = jnp.concatenate([jnp.cos(ar), jnp.cos(ar), jnp.cos(ac), jnp.cos(ac)], axis=-1)
    sin = jnp.concatenate([-jnp.sin(ar), jnp.sin(ar), -jnp.sin(ac), jnp.sin(ac)], axis=-1)
    cos = jnp.concatenate([jnp.ones((IN_BM, HEAD_DIM), F32), cos], axis=0)
    sin = jnp.concatenate([jnp.zeros((IN_BM, HEAD_DIM), F32), sin], axis=0)
    return cos, sin


MM_BM = 1024
MM_BN = 512


def _mm_kernel(a_ref, w_ref, o_ref):
    o_ref[...] = jnp.dot(a_ref[...], w_ref[...], preferred_element_type=F32)


def _matmul(a_bf16, w_bf16):
    m, k = a_bf16.shape
    n = w_bf16.shape[1]
    return pl.pallas_call(
        _mm_kernel,
        name="out_proj",
        grid=(m // MM_BM, n // MM_BN),
        in_specs=[pl.BlockSpec((MM_BM, k), lambda i, j: (i, 0)),
                  pl.BlockSpec((k, MM_BN), lambda i, j: (0, j))],
        out_specs=pl.BlockSpec((MM_BM, MM_BN), lambda i, j: (i, j)),
        out_shape=jax.ShapeDtypeStruct((m, n), F32),
        compiler_params=_params(("arbitrary", "arbitrary"), 48),
    )(a_bf16, w_bf16)


def _softmax_pv(s_list, v_list, sink=None):
    m = jnp.max(s_list[0], axis=-1, keepdims=True)
    for s in s_list[1:]:
        m = jnp.maximum(m, jnp.max(s, axis=-1, keepdims=True))
    if sink is not None:
        m = jnp.maximum(m, sink)
    den = None
    out = None
    for s, v in zip(s_list, v_list):
        p = jnp.exp(s - m)
        d = jnp.sum(p, axis=-1, keepdims=True)
        o = jnp.dot(p.astype(BF16), v, preferred_element_type=F32)
        den = d if den is None else den + d
        out = o if out is None else out + o
    if sink is not None:
        den = den + jnp.exp(sink - m)
    return out / den


def _qk(q, k):
    return lax.dot_general(q, k, (((1,), (1,)), ((), ())), preferred_element_type=F32) * ATTN_SCALE


def _ctx_attn_kernel(sink_ref, q_ref, k_ref, v_ref, o_ref, *, nkv, g):
    kb = pl.program_id(1)
    for kv in range(nkv):
        k = k_ref[:, kv * HEAD_DIM:(kv + 1) * HEAD_DIM].astype(BF16)
        v = v_ref[:, kv * HEAD_DIM:(kv + 1) * HEAD_DIM].astype(BF16)
        for gi in range(g):
            c0 = (kv * g + gi) * HEAD_DIM
            q = q_ref[:, c0:c0 + HEAD_DIM].astype(BF16)
            sink = sink_ref[(kb * nkv + kv) * g + gi]
            o = _softmax_pv([_qk(q, k)], [v], sink)
            o_ref[:, c0:c0 + HEAD_DIM] = o.astype(o_ref.dtype)


def _ctx_attention(proj, sink, q_col, k_col, v_col, hq, hkv, nkv):
    g = hq // hkv
    qw = nkv * g * HEAD_DIM
    kw = nkv * HEAD_DIM
    return pl.pallas_call(
        functools.partial(_ctx_attn_kernel, nkv=nkv, g=g),
        name="ctx_attn",
        grid=(BATCH, hkv // nkv),
        in_specs=[
            pl.BlockSpec(memory_space=pltpu.SMEM),
            pl.BlockSpec((SEQ, qw), lambda b, h: (b, q_col // qw + h)),
            pl.BlockSpec((SEQ, kw), lambda b, h: (b, k_col // kw + h)),
            pl.BlockSpec((SEQ, kw), lambda b, h: (b, v_col // kw + h)),
        ],
        out_specs=pl.BlockSpec((SEQ, qw), lambda b, h: (b, h)),
        out_shape=jax.ShapeDtypeStruct((N_PROMPT, hq * HEAD_DIM), BF16),
        compiler_params=_params(("arbitrary", "arbitrary"), 40),
    )(sink, proj, proj, proj)


DENSE_BQ = 256


def _dense_attn_kernel(q_ref, k_ref, v_ref, kc_ref, vc_ref, o_ref, *, g):
    k = k_ref[...].astype(BF16)
    v = v_ref[...].astype(BF16)
    kc = kc_ref[...].astype(BF16)
    vc = vc_ref[...].astype(BF16)
    for gi in range(g):
        c0 = gi * HEAD_DIM
        q = q_ref[:, c0:c0 + HEAD_DIM].astype(BF16)
        o = _softmax_pv([_qk(q, k), _qk(q, kc)], [v, vc])
        o_ref[:, c0:c0 + HEAD_DIM] = o.astype(o_ref.dtype)


def _dense_attention(proj, cache_k, cache_v, layer_j):
    g = B_HEADS // B_KV_HEADS
    qw = g * HEAD_DIM
    nq = DEC_SEQ // DENSE_BQ
    q_col = 3 * A_HEADS * HEAD_DIM
    k_col = q_col + B_HEADS * HEAD_DIM
    v_col = k_col + B_KV_HEADS * HEAD_DIM
    row0 = N_PROMPT // DENSE_BQ
    ck = cache_k.reshape(DEC_BATCH, -1, PAST_LEN, B_KV_HEADS * HEAD_DIM)
    cv = cache_v.reshape(DEC_BATCH, -1, PAST_LEN, B_KV_HEADS * HEAD_DIM)
    return pl.pallas_call(
        functools.partial(_dense_attn_kernel, g=g),
        name="dense_attn",
        grid=(DEC_BATCH, B_KV_HEADS, nq),
        in_specs=[
            pl.BlockSpec((DENSE_BQ, qw), lambda b, h, i: (row0 + b * nq + i, q_col // qw + h)),
            pl.BlockSpec((DEC_SEQ, HEAD_DIM), lambda b, h, i: (N_PROMPT_PB + b, k_col // HEAD_DIM + h)),
            pl.BlockSpec((DEC_SEQ, HEAD_DIM), lambda b, h, i: (N_PROMPT_PB + b, v_col // HEAD_DIM + h)),
            pl.BlockSpec((None, None, PAST_LEN, HEAD_DIM), lambda b, h, i: (b, layer_j, 0, h)),
            pl.BlockSpec((None, None, PAST_LEN, HEAD_DIM), lambda b, h, i: (b, layer_j, 0, h)),
        ],
        out_specs=pl.BlockSpec((DENSE_BQ, qw), lambda b, h, i: (b * nq + i, h)),
        out_shape=jax.ShapeDtypeStruct((N_SAMPLE, B_HEADS * HEAD_DIM), BF16),
        compiler_params=_params(("arbitrary", "arbitrary", "arbitrary"), 48),
    )(proj, proj, proj, ck, cv)


WIN_BQ = 256
WIN_KEYS = WIN_BQ + 2 * C_WINDOW


def _window_attn_kernel(sink_ref, q_ref, k_ref, v_ref, kc_ref, vc_ref, o_ref, *, g):
    h = pl.program_id(1)
    i = pl.program_id(2)
    q0 = i * WIN_BQ
    k0 = pl.multiple_of(jnp.clip(q0 - C_WINDOW, 0, DEC_SEQ - WIN_KEYS), C_WINDOW)
    k = k_ref[pl.ds(k0, WIN_KEYS), :].astype(BF16)
    v = v_ref[pl.ds(k0, WIN_KEYS), :].astype(BF16)
    kc = kc_ref[...].astype(BF16)
    vc = vc_ref[...].astype(BF16)
    q_pos = q0 + lax.broadcasted_iota(I32, (WIN_BQ, WIN_KEYS), 0)
    k_pos = k0 + lax.broadcasted_iota(I32, (WIN_BQ, WIN_KEYS), 1)
    ok = jnp.abs(q_pos - k_pos) <= C_WINDOW
    for gi in range(g):
        c0 = gi * HEAD_DIM
        q = q_ref[:, c0:c0 + HEAD_DIM].astype(BF16)
        s_lat = jnp.where(ok, _qk(q, k), -jnp.inf)
        o = _softmax_pv([s_lat, _qk(q, kc)], [v, vc], sink_ref[h * g + gi])
        o_ref[:, c0:c0 + HEAD_DIM] = o.astype(o_ref.dtype)


def _window_attention(proj, cache_k, cache_v, sink, layer_j):
    g = C_HEADS // C_KV_HEADS
    qw = g * HEAD_DIM
    nq = DEC_SEQ // WIN_BQ
    k_col = C_HEADS * HEAD_DIM
    v_col = k_col + C_KV_HEADS * HEAD_DIM
    row0 = N_PROMPT // WIN_BQ
    ck = cache_k.reshape(DEC_BATCH, -1, PAST_LEN, C_KV_HEADS * HEAD_DIM)
    cv = cache_v.reshape(DEC_BATCH, -1, PAST_LEN, C_KV_HEADS * HEAD_DIM)
    return pl.pallas_call(
        functools.partial(_window_attn_kernel, g=g),
        name="window_attn",
        grid=(DEC_BATCH, C_KV_HEADS, nq),
        in_specs=[
            pl.BlockSpec(memory_space=pltpu.SMEM),
            pl.BlockSpec((WIN_BQ, qw), lambda b, h, i: (row0 + b * nq + i, h)),
            pl.BlockSpec((DEC_SEQ, HEAD_DIM), lambda b, h, i: (N_PROMPT_PB + b, k_col // HEAD_DIM + h)),
            pl.BlockSpec((DEC_SEQ, HEAD_DIM), lambda b, h, i: (N_PROMPT_PB + b, v_col // HEAD_DIM + h)),
            pl.BlockSpec((None, None, PAST_LEN, HEAD_DIM), lambda b, h, i: (b, layer_j, 0, h)),
            pl.BlockSpec((None, None, PAST_LEN, HEAD_DIM), lambda b, h, i: (b, layer_j, 0, h)),
        ],
        out_specs=pl.BlockSpec((WIN_BQ, qw), lambda b, h, i: (b * nq + i, h)),
        out_shape=jax.ShapeDtypeStruct((N_SAMPLE, C_HEADS * HEAD_DIM), BF16),
        compiler_params=_params(("arbitrary", "arbitrary", "arbitrary"), 48),
    )(sink, proj, proj, proj, ck, cv)


NA_HC = 4
NA_KEYS = WIN_H * GRID_W


def _na_row_start(i):
    return jnp.clip(i - WIN_H // 2, 0, GRID_ROWS - WIN_H)


def _na_attn_kernel(q_ref, k_ref, v_ref, kc_ref, vc_ref, bias_ref, o_ref):
    i = pl.program_id(2)
    r0 = pl.multiple_of(_na_row_start(i) * GRID_W, GRID_W)
    for hh in range(NA_HC):
        cs = slice(hh * HEAD_DIM, (hh + 1) * HEAD_DIM)
        q = q_ref[:, cs].astype(BF16)
        k = k_ref[pl.ds(r0, NA_KEYS), cs].astype(BF16)
        v = v_ref[pl.ds(r0, NA_KEYS), cs].astype(BF16)
        kc = kc_ref[:, cs].astype(BF16)
        vc = vc_ref[:, cs].astype(BF16)
        s_lat = _qk(q, k) + bias_ref[hh]
        o = _softmax_pv([s_lat, _qk(q, kc)], [v, vc])
        o_ref[:, cs] = o.astype(o_ref.dtype)


def _na_bias_table(rpb):
    qc = jnp.arange(GRID_W)[:, None]
    kc = jnp.arange(GRID_W)[None, :]
    ws = jnp.clip(qc - WIN_W // 2, 0, GRID_W - WIN_W)
    ok = (kc >= ws) & (kc < ws + WIN_W)
    dc = jnp.clip(kc - qc + WIN_W - 1, 0, 2 * WIN_W - 2)
    t = rpb[:, :, dc]
    t = jnp.where(ok[None, None], t, -jnp.inf)
    dr = jnp.arange(WIN_H)[:, None] + jnp.arange(WIN_H)[None, :]
    b = t[:, dr]
    b = b.transpose(0, 1, 3, 2, 4).reshape(A_HEADS, WIN_H, GRID_W, NA_KEYS)
    return b.astype(F32)


def _na_attention(proj, cache_k, cache_v, rpb, layer_j):
    cw = NA_HC * HEAD_DIM
    k_col = A_HEADS * HEAD_DIM
    v_col = 2 * A_HEADS * HEAD_DIM
    row0 = N_PROMPT // GRID_W
    ck = cache_k.reshape(DEC_BATCH, -1, PAST_LEN, A_HEADS * HEAD_DIM)
    cv = cache_v.reshape(DEC_BATCH, -1, PAST_LEN, A_HEADS * HEAD_DIM)
    bias = _na_bias_table(rpb)

    def bias_idx(b, h, i):
        return (h, _na_row_start(i) - i + WIN_H - 1, 0, 0)

    return pl.pallas_call(
        _na_attn_kernel,
        name="na_attn",
        grid=(DEC_BATCH, A_HEADS // NA_HC, GRID_ROWS),
        in_specs=[
            pl.BlockSpec((GRID_W, cw), lambda b, h, i: (row0 + b * GRID_ROWS + i, h)),
            pl.BlockSpec((DEC_SEQ, cw), lambda b, h, i: (N_PROMPT_PB + b, k_col // cw + h)),
            pl.BlockSpec((DEC_SEQ, cw), lambda b, h, i: (N_PROMPT_PB + b, v_col // cw + h)),
            pl.BlockSpec((None, None, PAST_LEN, cw), lambda b, h, i: (b, layer_j, 0, h)),
            pl.BlockSpec((None, None, PAST_LEN, cw), lambda b, h, i: (b, layer_j, 0, h)),
            pl.BlockSpec((NA_HC, None, GRID_W, NA_KEYS), bias_idx),
        ],
        out_specs=pl.BlockSpec((GRID_W, cw), lambda b, h, i: (b * GRID_ROWS + i, h)),
        out_shape=jax.ShapeDtypeStruct((N_SAMPLE, A_HEADS * HEAD_DIM), BF16),
        compiler_params=_params(("arbitrary", "arbitrary", "arbitrary"), 48),
    )(proj, proj, proj, ck, cv, bias)


EP_BM = 256


def _rms(x):
    return x * lax.rsqrt(jnp.mean(x * x, axis=-1, keepdims=True) + RMS_EPS)


def _post_mix_kernel(y_ref, x_ref, gate_ref, gpost_ref, gpre_ref, sh_ref, sc_ref,
                     wr_hi_ref, wr_lo_ref, br_ref, xo_ref, h_ref, ti_ref, tg_ref):
    x_new = x_ref[...] + gate_ref[...] * (_rms(y_ref[...]) * gpost_ref[...])
    xo_ref[...] = x_new
    h = (_rms(x_new) * gpre_ref[...]) * (1.0 + sc_ref[...]) + sh_ref[...]
    h_hi = h.astype(BF16)
    h_hi32 = h_hi.astype(F32)
    bits = pltpu.bitcast(h_hi32, U32)
    h_ref[...] = (bits[:, :HALF_D] >> SHIFT16) | (bits[:, HALF_D:] & HI_MASK)
    h_lo = (h - h_hi32).astype(BF16)
    w_hi = wr_hi_ref[...]
    logits = (jnp.dot(h_hi, w_hi, preferred_element_type=F32)
              + jnp.dot(h_lo, w_hi, preferred_element_type=F32)
              + jnp.dot(h_hi, wr_lo_ref[...], preferred_element_type=F32)) + br_ref[...]
    lane = lax.broadcasted_iota(I32, (EP_BM, LANES), 1).astype(F32)
    vals, idxs = [], []
    l = logits
    for _ in range(TOP_K):
        m = jnp.max(l, axis=-1, keepdims=True)
        idx = jnp.min(jnp.where(l == m, lane, float(LANES)), axis=-1, keepdims=True)
        vals.append(m)
        idxs.append(idx)
        l = jnp.where(lane == idx, -jnp.inf, l)
    es = [jnp.exp(v - vals[0]) for v in vals]
    den = es[0] + es[1] + es[2] + es[3]
    ti = jnp.zeros((EP_BM, LANES), F32)
    tg = jnp.zeros((EP_BM, LANES), F32)
    for k in range(TOP_K):
        ti = jnp.where(lane == float(k), idxs[k], ti)
        tg = jnp.where(lane == float(k), es[k] / den, tg)
    ti_ref[...] = ti.astype(I32)
    tg_ref[...] = tg


def _post_mix(y, x, gate, g_post, g_pre, shift, scale, wr_hi, wr_lo, br):
    row = pl.BlockSpec((EP_BM, D_MODEL), lambda i: (i, 0))
    vec = pl.BlockSpec((1, D_MODEL), lambda i: (0, 0))
    mod = pl.BlockSpec((None, 1, D_MODEL), lambda i: (i * EP_BM // PB_TOK, 0, 0))
    wr = pl.BlockSpec((D_MODEL, LANES), lambda i: (0, 0))
    small = pl.BlockSpec((EP_BM, LANES), lambda i: (i, 0))
    packed = pl.BlockSpec((EP_BM, HALF_D), lambda i: (i, 0))
    return pl.pallas_call(
        _post_mix_kernel,
        name="post_mix",
        grid=(N_TOK // EP_BM,),
        in_specs=[row, row, mod, vec, vec, mod, mod, wr, wr, pl.BlockSpec((1, LANES), lambda i: (0, 0))],
        out_specs=[row, packed, small, small],
        out_shape=[jax.ShapeDtypeStruct((N_TOK, D_MODEL), F32),
                   jax.ShapeDtypeStruct((N_TOK, HALF_D), U32),
                   jax.ShapeDtypeStruct((N_TOK, LANES), I32),
                   jax.ShapeDtypeStruct((N_TOK, LANES), F32)],
        compiler_params=_params(("arbitrary",), 48),
    )(y, x, gate, g_post.reshape(1, D_MODEL), g_pre.reshape(1, D_MODEL), shift, scale, wr_hi, wr_lo, br)


CB_TOK = 128
CB_ROWS = CB_TOK * TOP_K
CB_STEPS = N_TOK // CB_TOK


def _combine_kernel(dest_ref, dest_next_ref, e_hbm, tg_ref, x_ref, gate_ref, gpost_ref, xo_ref, buf, sem):
    i = pl.program_id(0)
    slot = i % 2

    def gather(idx_ref, dst_slot):
        def body(t, carry):
            for k in range(TOP_K):
                pltpu.make_async_copy(e_hbm.at[pl.ds(idx_ref[0, t * TOP_K + k], 1)],
                                      buf.at[dst_slot, pl.ds(k * CB_TOK + t, 1)], sem.at[dst_slot]).start()
            return carry
        lax.fori_loop(0, CB_TOK, body, 0, unroll=2)

    @pl.when(i == 0)
    def _():
        gather(dest_ref, 0)

    @pl.when(i + 1 < CB_STEPS)
    def _():
        gather(dest_next_ref, 1 - slot)

    pltpu.make_async_copy(e_hbm.at[pl.ds(0, CB_ROWS)], buf.at[slot], sem.at[slot]).wait()
    tg = tg_ref[...]
    y = buf[slot, 0:CB_TOK, :] * tg[:, 0:1]
    for k in range(1, TOP_K):
        y = y + buf[slot, k * CB_TOK:(k + 1) * CB_TOK, :] * tg[:, k:k + 1]
    xo_ref[...] = x_ref[...] + gate_ref[...] * (_rms(y) * gpost_ref[...])


def _combine_post_ffn(expert_out, dest, top_gate, x, gate, g_post):
    row = pl.BlockSpec((CB_TOK, D_MODEL), lambda i: (i, 0))
    dest3 = dest.reshape(CB_STEPS, 1, CB_ROWS)
    return pl.pallas_call(
        _combine_kernel,
        name="moe_combine",
        grid=(CB_STEPS,),
        in_specs=[
            pl.BlockSpec((None, 1, CB_ROWS), lambda i: (i, 0, 0), memory_space=pltpu.SMEM),
            pl.BlockSpec((None, 1, CB_ROWS), lambda i: (jnp.minimum(i + 1, CB_STEPS - 1), 0, 0),
                         memory_space=pltpu.SMEM),
            pl.BlockSpec(memory_space=pl.ANY),
            pl.BlockSpec((CB_TOK, LANES), lambda i: (i, 0)),
            row,
            pl.BlockSpec((None, 1, D_MODEL), lambda i: (i * CB_TOK // PB_TOK, 0, 0)),
            pl.BlockSpec((1, D_MODEL), lambda i: (0, 0)),
        ],
        out_specs=row,
        out_shape=jax.ShapeDtypeStruct((N_TOK, D_MODEL), F32),
        scratch_shapes=[pltpu.VMEM((2, CB_ROWS, D_MODEL), F32), pltpu.SemaphoreType.DMA((2,))],
        compiler_params=_params(("arbitrary",), 40),
    )(dest3, dest3, expert_out, top_gate, x, gate, g_post.reshape(1, D_MODEL))


def _for_valid_rows(nsub, compute):
    pair = 2 * MOE_SUB

    def body(p, carry):
        compute(pl.ds(pl.multiple_of(p * pair, pair), pair))
        return carry

    lax.fori_loop(0, nsub // 2, body, 0)

    @pl.when(nsub % 2 == 1)
    def _():
        compute(pl.ds(pl.multiple_of((nsub - 1) * MOE_SUB, MOE_SUB), MOE_SUB))


GATHER_UNROLL = 8


def _moe_up_kernel(sbe_ref, sbi_ref, nsub_ref, tok_ref, tok_next_ref, h_hbm, wg_ref, wl_ref, bg_ref, bl_ref,
                   o_ref, xg, xb, wgb, wlb, sem):
    s = pl.program_id(0)
    c = pl.program_id(1)
    nsub = nsub_ref[s]
    slot = s % 2

    def gather(idx_ref, dst_slot, n_sub_blocks):
        def body(j, carry):
            for u in range(GATHER_UNROLL):
                a = j * GATHER_UNROLL + u
                pltpu.make_async_copy(h_hbm.at[pl.ds(idx_ref[0, a], 1)], xg.at[dst_slot, pl.ds(a, 1)],
                                      sem.at[dst_slot]).start()
            return carry
        lax.fori_loop(0, n_sub_blocks * (MOE_SUB // GATHER_UNROLL), body, 0)

    @pl.when((nsub > 0) & (c == 0))
    def _():
        @pl.when(s == 0)
        def _():
            gather(tok_ref, 0, nsub)

        s_next = jnp.minimum(s + 1, MOE_NSB - 1)
        n_next = jnp.where(s + 1 < MOE_NSB, nsub_ref[s_next], 0)

        @pl.when(n_next > 0)
        def _():
            gather(tok_next_ref, 1 - slot, n_next)

        def wait_rows(r, carry):
            pltpu.make_async_copy(h_hbm.at[pl.ds(0, MOE_SUB)], xg.at[slot, pl.ds(0, MOE_SUB)], sem.at[slot]).wait()
            return carry

        lax.fori_loop(0, nsub, wait_rows, 0)

        def unpack(r, carry):
            rows = pl.ds(pl.multiple_of(r * MOE_SUB, MOE_SUB), MOE_SUB)
            w = xg[slot, rows, :]
            xb[rows, 0:HALF_D] = pltpu.bitcast(w << SHIFT16, F32).astype(BF16)
            xb[rows, HALF_D:D_MODEL] = pltpu.bitcast(w & HI_MASK, F32).astype(BF16)
            return carry

        lax.fori_loop(0, nsub, unpack, 0)

    @pl.when(nsub > 0)
    def _():
        wgb[...] = wg_ref[...].astype(BF16)
        wlb[...] = wl_ref[...].astype(BF16)

        def compute(rows):
            xs = xb[rows, :]
            glu = jnp.dot(xs, wgb[...], preferred_element_type=F32) + bg_ref[...]
            lin = jnp.dot(xs, wlb[...], preferred_element_type=F32) + bl_ref[...]
            glu = jnp.minimum(glu, SWIGLU_LIMIT)
            lin = jnp.clip(lin, -SWIGLU_LIMIT, SWIGLU_LIMIT)
            act = glu / (1.0 + jnp.exp(-SWIGLU_ALPHA * glu)) * (lin + 1.0)
            o_ref[rows, :] = act.astype(BF16)

        _for_valid_rows(nsub, compute)


def _moe_up(h_packed, row_tok, sbe, sbi, nsub, w_gate_up, b_gate_up, layer):
    n_c = MOE_D_FF // MOE_FC
    last_c = n_c - 1

    def c_eff(s, c, nsub_ref):
        return jnp.where(nsub_ref[s] > 0, c, last_c)

    grid_spec = pltpu.PrefetchScalarGridSpec(
        num_scalar_prefetch=3,
        grid=(MOE_NSB, n_c),
        in_specs=[
            pl.BlockSpec((None, 1, MOE_R), lambda s, c, e, i, n: (i[s], 0, 0), memory_space=pltpu.SMEM),
            pl.BlockSpec((None, 1, MOE_R), lambda s, c, e, i, n: (i[jnp.minimum(s + 1, MOE_NSB - 1)], 0, 0),
                         memory_space=pltpu.SMEM),
            pl.BlockSpec(memory_space=pl.ANY),
            pl.BlockSpec((None, None, D_MODEL, MOE_FC), lambda s, c, e, i, n: (layer, e[s], 0, c_eff(s, c, n))),
            pl.BlockSpec((None, None, D_MODEL, MOE_FC),
                         lambda s, c, e, i, n: (layer, e[s], 0, n_c + c_eff(s, c, n))),
            pl.BlockSpec((None, None, 1, MOE_FC), lambda s, c, e, i, n: (layer, e[s], 0, c_eff(s, c, n))),
            pl.BlockSpec((None, None, 1, MOE_FC), lambda s, c, e, i, n: (layer, e[s], 0, n_c + c_eff(s, c, n))),
        ],
        out_specs=pl.BlockSpec((MOE_R, MOE_FC), lambda s, c, e, i, n: (i[s], c_eff(s, c, n))),
        scratch_shapes=[pltpu.VMEM((2, MOE_R, HALF_D), U32), pltpu.VMEM((MOE_R, D_MODEL), BF16),
                        pltpu.VMEM((D_MODEL, MOE_FC), BF16), pltpu.VMEM((D_MODEL, MOE_FC), BF16),
                        pltpu.SemaphoreType.DMA((2,))],
    )
    bgu = b_gate_up.reshape(DEPTH, N_EXPERTS, 1, 2 * MOE_D_FF)
    tok3 = row_tok.reshape(MOE_NSB, 1, MOE_R)
    return pl.pallas_call(
        _moe_up_kernel,
        name="moe_up",
        grid_spec=grid_spec,
        out_shape=jax.ShapeDtypeStruct((MOE_CAP, MOE_D_FF), BF16),
        compiler_params=_params(("arbitrary", "arbitrary"), 56),
    )(sbe, sbi, nsub, tok3, tok3, h_packed, w_gate_up, w_gate_up, bgu, bgu)


def _moe_down_kernel(sbe_ref, sbi_ref, nsub_ref, h_ref, w_ref, b_ref, o_ref, wb):
    s = pl.program_id(0)
    nsub = nsub_ref[s]

    @pl.when(nsub > 0)
    def _():
        wb[...] = w_ref[...].astype(BF16)

        def compute(rows):
            o_ref[rows, :] = jnp.dot(h_ref[rows, :], wb[...], preferred_element_type=F32) + b_ref[...]

        _for_valid_rows(nsub, compute)


def _moe_down(h_sorted, sbe, sbi, nsub, w_down, b_down, layer):
    n_c = D_MODEL // MOE_NC
    last_c = n_c - 1

    def c_eff(s, c, nsub_ref):
        return jnp.where(nsub_ref[s] > 0, c, last_c)

    grid_spec = pltpu.PrefetchScalarGridSpec(
        num_scalar_prefetch=3,
        grid=(MOE_NSB, n_c),
        in_specs=[
            pl.BlockSpec((MOE_R, MOE_D_FF), lambda s, c, e, i, n: (i[s], 0)),
            pl.BlockSpec((None, None, MOE_D_FF, MOE_NC), lambda s, c, e, i, n: (layer, e[s], 0, c_eff(s, c, n))),
            pl.BlockSpec((None, None, 1, MOE_NC), lambda s, c, e, i, n: (layer, e[s], 0, c_eff(s, c, n))),
        ],
        out_specs=pl.BlockSpec((MOE_R, MOE_NC), lambda s, c, e, i, n: (i[s], c_eff(s, c, n))),
        scratch_shapes=[pltpu.VMEM((MOE_D_FF, MOE_NC), BF16)],
    )
    return pl.pallas_call(
        _moe_down_kernel,
        name="moe_down",
        grid_spec=grid_spec,
        out_shape=jax.ShapeDtypeStruct((MOE_CAP, D_MODEL), F32),
        compiler_params=_params(("arbitrary", "arbitrary"), 48),
    )(sbe, sbi, nsub, h_sorted, w_down, b_down.reshape(DEPTH, N_EXPERTS, 1, D_MODEL))


def _route(top_idx):
    flat_e = top_idx.reshape(-1)
    onehot = (flat_e[:, None] == jnp.arange(N_EXPERTS, dtype=I32)[None, :]).astype(I32)
    csum = jnp.cumsum(onehot, axis=0)
    rank = jnp.take_along_axis(csum, flat_e[:, None], axis=1)[:, 0] - 1
    counts = csum[-1]
    nsb = (counts + MOE_R - 1) // MOE_R
    sb_end = jnp.cumsum(nsb)
    sb_off = sb_end - nsb
    dest = sb_off[flat_e] * MOE_R + rank
    n_real = sb_end[-1]
    s = jnp.arange(MOE_NSB, dtype=I32)
    s_eff = jnp.minimum(s, n_real - 1)
    sbe = jnp.minimum(jnp.searchsorted(sb_end, s_eff, side='right'), N_EXPERTS - 1).astype(I32)
    valid = jnp.clip(counts[sbe] - (s_eff - sb_off[sbe]) * MOE_R, 0, MOE_R)
    valid = jnp.where(s < n_real, valid, 0)
    nsub = ((valid + MOE_SUB - 1) // MOE_SUB).astype(I32)
    dest = dest.astype(I32)
    row_tok = jnp.zeros((MOE_CAP,), I32).at[dest].set(jnp.arange(N_ASSIGN, dtype=I32) // TOP_K)
    return dest, row_tok, sbe, s_eff.astype(I32), nsub


def _moe_experts(h_packed, top_idx, w_gate_up, b_gate_up, w_down, b_down, layer):
    dest, row_tok, sbe, sbi, nsub = _route(top_idx)
    h_sorted = _moe_up(h_packed, row_tok, sbe, sbi, nsub, w_gate_up, b_gate_up, layer)
    return _moe_down(h_sorted, sbe, sbi, nsub, w_down, b_down, layer), dest


def _pb_rows(mod_piece):
    idx = jnp.array([0] * N_PROMPT_PB + list(range(1, 1 + DEC_BATCH)), I32)
    return mod_piece[idx][:, None, :]


def kernel(x_prompt, x_sample, cache_a_k, cache_a_v, cache_b_k, cache_b_v, cache_c_k, cache_c_v, c, c_ctx, w_ada, b_ada, g_pre_mix, g_post_mix, g_pre_ffn, g_post_ffn, w_in_even, w_out_even, rpb_a, q_norm_b, k_norm_b, w_in_odd, w_out_odd, sink_c, w_router, b_router, w_gate_up, b_gate_up, w_down, b_down):
    x = jnp.concatenate([x_prompt.reshape(N_PROMPT, D_MODEL), x_sample.reshape(N_SAMPLE, D_MODEL)], axis=0)
    cond8 = jnp.concatenate([c_ctx[None, :], c, jnp.zeros((8 - 1 - DEC_BATCH, D_MODEL), F32)], axis=0)
    mods = _ada_mod(cond8, w_ada, b_ada)
    cos_tab, sin_tab = _rope_tables()
    no_sink = jnp.full((max(A_HEADS, B_HEADS),), -jnp.inf, F32)
    states = {}

    for l in range(DEPTH):
        j = l // 2
        m = [_pb_rows(mods[l, :, k * D_MODEL:(k + 1) * D_MODEL]) for k in range(6)]
        if l % 2 == 0:
            w_in = w_in_even[j].astype(BF16)
            w_out = w_out_even[j].astype(BF16)
            nb = EVEN_IN // IN_BN
            qb0 = 3 * A_HEADS * HEAD_DIM // IN_BN
            kb0 = qb0 + B_HEADS * HEAD_DIM // IN_BN
            vb0 = kb0 + B_KV_HEADS * HEAD_DIM // IN_BN
            modes = jnp.array([MODE_PLAIN] * qb0 + [MODE_NORM_Q] * (kb0 - qb0) + [MODE_NORM_K] * (vb0 - kb0)
                              + [MODE_PLAIN] * (nb - vb0), I32)
            gains = jnp.concatenate([q_norm_b[j][None], k_norm_b[j][None], jnp.ones((6, HEAD_DIM), F32)], axis=0)
        else:
            w_in = w_in_odd[j].astype(BF16)
            w_out = w_out_odd[j].astype(BF16)
            nb = ODD_IN // IN_BN
            vc0 = (C_HEADS + C_KV_HEADS) * HEAD_DIM // IN_BN
            modes = jnp.array([MODE_ROPE] * vc0 + [MODE_PLAIN] * (nb - vc0), I32)
            gains = jnp.ones((8, HEAD_DIM), F32)

        proj = _in_proj(x, g_pre_mix[l], m[0], m[1], w_in, modes, gains, cos_tab, sin_tab)

        if l % 2 == 0:
            ah = A_HEADS * HEAD_DIM
            ctx_a = _ctx_attention(proj, no_sink, 0, ah, 2 * ah, A_HEADS, A_HEADS, 8)
            ctx_b = _ctx_attention(proj, no_sink, 3 * ah, 3 * ah + B_HEADS * HEAD_DIM,
                                   3 * ah + (B_HEADS + B_KV_HEADS) * HEAD_DIM, B_HEADS, B_KV_HEADS, B_KV_HEADS)
            lat_a = _na_attention(proj, cache_a_k, cache_a_v, rpb_a[j], j)
            lat_b = _dense_attention(proj, cache_b_k, cache_b_v, j)
            attn = jnp.concatenate([jnp.concatenate([ctx_a, ctx_b], axis=1),
                                    jnp.concatenate([lat_a, lat_b], axis=1)], axis=0)
            pp = proj[:N_PROMPT]
            kb_c = 3 * ah + B_HEADS * HEAD_DIM
            vb_c = kb_c + B_KV_HEADS * HEAD_DIM
            states.setdefault('ak', []).append(pp[:, ah:2 * ah].reshape(BATCH, SEQ, A_HEADS, HEAD_DIM))
            states.setdefault('av', []).append(pp[:, 2 * ah:3 * ah].reshape(BATCH, SEQ, A_HEADS, HEAD_DIM))
            states.setdefault('bk', []).append(pp[:, kb_c:vb_c].reshape(BATCH, SEQ, B_KV_HEADS, HEAD_DIM))
            states.setdefault('bv', []).append(pp[:, vb_c:].reshape(BATCH, SEQ, B_KV_HEADS, HEAD_DIM))
        else:
            qc = C_HEADS * HEAD_DIM
            kvw = C_KV_HEADS * HEAD_DIM
            ctx_c = _ctx_attention(proj, sink_c[j], 0, qc, qc + kvw, C_HEADS, C_KV_HEADS, C_KV_HEADS)
            lat_c = _window_attention(proj, cache_c_k, cache_c_v, sink_c[j], j)
            attn = jnp.concatenate([ctx_c, lat_c], axis=0)
            pp = proj[:N_PROMPT]
            states.setdefault('ck', []).append(pp[:, qc:qc + kvw].reshape(BATCH, SEQ, C_KV_HEADS, HEAD_DIM))
            states.setdefault('cv', []).append(pp[:, qc + kvw:].reshape(BATCH, SEQ, C_KV_HEADS, HEAD_DIM))

        y = _matmul(attn, w_out)
        wr = jnp.pad(w_router[l], ((0, 0), (0, LANES - N_EXPERTS)))
        wr_hi = wr.astype(BF16)
        wr_lo = (wr - wr_hi.astype(F32)).astype(BF16)
        br = jnp.concatenate([b_router[l], jnp.full((LANES - N_EXPERTS,), NEG_BIG, F32)])[None, :]
        x, h_ffn, top_i, top_g = _post_mix(y, x, m[2], g_post_mix[l], g_pre_ffn[l], m[3], m[4], wr_hi, wr_lo, br)
        expert_out, dest = _moe_experts(h_ffn, top_i[:, :TOP_K], w_gate_up, b_gate_up, w_down, b_down, l)
        x = _combine_post_ffn(expert_out, dest, top_g, x, m[5], g_post_ffn[l])

    y_prompt = x[:N_PROMPT].reshape(BATCH, SEQ, D_MODEL)
    y_sample = x[N_PROMPT:].reshape(DEC_BATCH, DEC_SEQ, D_MODEL)
    st = {k: jnp.stack(v, axis=1) for k, v in states.items()}
    return (y_prompt, y_sample, st['ak'], st['av'], st['bk'], st['bv'], st['ck'], st['cv'])
```

```python
import functools

import jax
import jax.numpy as jnp
import numpy as np
from jax import lax
from jax.experimental import pallas as pl
from jax.experimental.pallas import tpu as pltpu

F32 = jnp.float32
BF16 = jnp.bfloat16
I32 = jnp.int32
U32 = jnp.uint32

D_MODEL = 4096
BATCH = 16
SEQ = 256
DEPTH = 2
DEC_BATCH = 4
DEC_SEQ = 2048
PAST_LEN = 512
GRID_W = 64
GRID_ROWS = DEC_SEQ // GRID_W
HEAD_DIM = 128
A_HEADS = 16
B_HEADS = 16
B_KV_HEADS = 4
C_HEADS = 32
C_KV_HEADS = 4
WIN_H = 8
WIN_W = 16
C_WINDOW = 128
N_EXPERTS = 32
TOP_K = 4
MOE_D_FF = 2048
SWIGLU_LIMIT = 7.0
SWIGLU_ALPHA = 1.702
ROPE_THETA = 10000.0
RMS_EPS = 1e-6
ATTN_SCALE = HEAD_DIM ** -0.5
EVEN_IN = (3 * A_HEADS + B_HEADS + 2 * B_KV_HEADS) * HEAD_DIM
ODD_IN = (C_HEADS + 2 * C_KV_HEADS) * HEAD_DIM

N_PROMPT = BATCH * SEQ
N_SAMPLE = DEC_BATCH * DEC_SEQ
N_TOK = N_PROMPT + N_SAMPLE
PB_TOK = DEC_SEQ
N_PB = N_TOK // PB_TOK
N_PROMPT_PB = N_PROMPT // PB_TOK

LANES = 128
MIB = 1024 * 1024

MOE_R = 1024
MOE_SUB = 256
N_ASSIGN = N_TOK * TOP_K
MOE_NSB = N_ASSIGN // MOE_R + N_EXPERTS
MOE_CAP = MOE_NSB * MOE_R
MOE_FC = 256
MOE_NC = 512

NEG_BIG = -1e30
HALF_D = D_MODEL // 2
HI_MASK = np.uint32(0xFFFF0000)
SHIFT16 = np.uint32(16)


def _params(sem, vmem_mib):
    return pltpu.CompilerParams(dimension_semantics=sem, vmem_limit_bytes=vmem_mib * MIB)


def _ada_kernel(c_ref, w_ref, b_ref, o_ref):
    c = c_ref[...]
    s = (c / (1.0 + jnp.exp(-c))).astype(BF16)
    o_ref[...] = jnp.dot(s, w_ref[...].astype(BF16), preferred_element_type=F32) + b_ref[...]


def _ada_mod(cond8, w_ada, b_ada):
    bn = 512
    n_out = 6 * D_MODEL
    return pl.pallas_call(
        _ada_kernel,
        name="ada_mod",
        grid=(DEPTH, n_out // bn),
        in_specs=[
            pl.BlockSpec((8, D_MODEL), lambda l, j: (0, 0)),
            pl.BlockSpec((None, D_MODEL, bn), lambda l, j: (l, 0, j)),
            pl.BlockSpec((None, 1, bn), lambda l, j: (l, 0, j)),
        ],
        out_specs=pl.BlockSpec((None, 8, bn), lambda l, j: (l, 0, j)),
        out_shape=jax.ShapeDtypeStruct((DEPTH, 8, n_out), F32),
        compiler_params=_params(("arbitrary", "arbitrary"), 40),
    )(cond8, w_ada, b_ada.reshape(DEPTH, 1, n_out))


IN_BM = 512
IN_BN = 512
MODE_PLAIN, MODE_NORM_Q, MODE_NORM_K, MODE_ROPE = 0, 1, 2, 3


def _inproj_kernel(mode_ref, x_ref, g_ref, sh_ref, sc_ref, w_ref, gain_ref, cos_ref, sin_ref,
                   o_ref, h_ref):
    j = pl.program_id(1)

    @pl.when(j == 0)
    def _():
        x = x_ref[...]
        r = lax.rsqrt(jnp.mean(x * x, axis=-1, keepdims=True) + RMS_EPS)
        h = (x * r * g_ref[...]) * (1.0 + sc_ref[...]) + sh_ref[...]
        h_ref[...] = h.astype(BF16)

    acc = jnp.dot(h_ref[...], w_ref[...], preferred_element_type=F32)
    mode = mode_ref[j]

    @pl.when(mode == MODE_PLAIN)
    def _():
        o_ref[...] = acc

    def rope_store(hh, y):
        lane = lax.broadcasted_iota(I32, (IN_BM, LANES), 1)
        sw = jnp.where((lane % 64) < 32, pltpu.roll(y, 96, 1), pltpu.roll(y, 32, 1))
        o_ref[:, hh * HEAD_DIM:(hh + 1) * HEAD_DIM] = y * cos_ref[...] + sw * sin_ref[...]

    @pl.when(mode == MODE_ROPE)
    def _():
        for hh in range(IN_BN // HEAD_DIM):
            rope_store(hh, acc[:, hh * HEAD_DIM:(hh + 1) * HEAD_DIM])

    @pl.when((mode == MODE_NORM_Q) | (mode == MODE_NORM_K))
    def _():
        gain = jnp.where(mode == MODE_NORM_Q, gain_ref[0:1, :], gain_ref[1:2, :])
        for hh in range(IN_BN // HEAD_DIM):
            xh = acc[:, hh * HEAD_DIM:(hh + 1) * HEAD_DIM]
            rs = lax.rsqrt(jnp.mean(xh * xh, axis=-1, keepdims=True) + RMS_EPS)
            rope_store(hh, xh * rs * gain)


def _in_proj(x, g_pre, shift, scale, w_bf16, modes, gains, cos_tab, sin_tab):
    n = w_bf16.shape[1]
    n_i = N_TOK // IN_BM
    n_prompt_blocks = N_PROMPT // IN_BM
    blocks_per_seq = DEC_SEQ // IN_BM

    def tab_idx(i, j, m):
        return (jnp.where(i < n_prompt_blocks, 0, 1 + (i - n_prompt_blocks) % blocks_per_seq), 0)

    grid_spec = pltpu.PrefetchScalarGridSpec(
        num_scalar_prefetch=1,
        grid=(n_i, n // IN_BN),
        in_specs=[
            pl.BlockSpec((IN_BM, D_MODEL), lambda i, j, m: (i, 0)),
            pl.BlockSpec((1, D_MODEL), lambda i, j, m: (0, 0)),
            pl.BlockSpec((None, 1, D_MODEL), lambda i, j, m: (i * IN_BM // PB_TOK, 0, 0)),
            pl.BlockSpec((None, 1, D_MODEL), lambda i, j, m: (i * IN_BM // PB_TOK, 0, 0)),
            pl.BlockSpec((D_MODEL, IN_BN), lambda i, j, m: (0, j)),
            pl.BlockSpec((8, LANES), lambda i, j, m: (0, 0)),
            pl.BlockSpec((IN_BM, LANES), tab_idx),
            pl.BlockSpec((IN_BM, LANES), tab_idx),
        ],
        out_specs=pl.BlockSpec((IN_BM, IN_BN), lambda i, j, m: (i, j)),
        scratch_shapes=[pltpu.VMEM((IN_BM, D_MODEL), BF16)],
    )
    return pl.pallas_call(
        _inproj_kernel,
        name="in_proj",
        grid_spec=grid_spec,
        out_shape=jax.ShapeDtypeStruct((N_TOK, n), F32),
        compiler_params=_params(("arbitrary", "arbitrary"), 48),
    )(modes, x, g_pre.reshape(1, D_MODEL), shift, scale, w_bf16, gains, cos_tab, sin_tab)


def _rope_tables():
    t = jnp.arange(DEC_SEQ)
    row = (t // GRID_W).astype(F32)
    col = (t % GRID_W).astype(F32)
    n_freq = HEAD_DIM // 4
    inv = ROPE_THETA ** (-jnp.arange(n_freq, dtype=F32) / n_freq)
    ar = row[:, None] * inv
    ac = col[:, None] * inv
    cos = jnp.concatenate([jnp.cos(ar), jnp.cos(ar), jnp.cos(ac), jnp.cos(ac)], axis=-1)
    sin = jnp.concatenate([-jnp.sin(ar), jnp.sin(ar), -jnp.sin(ac), jnp.sin(ac)], axis=-1)
    cos = jnp.concatenate([jnp.ones((IN_BM, HEAD_DIM), F32), cos], axis=0)
    sin = jnp.concatenate([jnp.zeros((IN_BM, HEAD_DIM), F32), sin], axis=0)
    return cos, sin


MM_BM = 1024
MM_BN = 512


MM_PROMPT_BLOCKS = N_PROMPT // MM_BM


def _out_proj_kernel(*refs, n_parts):
    prompt_refs = refs[:n_parts]
    latent_refs = refs[n_parts:2 * n_parts]
    w_ref, o_ref = refs[2 * n_parts:]
    i = pl.program_id(0)

    def run(parts):
        acc = None
        k0 = 0
        for r in parts:
            kk = r.shape[1]
            d = jnp.dot(r[...], w_ref[k0:k0 + kk, :], preferred_element_type=F32)
            acc = d if acc is None else acc + d
            k0 += kk
        o_ref[...] = acc

    @pl.when(i < MM_PROMPT_BLOCKS)
    def _():
        run(prompt_refs)

    @pl.when(i >= MM_PROMPT_BLOCKS)
    def _():
        run(latent_refs)


def _out_proj(prompt_parts, latent_parts, w_bf16):
    k, n = w_bf16.shape
    n_parts = len(prompt_parts)
    last_prompt = MM_PROMPT_BLOCKS - 1
    in_specs = ([pl.BlockSpec((MM_BM, p.shape[1]), lambda i, j: (jnp.minimum(i, last_prompt), 0))
                 for p in prompt_parts]
                + [pl.BlockSpec((MM_BM, p.shape[1]), lambda i, j: (jnp.maximum(i - MM_PROMPT_BLOCKS, 0), 0))
                   for p in latent_parts]
                + [pl.BlockSpec((k, MM_BN), lambda i, j: (0, j))])
    return pl.pallas_call(
        functools.partial(_out_proj_kernel, n_parts=n_parts),
        name="out_proj",
        grid=(N_TOK // MM_BM, n // MM_BN),
        in_specs=in_specs,
        out_specs=pl.BlockSpec((MM_BM, MM_BN), lambda i, j: (i, j)),
        out_shape=jax.ShapeDtypeStruct((N_TOK, n), F32),
        compiler_params=_params(("arbitrary", "arbitrary"), 56),
    )(*prompt_parts, *latent_parts, w_bf16)


def _softmax_pv(s_list, v_list, sink=None):
    m = jnp.max(s_list[0], axis=-1, keepdims=True)
    for s in s_list[1:]:
        m = jnp.maximum(m, jnp.max(s, axis=-1, keepdims=True))
    if sink is not None:
        m = jnp.maximum(m, sink)
    den = None
    out = None
    for s, v in zip(s_list, v_list):
        p = jnp.exp(s - m)
        d = jnp.sum(p, axis=-1, keepdims=True)
        o = jnp.dot(p.astype(BF16), v, preferred_element_type=F32)
        den = d if den is None else den + d
        out = o if out is None else out + o
    if sink is not None:
        den = den + jnp.exp(sink - m)
    return out / den


def _qk(q, k):
    return lax.dot_general(q, k, (((1,), (1,)), ((), ())), preferred_element_type=F32) * ATTN_SCALE


def _ctx_attn_kernel(sink_ref, q_ref, k_ref, v_ref, o_ref, *, nkv, g):
    kb = pl.program_id(1)
    for kv in range(nkv):
        k = k_ref[:, kv * HEAD_DIM:(kv + 1) * HEAD_DIM].astype(BF16)
        v = v_ref[:, kv * HEAD_DIM:(kv + 1) * HEAD_DIM].astype(BF16)
        for gi in range(g):
            c0 = (kv * g + gi) * HEAD_DIM
            q = q_ref[:, c0:c0 + HEAD_DIM].astype(BF16)
            sink = sink_ref[(kb * nkv + kv) * g + gi]
            o = _softmax_pv([_qk(q, k)], [v], sink)
            o_ref[:, c0:c0 + HEAD_DIM] = o.astype(o_ref.dtype)


def _ctx_attention(proj, sink, q_col, k_col, v_col, hq, hkv, nkv):
    g = hq // hkv
    qw = nkv * g * HEAD_DIM
    kw = nkv * HEAD_DIM
    return pl.pallas_call(
        functools.partial(_ctx_attn_kernel, nkv=nkv, g=g),
        name="ctx_attn",
        grid=(BATCH, hkv // nkv),
        in_specs=[
            pl.BlockSpec(memory_space=pltpu.SMEM),
            pl.BlockSpec((SEQ, qw), lambda b, h: (b, q_col // qw + h)),
            pl.BlockSpec((SEQ, kw), lambda b, h: (b, k_col // kw + h)),
            pl.BlockSpec((SEQ, kw), lambda b, h: (b, v_col // kw + h)),
        ],
        out_specs=pl.BlockSpec((SEQ, qw), lambda b, h: (b, h)),
        out_shape=jax.ShapeDtypeStruct((N_PROMPT, hq * HEAD_DIM), BF16),
        compiler_params=_params(("arbitrary", "arbitrary"), 40),
    )(sink, proj, proj, proj)


DENSE_BQ = 256


def _cache_bf16(step, srcs, dsts):
    @pl.when(step == 0)
    def _():
        for src, dst in zip(srcs, dsts):
            dst[...] = src[...].astype(BF16)


def _kv_scratch(width):
    return [pltpu.VMEM((DEC_SEQ, width), BF16), pltpu.VMEM((DEC_SEQ, width), BF16),
            pltpu.VMEM((PAST_LEN, width), BF16), pltpu.VMEM((PAST_LEN, width), BF16)]


def _dense_attn_kernel(q_ref, k_ref, v_ref, kc_ref, vc_ref, o_ref, kb, vb, kcb, vcb, *, g):
    _cache_bf16(pl.program_id(2), (k_ref, v_ref, kc_ref, vc_ref), (kb, vb, kcb, vcb))
    k = kb[...]
    v = vb[...]
    kc = kcb[...]
    vc = vcb[...]
    for gi in range(g):
        c0 = gi * HEAD_DIM
        q = q_ref[:, c0:c0 + HEAD_DIM].astype(BF16)
        o = _softmax_pv([_qk(q, k), _qk(q, kc)], [v, vc])
        o_ref[:, c0:c0 + HEAD_DIM] = o.astype(o_ref.dtype)


def _dense_attention(proj, cache_k, cache_v, layer_j):
    g = B_HEADS // B_KV_HEADS
    qw = g * HEAD_DIM
    nq = DEC_SEQ // DENSE_BQ
    q_col = 3 * A_HEADS * HEAD_DIM
    k_col = q_col + B_HEADS * HEAD_DIM
    v_col = k_col + B_KV_HEADS * HEAD_DIM
    row0 = N_PROMPT // DENSE_BQ
    ck = cache_k.reshape(DEC_BATCH, -1, PAST_LEN, B_KV_HEADS * HEAD_DIM)
    cv = cache_v.reshape(DEC_BATCH, -1, PAST_LEN, B_KV_HEADS * HEAD_DIM)
    return pl.pallas_call(
        functools.partial(_dense_attn_kernel, g=g),
        name="dense_attn",
        grid=(DEC_BATCH, B_KV_HEADS, nq),
        in_specs=[
            pl.BlockSpec((DENSE_BQ, qw), lambda b, h, i: (row0 + b * nq + i, q_col // qw + h)),
            pl.BlockSpec((DEC_SEQ, HEAD_DIM), lambda b, h, i: (N_PROMPT_PB + b, k_col // HEAD_DIM + h)),
            pl.BlockSpec((DEC_SEQ, HEAD_DIM), lambda b, h, i: (N_PROMPT_PB + b, v_col // HEAD_DIM + h)),
            pl.BlockSpec((None, None, PAST_LEN, HEAD_DIM), lambda b, h, i: (b, layer_j, 0, h)),
            pl.BlockSpec((None, None, PAST_LEN, HEAD_DIM), lambda b, h, i: (b, layer_j, 0, h)),
        ],
        out_specs=pl.BlockSpec((DENSE_BQ, qw), lambda b, h, i: (b * nq + i, h)),
        scratch_shapes=_kv_scratch(HEAD_DIM),
        out_shape=jax.ShapeDtypeStruct((N_SAMPLE, B_HEADS * HEAD_DIM), BF16),
        compiler_params=_params(("arbitrary", "arbitrary", "arbitrary"), 48),
    )(proj, proj, proj, ck, cv)


WIN_BQ = 256
WIN_KEYS = WIN_BQ + 2 * C_WINDOW


def _window_attn_kernel(sink_ref, q_ref, k_ref, v_ref, kc_ref, vc_ref, o_ref, kb, vb, kcb, vcb, *, g):
    h = pl.program_id(1)
    i = pl.program_id(2)
    _cache_bf16(i, (k_ref, v_ref, kc_ref, vc_ref), (kb, vb, kcb, vcb))
    q0 = i * WIN_BQ
    k0 = pl.multiple_of(jnp.clip(q0 - C_WINDOW, 0, DEC_SEQ - WIN_KEYS), C_WINDOW)
    k = kb[pl.ds(k0, WIN_KEYS), :]
    v = vb[pl.ds(k0, WIN_KEYS), :]
    kc = kcb[...]
    vc = vcb[...]
    q_pos = q0 + lax.broadcasted_iota(I32, (WIN_BQ, WIN_KEYS), 0)
    k_pos = k0 + lax.broadcasted_iota(I32, (WIN_BQ, WIN_KEYS), 1)
    ok = jnp.abs(q_pos - k_pos) <= C_WINDOW
    for gi in range(g):
        c0 = gi * HEAD_DIM
        q = q_ref[:, c0:c0 + HEAD_DIM].astype(BF16)
        s_lat = jnp.where(ok, _qk(q, k), -jnp.inf)
        o = _softmax_pv([s_lat, _qk(q, kc)], [v, vc], sink_ref[h * g + gi])
        o_ref[:, c0:c0 + HEAD_DIM] = o.astype(o_ref.dtype)


def _window_attention(proj, cache_k, cache_v, sink, layer_j):
    g = C_HEADS // C_KV_HEADS
    qw = g * HEAD_DIM
    nq = DEC_SEQ // WIN_BQ
    k_col = C_HEADS * HEAD_DIM
    v_col = k_col + C_KV_HEADS * HEAD_DIM
    row0 = N_PROMPT // WIN_BQ
    ck = cache_k.reshape(DEC_BATCH, -1, PAST_LEN, C_KV_HEADS * HEAD_DIM)
    cv = cache_v.reshape(DEC_BATCH, -1, PAST_LEN, C_KV_HEADS * HEAD_DIM)
    return pl.pallas_call(
        functools.partial(_window_attn_kernel, g=g),
        name="window_attn",
        grid=(DEC_BATCH, C_KV_HEADS, nq),
        in_specs=[
            pl.BlockSpec(memory_space=pltpu.SMEM),
            pl.BlockSpec((WIN_BQ, qw), lambda b, h, i: (row0 + b * nq + i, h)),
            pl.BlockSpec((DEC_SEQ, HEAD_DIM), lambda b, h, i: (N_PROMPT_PB + b, k_col // HEAD_DIM + h)),
            pl.BlockSpec((DEC_SEQ, HEAD_DIM), lambda b, h, i: (N_PROMPT_PB + b, v_col // HEAD_DIM + h)),
            pl.BlockSpec((None, None, PAST_LEN, HEAD_DIM), lambda b, h, i: (b, layer_j, 0, h)),
            pl.BlockSpec((None, None, PAST_LEN, HEAD_DIM), lambda b, h, i: (b, layer_j, 0, h)),
        ],
        out_specs=pl.BlockSpec((WIN_BQ, qw), lambda b, h, i: (b * nq + i, h)),
        scratch_shapes=_kv_scratch(HEAD_DIM),
        out_shape=jax.ShapeDtypeStruct((N_SAMPLE, C_HEADS * HEAD_DIM), BF16),
        compiler_params=_params(("arbitrary", "arbitrary", "arbitrary"), 48),
    )(sink, proj, proj, proj, ck, cv)


NA_HC = 4
NA_KEYS = WIN_H * GRID_W


def _na_row_start(i):
    return jnp.clip(i - WIN_H // 2, 0, GRID_ROWS - WIN_H)


def _na_attn_kernel(q_ref, k_ref, v_ref, kc_ref, vc_ref, bias_ref, o_ref, kb, vb, kcb, vcb):
    i = pl.program_id(2)
    _cache_bf16(i, (k_ref, v_ref, kc_ref, vc_ref), (kb, vb, kcb, vcb))
    r0 = pl.multiple_of(_na_row_start(i) * GRID_W, GRID_W)
    for hh in range(NA_HC):
        cs = slice(hh * HEAD_DIM, (hh + 1) * HEAD_DIM)
        q = q_ref[:, cs].astype(BF16)
        s_lat = _qk(q, kb[pl.ds(r0, NA_KEYS), cs]) + bias_ref[hh]
        o = _softmax_pv([s_lat, _qk(q, kcb[:, cs])], [vb[pl.ds(r0, NA_KEYS), cs], vcb[:, cs]])
        o_ref[:, cs] = o.astype(o_ref.dtype)


def _na_bias_table(rpb):
    qc = jnp.arange(GRID_W)[:, None]
    kc = jnp.arange(GRID_W)[None, :]
    ws = jnp.clip(qc - WIN_W // 2, 0, GRID_W - WIN_W)
    ok = (kc >= ws) & (kc < ws + WIN_W)
    dc = jnp.clip(kc - qc + WIN_W - 1, 0, 2 * WIN_W - 2)
    t = rpb[:, :, dc]
    t = jnp.where(ok[None, None], t, -jnp.inf)
    dr = jnp.arange(WIN_H)[:, None] + jnp.arange(WIN_H)[None, :]
    b = t[:, dr]
    b = b.transpose(0, 1, 3, 2, 4).reshape(A_HEADS, WIN_H, GRID_W, NA_KEYS)
    return b.astype(F32)


def _na_attention(proj, cache_k, cache_v, rpb, layer_j):
    cw = NA_HC * HEAD_DIM
    k_col = A_HEADS * HEAD_DIM
    v_col = 2 * A_HEADS * HEAD_DIM
    row0 = N_PROMPT // GRID_W
    ck = cache_k.reshape(DEC_BATCH, -1, PAST_LEN, A_HEADS * HEAD_DIM)
    cv = cache_v.reshape(DEC_BATCH, -1, PAST_LEN, A_HEADS * HEAD_DIM)
    bias = _na_bias_table(rpb)

    def bias_idx(b, h, i):
        return (h, _na_row_start(i) - i + WIN_H - 1, 0, 0)

    return pl.pallas_call(
        _na_attn_kernel,
        name="na_attn",
        grid=(DEC_BATCH, A_HEADS // NA_HC, GRID_ROWS),
        in_specs=[
            pl.BlockSpec((GRID_W, cw), lambda b, h, i: (row0 + b * GRID_ROWS + i, h)),
            pl.BlockSpec((DEC_SEQ, cw), lambda b, h, i: (N_PROMPT_PB + b, k_col // cw + h)),
            pl.BlockSpec((DEC_SEQ, cw), lambda b, h, i: (N_PROMPT_PB + b, v_col // cw + h)),
            pl.BlockSpec((None, None, PAST_LEN, cw), lambda b, h, i: (b, layer_j, 0, h)),
            pl.BlockSpec((None, None, PAST_LEN, cw), lambda b, h, i: (b, layer_j, 0, h)),
            pl.BlockSpec((NA_HC, None, GRID_W, NA_KEYS), bias_idx),
        ],
        out_specs=pl.BlockSpec((GRID_W, cw), lambda b, h, i: (b * GRID_ROWS + i, h)),
        scratch_shapes=_kv_scratch(cw),
        out_shape=jax.ShapeDtypeStruct((N_SAMPLE, A_HEADS * HEAD_DIM), BF16),
        compiler_params=_params(("arbitrary", "arbitrary", "arbitrary"), 48),
    )(proj, proj, proj, ck, cv, bias)


EP_BM = 256


def _rms(x):
    return x * lax.rsqrt(jnp.mean(x * x, axis=-1, keepdims=True) + RMS_EPS)


def _post_mix_kernel(y_ref, x_ref, gate_ref, gpost_ref, gpre_ref, sh_ref, sc_ref,
                     wr_hi_ref, wr_lo_ref, br_ref, xo_ref, h_ref, ti_ref, tg_ref):
    x_new = x_ref[...] + gate_ref[...] * (_rms(y_ref[...]) * gpost_ref[...])
    xo_ref[...] = x_new
    h = (_rms(x_new) * gpre_ref[...]) * (1.0 + sc_ref[...]) + sh_ref[...]
    h_hi = h.astype(BF16)
    h_hi32 = h_hi.astype(F32)
    bits = pltpu.bitcast(h_hi32, U32)
    h_ref[...] = (bits[:, :HALF_D] >> SHIFT16) | (bits[:, HALF_D:] & HI_MASK)
    h_lo = (h - h_hi32).astype(BF16)
    w_hi = wr_hi_ref[...]
    logits = (jnp.dot(h_hi, w_hi, preferred_element_type=F32)
              + jnp.dot(h_lo, w_hi, preferred_element_type=F32)
              + jnp.dot(h_hi, wr_lo_ref[...], preferred_element_type=F32)) + br_ref[...]
    lane = lax.broadcasted_iota(I32, (EP_BM, LANES), 1).astype(F32)
    vals, idxs = [], []
    l = logits
    for _ in range(TOP_K):
        m = jnp.max(l, axis=-1, keepdims=True)
        idx = jnp.min(jnp.where(l == m, lane, float(LANES)), axis=-1, keepdims=True)
        vals.append(m)
        idxs.append(idx)
        l = jnp.where(lane == idx, -jnp.inf, l)
    es = [jnp.exp(v - vals[0]) for v in vals]
    den = es[0] + es[1] + es[2] + es[3]
    ti = jnp.zeros((EP_BM, LANES), F32)
    tg = jnp.zeros((EP_BM, LANES), F32)
    for k in range(TOP_K):
        ti = jnp.where(lane == float(k), idxs[k], ti)
        tg = jnp.where(lane == float(k), es[k] / den, tg)
    ti_ref[...] = ti.astype(I32)
    tg_ref[...] = tg


def _post_mix(y, x, gate, g_post, g_pre, shift, scale, wr_hi, wr_lo, br):
    row = pl.BlockSpec((EP_BM, D_MODEL), lambda i: (i, 0))
    vec = pl.BlockSpec((1, D_MODEL), lambda i: (0, 0))
    mod = pl.BlockSpec((None, 1, D_MODEL), lambda i: (i * EP_BM // PB_TOK, 0, 0))
    wr = pl.BlockSpec((D_MODEL, LANES), lambda i: (0, 0))
    small = pl.BlockSpec((EP_BM, LANES), lambda i: (i, 0))
    packed = pl.BlockSpec((EP_BM, HALF_D), lambda i: (i, 0))
    return pl.pallas_call(
        _post_mix_kernel,
        name="post_mix",
        grid=(N_TOK // EP_BM,),
        in_specs=[row, row, mod, vec, vec, mod, mod, wr, wr, pl.BlockSpec((1, LANES), lambda i: (0, 0))],
        out_specs=[row, packed, small, small],
        out_shape=[jax.ShapeDtypeStruct((N_TOK, D_MODEL), F32),
                   jax.ShapeDtypeStruct((N_TOK, HALF_D), U32),
                   jax.ShapeDtypeStruct((N_TOK, LANES), I32),
                   jax.ShapeDtypeStruct((N_TOK, LANES), F32)],
        compiler_params=_params(("arbitrary",), 48),
    )(y, x, gate, g_post.reshape(1, D_MODEL), g_pre.reshape(1, D_MODEL), shift, scale, wr_hi, wr_lo, br)


CB_TOK = 128
CB_ROWS = CB_TOK * TOP_K
CB_STEPS = N_TOK // CB_TOK


CB_PROMPT_STEPS = N_PROMPT // CB_TOK


def _combine_kernel(dest_ref, dest_next_ref, e_hbm, tg_ref, x_ref, gate_ref, gpost_ref, *rest, split_out):
    out_refs, (buf, sem) = rest[:-2], rest[-2:]
    i = pl.program_id(0)
    slot = i % 2

    def gather(idx_ref, dst_slot):
        def body(t, carry):
            for k in range(TOP_K):
                pltpu.make_async_copy(e_hbm.at[pl.ds(idx_ref[0, t * TOP_K + k], 1)],
                                      buf.at[dst_slot, pl.ds(k * CB_TOK + t, 1)], sem.at[dst_slot]).start()
            return carry
        lax.fori_loop(0, CB_TOK, body, 0, unroll=2)

    @pl.when(i == 0)
    def _():
        gather(dest_ref, 0)

    @pl.when(i + 1 < CB_STEPS)
    def _():
        gather(dest_next_ref, 1 - slot)

    pltpu.make_async_copy(e_hbm.at[pl.ds(0, CB_ROWS)], buf.at[slot], sem.at[slot]).wait()
    tg = tg_ref[...]
    y = buf[slot, 0:CB_TOK, :] * tg[:, 0:1]
    for k in range(1, TOP_K):
        y = y + buf[slot, k * CB_TOK:(k + 1) * CB_TOK, :] * tg[:, k:k + 1]
    x_new = x_ref[...] + gate_ref[...] * (_rms(y) * gpost_ref[...])
    if split_out:
        @pl.when(i < CB_PROMPT_STEPS)
        def _():
            out_refs[0][...] = x_new

        @pl.when(i >= CB_PROMPT_STEPS)
        def _():
            out_refs[1][...] = x_new
    else:
        out_refs[0][...] = x_new


def _combine_post_ffn(expert_out, dest, top_gate, x, gate, g_post, split_out):
    row = pl.BlockSpec((CB_TOK, D_MODEL), lambda i: (i, 0))
    dest3 = dest.reshape(CB_STEPS, 1, CB_ROWS)
    if split_out:
        out_specs = [pl.BlockSpec((CB_TOK, D_MODEL), lambda i: (jnp.minimum(i, CB_PROMPT_STEPS - 1), 0)),
                     pl.BlockSpec((CB_TOK, D_MODEL), lambda i: (jnp.maximum(i - CB_PROMPT_STEPS, 0), 0))]
        out_shape = [jax.ShapeDtypeStruct((N_PROMPT, D_MODEL), F32), jax.ShapeDtypeStruct((N_SAMPLE, D_MODEL), F32)]
    else:
        out_specs = row
        out_shape = jax.ShapeDtypeStruct((N_TOK, D_MODEL), F32)
    return pl.pallas_call(
        functools.partial(_combine_kernel, split_out=split_out),
        name="moe_combine",
        grid=(CB_STEPS,),
        in_specs=[
            pl.BlockSpec((None, 1, CB_ROWS), lambda i: (i, 0, 0), memory_space=pltpu.SMEM),
            pl.BlockSpec((None, 1, CB_ROWS), lambda i: (jnp.minimum(i + 1, CB_STEPS - 1), 0, 0),
                         memory_space=pltpu.SMEM),
            pl.BlockSpec(memory_space=pl.ANY),
            pl.BlockSpec((CB_TOK, LANES), lambda i: (i, 0)),
            row,
            pl.BlockSpec((None, 1, D_MODEL), lambda i: (i * CB_TOK // PB_TOK, 0, 0)),
            pl.BlockSpec((1, D_MODEL), lambda i: (0, 0)),
        ],
        out_specs=out_specs,
        out_shape=out_shape,
        scratch_shapes=[pltpu.VMEM((2, CB_ROWS, D_MODEL), F32), pltpu.SemaphoreType.DMA((2,))],
        compiler_params=_params(("arbitrary",), 40),
    )(dest3, dest3, expert_out, top_gate, x, gate, g_post.reshape(1, D_MODEL))


def _for_valid_rows(nsub, compute):
    pair = 2 * MOE_SUB

    def body(p, carry):
        compute(pl.multiple_of(p * pair, pair), pair)
        return carry

    lax.fori_loop(0, nsub // 2, body, 0)

    @pl.when(nsub % 2 == 1)
    def _():
        compute(pl.multiple_of((nsub - 1) * MOE_SUB, MOE_SUB), MOE_SUB)


N_FF_CHUNKS = MOE_D_FF // MOE_FC


def _moe_up_kernel(sbe_ref, sbi_ref, nsub_ref, tok_ref, tok_next_ref, h_hbm, wg_ref, wl_ref, bg_ref, bl_ref,
                   o_ref, xg, xb, wgb, wlb, sem):
    s = pl.program_id(0)
    c = pl.program_id(1)
    nsub = nsub_ref[s]
    slot = s % 2
    s_next = jnp.minimum(s + 1, MOE_NSB - 1)
    n_next = jnp.where(s + 1 < MOE_NSB, nsub_ref[s_next], 0)

    def start_row(idx_ref, dst_slot, a):
        pltpu.make_async_copy(h_hbm.at[pl.ds(idx_ref[0, a], 1)], xg.at[dst_slot, pl.ds(a, 1)],
                              sem.at[dst_slot]).start()

    def gather_sub_blocks(idx_ref, dst_slot, first, last):
        def body(j, carry):
            for u in range(N_FF_CHUNKS):
                start_row(idx_ref, dst_slot, j * N_FF_CHUNKS + u)
            return carry
        lax.fori_loop(first * (MOE_SUB // N_FF_CHUNKS), last * (MOE_SUB // N_FF_CHUNKS), body, 0)

    @pl.when((nsub > 0) & (c == 0))
    def _():
        @pl.when(s == 0)
        def _():
            gather_sub_blocks(tok_ref, 0, 0, nsub)

        @pl.when(n_next > nsub)
        def _():
            gather_sub_blocks(tok_next_ref, 1 - slot, nsub, n_next)

        n_prev = nsub_ref[jnp.maximum(s - 1, 0)]
        n_wait = jnp.where(s == 0, nsub, jnp.maximum(n_prev, nsub))

        def wait_rows(r, carry):
            pltpu.make_async_copy(h_hbm.at[pl.ds(0, MOE_SUB)], xg.at[slot, pl.ds(0, MOE_SUB)], sem.at[slot]).wait()
            return carry

        lax.fori_loop(0, n_wait, wait_rows, 0)

        def unpack(r, carry):
            rows = pl.ds(pl.multiple_of(r * MOE_SUB, MOE_SUB), MOE_SUB)
            w = xg[slot, rows, :]
            xb[rows, 0:HALF_D] = pltpu.bitcast(w << SHIFT16, F32).astype(BF16)
            xb[rows, HALF_D:D_MODEL] = pltpu.bitcast(w & HI_MASK, F32).astype(BF16)
            return carry

        lax.fori_loop(0, nsub, unpack, 0)

    def make_compute(prefetch):
        def compute(r0, n):
            rows = pl.ds(r0, n)
            xs = xb[rows, :]
            glu = jnp.dot(xs, wgb[...], preferred_element_type=F32) + bg_ref[...]
            lin = jnp.dot(xs, wlb[...], preferred_element_type=F32) + bl_ref[...]
            if prefetch:
                for j in range(n // N_FF_CHUNKS):
                    start_row(tok_next_ref, 1 - slot, r0 + N_FF_CHUNKS * j + c)
            glu = jnp.minimum(glu, SWIGLU_LIMIT)
            lin = jnp.clip(lin, -SWIGLU_LIMIT, SWIGLU_LIMIT)
            act = glu / (1.0 + jnp.exp(-SWIGLU_ALPHA * glu)) * (lin + 1.0)
            o_ref[rows, :] = act.astype(BF16)
        return compute

    @pl.when(nsub > 0)
    def _():
        wgb[...] = wg_ref[...].astype(BF16)
        wlb[...] = wl_ref[...].astype(BF16)

        @pl.when(n_next > 0)
        def _():
            _for_valid_rows(nsub, make_compute(True))

        @pl.when(n_next == 0)
        def _():
            _for_valid_rows(nsub, make_compute(False))


def _moe_up(h_packed, row_tok, sbe, sbi, nsub, w_gate_up, b_gate_up, layer):
    n_c = MOE_D_FF // MOE_FC
    last_c = n_c - 1

    def c_eff(s, c, nsub_ref):
        return jnp.where(nsub_ref[s] > 0, c, last_c)

    grid_spec = pltpu.PrefetchScalarGridSpec(
        num_scalar_prefetch=3,
        grid=(MOE_NSB, n_c),
        in_specs=[
            pl.BlockSpec((None, 1, MOE_R), lambda s, c, e, i, n: (i[s], 0, 0), memory_space=pltpu.SMEM),
            pl.BlockSpec((None, 1, MOE_R), lambda s, c, e, i, n: (i[jnp.minimum(s + 1, MOE_NSB - 1)], 0, 0),
                         memory_space=pltpu.SMEM),
            pl.BlockSpec(memory_space=pl.ANY),
            pl.BlockSpec((None, None, D_MODEL, MOE_FC), lambda s, c, e, i, n: (layer, e[s], 0, c_eff(s, c, n))),
            pl.BlockSpec((None, None, D_MODEL, MOE_FC),
                         lambda s, c, e, i, n: (layer, e[s], 0, n_c + c_eff(s, c, n))),
            pl.BlockSpec((None, None, 1, MOE_FC), lambda s, c, e, i, n: (layer, e[s], 0, c_eff(s, c, n))),
            pl.BlockSpec((None, None, 1, MOE_FC), lambda s, c, e, i, n: (layer, e[s], 0, n_c + c_eff(s, c, n))),
        ],
        out_specs=pl.BlockSpec((MOE_R, MOE_FC), lambda s, c, e, i, n: (i[s], c_eff(s, c, n))),
        scratch_shapes=[pltpu.VMEM((2, MOE_R, HALF_D), U32), pltpu.VMEM((MOE_R, D_MODEL), BF16),
                        pltpu.VMEM((D_MODEL, MOE_FC), BF16), pltpu.VMEM((D_MODEL, MOE_FC), BF16),
                        pltpu.SemaphoreType.DMA((2,))],
    )
    bgu = b_gate_up.reshape(DEPTH, N_EXPERTS, 1, 2 * MOE_D_FF)
    tok3 = row_tok.reshape(MOE_NSB, 1, MOE_R)
    return pl.pallas_call(
        _moe_up_kernel,
        name="moe_up",
        grid_spec=grid_spec,
        out_shape=jax.ShapeDtypeStruct((MOE_CAP, MOE_D_FF), BF16),
        compiler_params=_params(("arbitrary", "arbitrary"), 56),
    )(sbe, sbi, nsub, tok3, tok3, h_packed, w_gate_up, w_gate_up, bgu, bgu)


def _moe_down_kernel(sbe_ref, sbi_ref, nsub_ref, h_ref, w_ref, b_ref, o_ref, wb):
    s = pl.program_id(0)
    nsub = nsub_ref[s]

    @pl.when(nsub > 0)
    def _():
        wb[...] = w_ref[...].astype(BF16)

        def compute(r0, n):
            rows = pl.ds(r0, n)
            o_ref[rows, :] = jnp.dot(h_ref[rows, :], wb[...], preferred_element_type=F32) + b_ref[...]

        _for_valid_rows(nsub, compute)


def _moe_down(h_sorted, sbe, sbi, nsub, w_down, b_down, layer):
    n_c = D_MODEL // MOE_NC
    last_c = n_c - 1

    def c_eff(s, c, nsub_ref):
        return jnp.where(nsub_ref[s] > 0, c, last_c)

    grid_spec = pltpu.PrefetchScalarGridSpec(
        num_scalar_prefetch=3,
        grid=(MOE_NSB, n_c),
        in_specs=[
            pl.BlockSpec((MOE_R, MOE_D_FF), lambda s, c, e, i, n: (i[s], 0)),
            pl.BlockSpec((None, None, MOE_D_FF, MOE_NC), lambda s, c, e, i, n: (layer, e[s], 0, c_eff(s, c, n))),
            pl.BlockSpec((None, None, 1, MOE_NC), lambda s, c, e, i, n: (layer, e[s], 0, c_eff(s, c, n))),
        ],
        out_specs=pl.BlockSpec((MOE_R, MOE_NC), lambda s, c, e, i, n: (i[s], c_eff(s, c, n))),
        scratch_shapes=[pltpu.VMEM((MOE_D_FF, MOE_NC), BF16)],
    )
    return pl.pallas_call(
        _moe_down_kernel,
        name="moe_down",
        grid_spec=grid_spec,
        out_shape=jax.ShapeDtypeStruct((MOE_CAP, D_MODEL), F32),
        compiler_params=_params(("arbitrary", "arbitrary"), 48),
    )(sbe, sbi, nsub, h_sorted, w_down, b_down.reshape(DEPTH, N_EXPERTS, 1, D_MODEL))


def _route(top_idx):
    flat_e = top_idx.reshape(-1)
    onehot = (flat_e[:, None] == jnp.arange(N_EXPERTS, dtype=I32)[None, :]).astype(I32)
    csum = jnp.cumsum(onehot, axis=0)
    rank = jnp.take_along_axis(csum, flat_e[:, None], axis=1)[:, 0] - 1
    counts = csum[-1]
    nsb = (counts + MOE_R - 1) // MOE_R
    sb_end = jnp.cumsum(nsb)
    sb_off = sb_end - nsb
    dest = sb_off[flat_e] * MOE_R + rank
    n_real = sb_end[-1]
    s = jnp.arange(MOE_NSB, dtype=I32)
    s_eff = jnp.minimum(s, n_real - 1)
    sbe = jnp.minimum(jnp.searchsorted(sb_end, s_eff, side='right'), N_EXPERTS - 1).astype(I32)
    valid = jnp.clip(counts[sbe] - (s_eff - sb_off[sbe]) * MOE_R, 0, MOE_R)
    valid = jnp.where(s < n_real, valid, 0)
    nsub = ((valid + MOE_SUB - 1) // MOE_SUB).astype(I32)
    dest = dest.astype(I32)
    row_tok = jnp.zeros((MOE_CAP,), I32).at[dest].set(jnp.arange(N_ASSIGN, dtype=I32) // TOP_K)
    return dest, row_tok, sbe, s_eff.astype(I32), nsub


def _moe_experts(h_packed, top_idx, w_gate_up, b_gate_up, w_down, b_down, layer):
    dest, row_tok, sbe, sbi, nsub = _route(top_idx)
    h_sorted = _moe_up(h_packed, row_tok, sbe, sbi, nsub, w_gate_up, b_gate_up, layer)
    return _moe_down(h_sorted, sbe, sbi, nsub, w_down, b_down, layer), dest


def _pb_rows(mod_piece):
    idx = jnp.array([0] * N_PROMPT_PB + list(range(1, 1 + DEC_BATCH)), I32)
    return mod_piece[idx][:, None, :]


def kernel(x_prompt, x_sample, cache_a_k, cache_a_v, cache_b_k, cache_b_v, cache_c_k, cache_c_v, c, c_ctx, w_ada, b_ada, g_pre_mix, g_post_mix, g_pre_ffn, g_post_ffn, w_in_even, w_out_even, rpb_a, q_norm_b, k_norm_b, w_in_odd, w_out_odd, sink_c, w_router, b_router, w_gate_up, b_gate_up, w_down, b_down):
    x = jnp.concatenate([x_prompt.reshape(N_PROMPT, D_MODEL), x_sample.reshape(N_SAMPLE, D_MODEL)], axis=0)
    cond8 = jnp.concatenate([c_ctx[None, :], c, jnp.zeros((8 - 1 - DEC_BATCH, D_MODEL), F32)], axis=0)
    mods = _ada_mod(cond8, w_ada, b_ada)
    cos_tab, sin_tab = _rope_tables()
    no_sink = jnp.full((max(A_HEADS, B_HEADS),), -jnp.inf, F32)
    states = {}

    for l in range(DEPTH):
        j = l // 2
        m = [_pb_rows(mods[l, :, k * D_MODEL:(k + 1) * D_MODEL]) for k in range(6)]
        if l % 2 == 0:
            w_in = w_in_even[j].astype(BF16)
            w_out = w_out_even[j].astype(BF16)
            nb = EVEN_IN // IN_BN
            qb0 = 3 * A_HEADS * HEAD_DIM // IN_BN
            kb0 = qb0 + B_HEADS * HEAD_DIM // IN_BN
            vb0 = kb0 + B_KV_HEADS * HEAD_DIM // IN_BN
            modes = jnp.array([MODE_PLAIN] * qb0 + [MODE_NORM_Q] * (kb0 - qb0) + [MODE_NORM_K] * (vb0 - kb0)
                              + [MODE_PLAIN] * (nb - vb0), I32)
            gains = jnp.concatenate([q_norm_b[j][None], k_norm_b[j][None], jnp.ones((6, HEAD_DIM), F32)], axis=0)
        else:
            w_in = w_in_odd[j].astype(BF16)
            w_out = w_out_odd[j].astype(BF16)
            nb = ODD_IN // IN_BN
            vc0 = (C_HEADS + C_KV_HEADS) * HEAD_DIM // IN_BN
            modes = jnp.array([MODE_ROPE] * vc0 + [MODE_PLAIN] * (nb - vc0), I32)
            gains = jnp.ones((8, HEAD_DIM), F32)

        proj = _in_proj(x, g_pre_mix[l], m[0], m[1], w_in, modes, gains, cos_tab, sin_tab)

        if l % 2 == 0:
            ah = A_HEADS * HEAD_DIM
            ctx_a = _ctx_attention(proj, no_sink, 0, ah, 2 * ah, A_HEADS, A_HEADS, 8)
            ctx_b = _ctx_attention(proj, no_sink, 3 * ah, 3 * ah + B_HEADS * HEAD_DIM,
                                   3 * ah + (B_HEADS + B_KV_HEADS) * HEAD_DIM, B_HEADS, B_KV_HEADS, B_KV_HEADS)
            lat_a = _na_attention(proj, cache_a_k, cache_a_v, rpb_a[j], j)
            lat_b = _dense_attention(proj, cache_b_k, cache_b_v, j)
            prompt_parts, latent_parts = [ctx_a, ctx_b], [lat_a, lat_b]
            pp = proj[:N_PROMPT]
            kb_c = 3 * ah + B_HEADS * HEAD_DIM
            vb_c = kb_c + B_KV_HEADS * HEAD_DIM
            states.setdefault('ak', []).append(pp[:, ah:2 * ah].reshape(BATCH, SEQ, A_HEADS, HEAD_DIM))
            states.setdefault('av', []).append(pp[:, 2 * ah:3 * ah].reshape(BATCH, SEQ, A_HEADS, HEAD_DIM))
            states.setdefault('bk', []).append(pp[:, kb_c:vb_c].reshape(BATCH, SEQ, B_KV_HEADS, HEAD_DIM))
            states.setdefault('bv', []).append(pp[:, vb_c:].reshape(BATCH, SEQ, B_KV_HEADS, HEAD_DIM))
        else:
            qc = C_HEADS * HEAD_DIM
            kvw = C_KV_HEADS * HEAD_DIM
            ctx_c = _ctx_attention(proj, sink_c[j], 0, qc, qc + kvw, C_HEADS, C_KV_HEADS, C_KV_HEADS)
            lat_c = _window_attention(proj, cache_c_k, cache_c_v, sink_c[j], j)
            prompt_parts, latent_parts = [ctx_c], [lat_c]
            pp = proj[:N_PROMPT]
            states.setdefault('ck', []).append(pp[:, qc:qc + kvw].reshape(BATCH, SEQ, C_KV_HEADS, HEAD_DIM))
            states.setdefault('cv', []).append(pp[:, qc + kvw:].reshape(BATCH, SEQ, C_KV_HEADS, HEAD_DIM))

        y = _out_proj(prompt_parts, latent_parts, w_out)
        wr = jnp.pad(w_router[l], ((0, 0), (0, LANES - N_EXPERTS)))
        wr_hi = wr.astype(BF16)
        wr_lo = (wr - wr_hi.astype(F32)).astype(BF16)
        br = jnp.concatenate([b_router[l], jnp.full((LANES - N_EXPERTS,), NEG_BIG, F32)])[None, :]
        x, h_ffn, top_i, top_g = _post_mix(y, x, m[2], g_post_mix[l], g_pre_ffn[l], m[3], m[4], wr_hi, wr_lo, br)
        expert_out, dest = _moe_experts(h_ffn, top_i[:, :TOP_K], w_gate_up, b_gate_up, w_down, b_down, l)
        x = _combine_post_ffn(expert_out, dest, top_g, x, m[5], g_post_ffn[l], split_out=(l == DEPTH - 1))

    y_prompt = x[0].reshape(BATCH, SEQ, D_MODEL)
    y_sample = x[1].reshape(DEC_BATCH, DEC_SEQ, D_MODEL)
    st = {k: jnp.stack(v, axis=1) for k, v in states.items()}
    return (y_prompt, y_sample, st['ak'], st['av'], st['bk'], st['bv'], st['ck'], st['cv'])
```

```python
import functools

import jax
import jax.numpy as jnp
import numpy as np
from jax import lax
from jax.experimental import pallas as pl
from jax.experimental.pallas import tpu as pltpu

F32 = jnp.float32
BF16 = jnp.bfloat16
I32 = jnp.int32
U32 = jnp.uint32

D_MODEL = 4096
BATCH = 16
SEQ = 256
DEPTH = 2
DEC_BATCH = 4
DEC_SEQ = 2048
PAST_LEN = 512
GRID_W = 64
GRID_ROWS = DEC_SEQ // GRID_W
HEAD_DIM = 128
A_HEADS = 16
B_HEADS = 16
B_KV_HEADS = 4
C_HEADS = 32
C_KV_HEADS = 4
WIN_H = 8
WIN_W = 16
C_WINDOW = 128
N_EXPERTS = 32
TOP_K = 4
MOE_D_FF = 2048
SWIGLU_LIMIT = 7.0
SWIGLU_ALPHA = 1.702
ROPE_THETA = 10000.0
RMS_EPS = 1e-6
ATTN_SCALE = HEAD_DIM ** -0.5
EVEN_IN = (3 * A_HEADS + B_HEADS + 2 * B_KV_HEADS) * HEAD_DIM
ODD_IN = (C_HEADS + 2 * C_KV_HEADS) * HEAD_DIM

N_PROMPT = BATCH * SEQ
N_SAMPLE = DEC_BATCH * DEC_SEQ
N_TOK = N_PROMPT + N_SAMPLE
PB_TOK = DEC_SEQ
N_PB = N_TOK // PB_TOK
N_PROMPT_PB = N_PROMPT // PB_TOK

LANES = 128
MIB = 1024 * 1024

MOE_R = 1024
MOE_SUB = 256
N_ASSIGN = N_TOK * TOP_K
MOE_NSB = N_ASSIGN // MOE_R + N_EXPERTS
MOE_CAP = MOE_NSB * MOE_R
MOE_FC = 256
MOE_NC = 512

NEG_BIG = -1e30
HALF_D = D_MODEL // 2
HI_MASK = np.uint32(0xFFFF0000)
SHIFT16 = np.uint32(16)


def _params(sem, vmem_mib):
    return pltpu.CompilerParams(dimension_semantics=sem, vmem_limit_bytes=vmem_mib * MIB)


def _ada_kernel(c_ref, w_ref, b_ref, o_ref):
    c = c_ref[...]
    s = (c / (1.0 + jnp.exp(-c))).astype(BF16)
    o_ref[...] = jnp.dot(s, w_ref[...].astype(BF16), preferred_element_type=F32) + b_ref[...]


def _ada_mod(cond8, w_ada, b_ada):
    bn = 512
    n_out = 6 * D_MODEL
    return pl.pallas_call(
        _ada_kernel,
        name="ada_mod",
        grid=(DEPTH, n_out // bn),
        in_specs=[
            pl.BlockSpec((8, D_MODEL), lambda l, j: (0, 0)),
            pl.BlockSpec((None, D_MODEL, bn), lambda l, j: (l, 0, j)),
            pl.BlockSpec((None, 1, bn), lambda l, j: (l, 0, j)),
        ],
        out_specs=pl.BlockSpec((None, 8, bn), lambda l, j: (l, 0, j)),
        out_shape=jax.ShapeDtypeStruct((DEPTH, 8, n_out), F32),
        compiler_params=_params(("arbitrary", "arbitrary"), 40),
    )(cond8, w_ada, b_ada.reshape(DEPTH, 1, n_out))


IN_BM = 512
IN_BN = 512
MODE_PLAIN, MODE_NORM_Q, MODE_NORM_K, MODE_ROPE = 0, 1, 2, 3


def _inproj_kernel(mode_ref, x_ref, g_ref, sh_ref, sc_ref, w_ref, gain_ref, cos_ref, sin_ref,
                   o_ref, h_ref):
    j = pl.program_id(1)

    @pl.when(j == 0)
    def _():
        x = x_ref[...]
        r = lax.rsqrt(jnp.mean(x * x, axis=-1, keepdims=True) + RMS_EPS)
        h = (x * r * g_ref[...]) * (1.0 + sc_ref[...]) + sh_ref[...]
        h_ref[...] = h.astype(BF16)

    acc = jnp.dot(h_ref[...], w_ref[...], preferred_element_type=F32)
    mode = mode_ref[j]

    @pl.when(mode == MODE_PLAIN)
    def _():
        o_ref[...] = acc

    def rope_store(hh, y):
        lane = lax.broadcasted_iota(I32, (IN_BM, LANES), 1)
        sw = jnp.where((lane % 64) < 32, pltpu.roll(y, 96, 1), pltpu.roll(y, 32, 1))
        o_ref[:, hh * HEAD_DIM:(hh + 1) * HEAD_DIM] = y * cos_ref[...] + sw * sin_ref[...]

    @pl.when(mode == MODE_ROPE)
    def _():
        for hh in range(IN_BN // HEAD_DIM):
            rope_store(hh, acc[:, hh * HEAD_DIM:(hh + 1) * HEAD_DIM])

    @pl.when((mode == MODE_NORM_Q) | (mode == MODE_NORM_K))
    def _():
        gain = jnp.where(mode == MODE_NORM_Q, gain_ref[0:1, :], gain_ref[1:2, :])
        for hh in range(IN_BN // HEAD_DIM):
            xh = acc[:, hh * HEAD_DIM:(hh + 1) * HEAD_DIM]
            rs = lax.rsqrt(jnp.mean(xh * xh, axis=-1, keepdims=True) + RMS_EPS)
            rope_store(hh, xh * rs * gain)


def _in_proj(x, g_pre, shift, scale, w_bf16, modes, gains, cos_tab, sin_tab):
    n = w_bf16.shape[1]
    n_i = N_TOK // IN_BM
    n_prompt_blocks = N_PROMPT // IN_BM
    blocks_per_seq = DEC_SEQ // IN_BM

    def tab_idx(i, j, m):
        return (jnp.where(i < n_prompt_blocks, 0, 1 + (i - n_prompt_blocks) % blocks_per_seq), 0)

    grid_spec = pltpu.PrefetchScalarGridSpec(
        num_scalar_prefetch=1,
        grid=(n_i, n // IN_BN),
        in_specs=[
            pl.BlockSpec((IN_BM, D_MODEL), lambda i, j, m: (i, 0)),
            pl.BlockSpec((1, D_MODEL), lambda i, j, m: (0, 0)),
            pl.BlockSpec((None, 1, D_MODEL), lambda i, j, m: (i * IN_BM // PB_TOK, 0, 0)),
            pl.BlockSpec((None, 1, D_MODEL), lambda i, j, m: (i * IN_BM // PB_TOK, 0, 0)),
            pl.BlockSpec((D_MODEL, IN_BN), lambda i, j, m: (0, j)),
            pl.BlockSpec((8, LANES), lambda i, j, m: (0, 0)),
            pl.BlockSpec((IN_BM, LANES), tab_idx),
            pl.BlockSpec((IN_BM, LANES), tab_idx),
        ],
        out_specs=pl.BlockSpec((IN_BM, IN_BN), lambda i, j, m: (i, j)),
        scratch_shapes=[pltpu.VMEM((IN_BM, D_MODEL), BF16)],
    )
    return pl.pallas_call(
        _inproj_kernel,
        name="in_proj",
        grid_spec=grid_spec,
        out_shape=jax.ShapeDtypeStruct((N_TOK, n), F32),
        compiler_params=_params(("arbitrary", "arbitrary"), 48),
    )(modes, x, g_pre.reshape(1, D_MODEL), shift, scale, w_bf16, gains, cos_tab, sin_tab)


def _rope_tables():
    t = jnp.arange(DEC_SEQ)
    row = (t // GRID_W).astype(F32)
    col = (t % GRID_W).astype(F32)
    n_freq = HEAD_DIM // 4
    inv = ROPE_THETA ** (-jnp.arange(n_freq, dtype=F32) / n_freq)
    ar = row[:, None] * inv
    ac = col[:, None] * inv
    cos = jnp.concatenate([jnp.cos(ar), jnp.cos(ar), jnp.cos(ac), jnp.cos(ac)], axis=-1)
    sin = jnp.concatenate([-jnp.sin(ar), jnp.sin(ar), -jnp.sin(ac), jnp.sin(ac)], axis=-1)
    cos = jnp.concatenate([jnp.ones((IN_BM, HEAD_DIM), F32), cos], axis=0)
    sin = jnp.concatenate([jnp.zeros((IN_BM, HEAD_DIM), F32), sin], axis=0)
    return cos, sin


MM_BM = 1024
MM_BN = 512


MM_PROMPT_BLOCKS = N_PROMPT // MM_BM


def _out_proj_kernel(*refs, n_parts):
    prompt_refs = refs[:n_parts]
    latent_refs = refs[n_parts:2 * n_parts]
    w_ref, o_ref = refs[2 * n_parts:]
    i = pl.program_id(0)

    def run(parts):
        acc = None
        k0 = 0
        for r in parts:
            kk = r.shape[1]
            d = jnp.dot(r[...], w_ref[k0:k0 + kk, :], preferred_element_type=F32)
            acc = d if acc is None else acc + d
            k0 += kk
        o_ref[...] = acc

    @pl.when(i < MM_PROMPT_BLOCKS)
    def _():
        run(prompt_refs)

    @pl.when(i >= MM_PROMPT_BLOCKS)
    def _():
        run(latent_refs)


def _out_proj(prompt_parts, latent_parts, w_bf16):
    k, n = w_bf16.shape
    n_parts = len(prompt_parts)
    last_prompt = MM_PROMPT_BLOCKS - 1
    in_specs = ([pl.BlockSpec((MM_BM, p.shape[1]), lambda i, j: (jnp.minimum(i, last_prompt), 0))
                 for p in prompt_parts]
                + [pl.BlockSpec((MM_BM, p.shape[1]), lambda i, j: (jnp.maximum(i - MM_PROMPT_BLOCKS, 0), 0))
                   for p in latent_parts]
                + [pl.BlockSpec((k, MM_BN), lambda i, j: (0, j))])
    return pl.pallas_call(
        functools.partial(_out_proj_kernel, n_parts=n_parts),
        name="out_proj",
        grid=(N_TOK // MM_BM, n // MM_BN),
        in_specs=in_specs,
        out_specs=pl.BlockSpec((MM_BM, MM_BN), lambda i, j: (i, j)),
        out_shape=jax.ShapeDtypeStruct((N_TOK, n), F32),
        compiler_params=_params(("arbitrary", "arbitrary"), 56),
    )(*prompt_parts, *latent_parts, w_bf16)


def _softmax_pv(s_list, v_list, sink=None):
    m = jnp.max(s_list[0], axis=-1, keepdims=True)
    for s in s_list[1:]:
        m = jnp.maximum(m, jnp.max(s, axis=-1, keepdims=True))
    if sink is not None:
        m = jnp.maximum(m, sink)
    den = None
    out = None
    for s, v in zip(s_list, v_list):
        p = jnp.exp(s - m)
        d = jnp.sum(p, axis=-1, keepdims=True)
        o = jnp.dot(p.astype(BF16), v, preferred_element_type=F32)
        den = d if den is None else den + d
        out = o if out is None else out + o
    if sink is not None:
        den = den + jnp.exp(sink - m)
    return out / den


def _qk(q, k):
    return lax.dot_general(q, k, (((1,), (1,)), ((), ())), preferred_element_type=F32) * ATTN_SCALE


def _ctx_attn_kernel(sink_ref, q_ref, k_ref, v_ref, o_ref, *, nkv, g):
    kb = pl.program_id(1)
    for kv in range(nkv):
        k = k_ref[:, kv * HEAD_DIM:(kv + 1) * HEAD_DIM].astype(BF16)
        v = v_ref[:, kv * HEAD_DIM:(kv + 1) * HEAD_DIM].astype(BF16)
        for gi in range(g):
            c0 = (kv * g + gi) * HEAD_DIM
            q = q_ref[:, c0:c0 + HEAD_DIM].astype(BF16)
            sink = sink_ref[(kb * nkv + kv) * g + gi]
            o = _softmax_pv([_qk(q, k)], [v], sink)
            o_ref[:, c0:c0 + HEAD_DIM] = o.astype(o_ref.dtype)


def _ctx_attention(proj, sink, q_col, k_col, v_col, hq, hkv, nkv):
    g = hq // hkv
    qw = nkv * g * HEAD_DIM
    kw = nkv * HEAD_DIM
    return pl.pallas_call(
        functools.partial(_ctx_attn_kernel, nkv=nkv, g=g),
        name="ctx_attn",
        grid=(BATCH, hkv // nkv),
        in_specs=[
            pl.BlockSpec(memory_space=pltpu.SMEM),
            pl.BlockSpec((SEQ, qw), lambda b, h: (b, q_col // qw + h)),
            pl.BlockSpec((SEQ, kw), lambda b, h: (b, k_col // kw + h)),
            pl.BlockSpec((SEQ, kw), lambda b, h: (b, v_col // kw + h)),
        ],
        out_specs=pl.BlockSpec((SEQ, qw), lambda b, h: (b, h)),
        out_shape=jax.ShapeDtypeStruct((N_PROMPT, hq * HEAD_DIM), BF16),
        compiler_params=_params(("arbitrary", "arbitrary"), 40),
    )(sink, proj, proj, proj)


DENSE_BQ = 256


def _cache_bf16(step, srcs, dsts):
    @pl.when(step == 0)
    def _():
        for src, dst in zip(srcs, dsts):
            dst[...] = src[...].astype(BF16)


def _kv_scratch(width):
    return [pltpu.VMEM((DEC_SEQ, width), BF16), pltpu.VMEM((DEC_SEQ, width), BF16),
            pltpu.VMEM((PAST_LEN, width), BF16), pltpu.VMEM((PAST_LEN, width), BF16)]


def _dense_attn_kernel(q_ref, k_ref, v_ref, kc_ref, vc_ref, o_ref, kb, vb, kcb, vcb, *, g):
    _cache_bf16(pl.program_id(2), (k_ref, v_ref, kc_ref, vc_ref), (kb, vb, kcb, vcb))
    k = kb[...]
    v = vb[...]
    kc = kcb[...]
    vc = vcb[...]
    for gi in range(g):
        c0 = gi * HEAD_DIM
        q = q_ref[:, c0:c0 + HEAD_DIM].astype(BF16)
        o = _softmax_pv([_qk(q, k), _qk(q, kc)], [v, vc])
        o_ref[:, c0:c0 + HEAD_DIM] = o.astype(o_ref.dtype)


def _dense_attention(proj, cache_k, cache_v, layer_j):
    g = B_HEADS // B_KV_HEADS
    qw = g * HEAD_DIM
    nq = DEC_SEQ // DENSE_BQ
    q_col = 3 * A_HEADS * HEAD_DIM
    k_col = q_col + B_HEADS * HEAD_DIM
    v_col = k_col + B_KV_HEADS * HEAD_DIM
    row0 = N_PROMPT // DENSE_BQ
    ck = cache_k.reshape(DEC_BATCH, -1, PAST_LEN, B_KV_HEADS * HEAD_DIM)
    cv = cache_v.reshape(DEC_BATCH, -1, PAST_LEN, B_KV_HEADS * HEAD_DIM)
    return pl.pallas_call(
        functools.partial(_dense_attn_kernel, g=g),
        name="dense_attn",
        grid=(DEC_BATCH, B_KV_HEADS, nq),
        in_specs=[
            pl.BlockSpec((DENSE_BQ, qw), lambda b, h, i: (row0 + b * nq + i, q_col // qw + h)),
            pl.BlockSpec((DEC_SEQ, HEAD_DIM), lambda b, h, i: (N_PROMPT_PB + b, k_col // HEAD_DIM + h)),
            pl.BlockSpec((DEC_SEQ, HEAD_DIM), lambda b, h, i: (N_PROMPT_PB + b, v_col // HEAD_DIM + h)),
            pl.BlockSpec((None, None, PAST_LEN, HEAD_DIM), lambda b, h, i: (b, layer_j, 0, h)),
            pl.BlockSpec((None, None, PAST_LEN, HEAD_DIM), lambda b, h, i: (b, layer_j, 0, h)),
        ],
        out_specs=pl.BlockSpec((DENSE_BQ, qw), lambda b, h, i: (b * nq + i, h)),
        scratch_shapes=_kv_scratch(HEAD_DIM),
        out_shape=jax.ShapeDtypeStruct((N_SAMPLE, B_HEADS * HEAD_DIM), BF16),
        compiler_params=_params(("arbitrary", "arbitrary", "arbitrary"), 48),
    )(proj, proj, proj, ck, cv)


WIN_BQ = 256
WIN_KEYS = WIN_BQ + 2 * C_WINDOW


def _window_attn_kernel(sink_ref, q_ref, k_ref, v_ref, kc_ref, vc_ref, o_ref, kb, vb, kcb, vcb, *, g):
    h = pl.program_id(1)
    i = pl.program_id(2)
    _cache_bf16(i, (k_ref, v_ref, kc_ref, vc_ref), (kb, vb, kcb, vcb))
    q0 = i * WIN_BQ
    k0 = pl.multiple_of(jnp.clip(q0 - C_WINDOW, 0, DEC_SEQ - WIN_KEYS), C_WINDOW)
    k = kb[pl.ds(k0, WIN_KEYS), :]
    v = vb[pl.ds(k0, WIN_KEYS), :]
    kc = kcb[...]
    vc = vcb[...]
    q_pos = q0 + lax.broadcasted_iota(I32, (WIN_BQ, WIN_KEYS), 0)
    k_pos = k0 + lax.broadcasted_iota(I32, (WIN_BQ, WIN_KEYS), 1)
    ok = jnp.abs(q_pos - k_pos) <= C_WINDOW
    for gi in range(g):
        c0 = gi * HEAD_DIM
        q = q_ref[:, c0:c0 + HEAD_DIM].astype(BF16)
        s_lat = jnp.where(ok, _qk(q, k), -jnp.inf)
        o = _softmax_pv([s_lat, _qk(q, kc)], [v, vc], sink_ref[h * g + gi])
        o_ref[:, c0:c0 + HEAD_DIM] = o.astype(o_ref.dtype)


def _window_attention(proj, cache_k, cache_v, sink, layer_j):
    g = C_HEADS // C_KV_HEADS
    qw = g * HEAD_DIM
    nq = DEC_SEQ // WIN_BQ
    k_col = C_HEADS * HEAD_DIM
    v_col = k_col + C_KV_HEADS * HEAD_DIM
    row0 = N_PROMPT // WIN_BQ
    ck = cache_k.reshape(DEC_BATCH, -1, PAST_LEN, C_KV_HEADS * HEAD_DIM)
    cv = cache_v.reshape(DEC_BATCH, -1, PAST_LEN, C_KV_HEADS * HEAD_DIM)
    return pl.pallas_call(
        functools.partial(_window_attn_kernel, g=g),
        name="window_attn",
        grid=(DEC_BATCH, C_KV_HEADS, nq),
        in_specs=[
            pl.BlockSpec(memory_space=pltpu.SMEM),
            pl.BlockSpec((WIN_BQ, qw), lambda b, h, i: (row0 + b * nq + i, h)),
            pl.BlockSpec((DEC_SEQ, HEAD_DIM), lambda b, h, i: (N_PROMPT_PB + b, k_col // HEAD_DIM + h)),
            pl.BlockSpec((DEC_SEQ, HEAD_DIM), lambda b, h, i: (N_PROMPT_PB + b, v_col // HEAD_DIM + h)),
            pl.BlockSpec((None, None, PAST_LEN, HEAD_DIM), lambda b, h, i: (b, layer_j, 0, h)),
            pl.BlockSpec((None, None, PAST_LEN, HEAD_DIM), lambda b, h, i: (b, layer_j, 0, h)),
        ],
        out_specs=pl.BlockSpec((WIN_BQ, qw), lambda b, h, i: (b * nq + i, h)),
        scratch_shapes=_kv_scratch(HEAD_DIM),
        out_shape=jax.ShapeDtypeStruct((N_SAMPLE, C_HEADS * HEAD_DIM), BF16),
        compiler_params=_params(("arbitrary", "arbitrary", "arbitrary"), 48),
    )(sink, proj, proj, proj, ck, cv)


NA_HC = 4
NA_KEYS = WIN_H * GRID_W


def _na_row_start(i):
    return jnp.clip(i - WIN_H // 2, 0, GRID_ROWS - WIN_H)


def _na_attn_kernel(q_ref, k_ref, v_ref, kc_ref, vc_ref, bias_ref, o_ref, kb, vb, kcb, vcb):
    i = pl.program_id(2)
    _cache_bf16(i, (k_ref, v_ref, kc_ref, vc_ref), (kb, vb, kcb, vcb))
    r0 = pl.multiple_of(_na_row_start(i) * GRID_W, GRID_W)
    for hh in range(NA_HC):
        cs = slice(hh * HEAD_DIM, (hh + 1) * HEAD_DIM)
        q = q_ref[:, cs].astype(BF16)
        s_lat = _qk(q, kb[pl.ds(r0, NA_KEYS), cs]) + bias_ref[hh]
        o = _softmax_pv([s_lat, _qk(q, kcb[:, cs])], [vb[pl.ds(r0, NA_KEYS), cs], vcb[:, cs]])
        o_ref[:, cs] = o.astype(o_ref.dtype)


def _na_bias_table(rpb):
    qc = jnp.arange(GRID_W)[:, None]
    kc = jnp.arange(GRID_W)[None, :]
    ws = jnp.clip(qc - WIN_W // 2, 0, GRID_W - WIN_W)
    ok = (kc >= ws) & (kc < ws + WIN_W)
    dc = jnp.clip(kc - qc + WIN_W - 1, 0, 2 * WIN_W - 2)
    t = rpb[:, :, dc]
    t = jnp.where(ok[None, None], t, -jnp.inf)
    dr = jnp.arange(WIN_H)[:, None] + jnp.arange(WIN_H)[None, :]
    b = t[:, dr]
    b = b.transpose(0, 1, 3, 2, 4).reshape(A_HEADS, WIN_H, GRID_W, NA_KEYS)
    return b.astype(F32)


def _na_attention(proj, cache_k, cache_v, rpb, layer_j):
    cw = NA_HC * HEAD_DIM
    k_col = A_HEADS * HEAD_DIM
    v_col = 2 * A_HEADS * HEAD_DIM
    row0 = N_PROMPT // GRID_W
    ck = cache_k.reshape(DEC_BATCH, -1, PAST_LEN, A_HEADS * HEAD_DIM)
    cv = cache_v.reshape(DEC_BATCH, -1, PAST_LEN, A_HEADS * HEAD_DIM)
    bias = _na_bias_table(rpb)

    def bias_idx(b, h, i):
        return (h, _na_row_start(i) - i + WIN_H - 1, 0, 0)

    return pl.pallas_call(
        _na_attn_kernel,
        name="na_attn",
        grid=(DEC_BATCH, A_HEADS // NA_HC, GRID_ROWS),
        in_specs=[
            pl.BlockSpec((GRID_W, cw), lambda b, h, i: (row0 + b * GRID_ROWS + i, h)),
            pl.BlockSpec((DEC_SEQ, cw), lambda b, h, i: (N_PROMPT_PB + b, k_col // cw + h)),
            pl.BlockSpec((DEC_SEQ, cw), lambda b, h, i: (N_PROMPT_PB + b, v_col // cw + h)),
            pl.BlockSpec((None, None, PAST_LEN, cw), lambda b, h, i: (b, layer_j, 0, h)),
            pl.BlockSpec((None, None, PAST_LEN, cw), lambda b, h, i: (b, layer_j, 0, h)),
            pl.BlockSpec((NA_HC, None, GRID_W, NA_KEYS), bias_idx),
        ],
        out_specs=pl.BlockSpec((GRID_W, cw), lambda b, h, i: (b * GRID_ROWS + i, h)),
        scratch_shapes=_kv_scratch(cw),
        out_shape=jax.ShapeDtypeStruct((N_SAMPLE, A_HEADS * HEAD_DIM), BF16),
        compiler_params=_params(("arbitrary", "arbitrary", "arbitrary"), 48),
    )(proj, proj, proj, ck, cv, bias)


EP_BM = 256


def _rms(x):
    return x * lax.rsqrt(jnp.mean(x * x, axis=-1, keepdims=True) + RMS_EPS)


def _post_mix_kernel(y_ref, x_ref, gate_ref, gpost_ref, gpre_ref, sh_ref, sc_ref,
                     wr_hi_ref, wr_lo_ref, br_ref, xo_ref, h_ref, ti_ref, tg_ref):
    x_new = x_ref[...] + gate_ref[...] * (_rms(y_ref[...]) * gpost_ref[...])
    xo_ref[...] = x_new
    h = (_rms(x_new) * gpre_ref[...]) * (1.0 + sc_ref[...]) + sh_ref[...]
    h_hi = h.astype(BF16)
    h_hi32 = h_hi.astype(F32)
    bits = pltpu.bitcast(h_hi32, U32)
    h_ref[...] = (bits[:, :HALF_D] >> SHIFT16) | (bits[:, HALF_D:] & HI_MASK)
    h_lo = (h - h_hi32).astype(BF16)
    w_hi = wr_hi_ref[...]
    logits = (jnp.dot(h_hi, w_hi, preferred_element_type=F32)
              + jnp.dot(h_lo, w_hi, preferred_element_type=F32)
              + jnp.dot(h_hi, wr_lo_ref[...], preferred_element_type=F32)) + br_ref[...]
    lane = lax.broadcasted_iota(I32, (EP_BM, LANES), 1).astype(F32)
    vals, idxs = [], []
    l = logits
    for _ in range(TOP_K):
        m = jnp.max(l, axis=-1, keepdims=True)
        idx = jnp.min(jnp.where(l == m, lane, float(LANES)), axis=-1, keepdims=True)
        vals.append(m)
        idxs.append(idx)
        l = jnp.where(lane == idx, -jnp.inf, l)
    es = [jnp.exp(v - vals[0]) for v in vals]
    den = es[0] + es[1] + es[2] + es[3]
    ti = jnp.zeros((EP_BM, LANES), F32)
    tg = jnp.zeros((EP_BM, LANES), F32)
    for k in range(TOP_K):
        ti = jnp.where(lane == float(k), idxs[k], ti)
        tg = jnp.where(lane == float(k), es[k] / den, tg)
    ti_ref[...] = ti.astype(I32)
    tg_ref[...] = tg


def _post_mix(y, x, gate, g_post, g_pre, shift, scale, wr_hi, wr_lo, br):
    row = pl.BlockSpec((EP_BM, D_MODEL), lambda i: (i, 0))
    vec = pl.BlockSpec((1, D_MODEL), lambda i: (0, 0))
    mod = pl.BlockSpec((None, 1, D_MODEL), lambda i: (i * EP_BM // PB_TOK, 0, 0))
    wr = pl.BlockSpec((D_MODEL, LANES), lambda i: (0, 0))
    small = pl.BlockSpec((EP_BM, LANES), lambda i: (i, 0))
    packed = pl.BlockSpec((EP_BM, HALF_D), lambda i: (i, 0))
    return pl.pallas_call(
        _post_mix_kernel,
        name="post_mix",
        grid=(N_TOK // EP_BM,),
        in_specs=[row, row, mod, vec, vec, mod, mod, wr, wr, pl.BlockSpec((1, LANES), lambda i: (0, 0))],
        out_specs=[row, packed, small, small],
        out_shape=[jax.ShapeDtypeStruct((N_TOK, D_MODEL), F32),
                   jax.ShapeDtypeStruct((N_TOK, HALF_D), U32),
                   jax.ShapeDtypeStruct((N_TOK, LANES), I32),
                   jax.ShapeDtypeStruct((N_TOK, LANES), F32)],
        compiler_params=_params(("arbitrary",), 48),
    )(y, x, gate, g_post.reshape(1, D_MODEL), g_pre.reshape(1, D_MODEL), shift, scale, wr_hi, wr_lo, br)


CB_TOK = 128
CB_ROWS = CB_TOK * TOP_K
CB_STEPS = N_TOK // CB_TOK


CB_PROMPT_STEPS = N_PROMPT // CB_TOK


def _combine_kernel(dest_ref, dest_next_ref, e_hbm, tg_ref, x_ref, gate_ref, gpost_ref, *rest, split_out):
    out_refs, (buf, sem) = rest[:-2], rest[-2:]
    i = pl.program_id(0)
    slot = i % 2

    def gather(idx_ref, dst_slot):
        def body(t, carry):
            for k in range(TOP_K):
                pltpu.make_async_copy(e_hbm.at[pl.ds(idx_ref[0, t * TOP_K + k], 1)],
                                      buf.at[dst_slot, pl.ds(k * CB_TOK + t, 1)], sem.at[dst_slot]).start()
            return carry
        lax.fori_loop(0, CB_TOK, body, 0, unroll=2)

    @pl.when(i == 0)
    def _():
        gather(dest_ref, 0)

    @pl.when(i + 1 < CB_STEPS)
    def _():
        gather(dest_next_ref, 1 - slot)

    pltpu.make_async_copy(e_hbm.at[pl.ds(0, CB_ROWS)], buf.at[slot], sem.at[slot]).wait()
    tg = tg_ref[...]
    y = buf[slot, 0:CB_TOK, :] * tg[:, 0:1]
    for k in range(1, TOP_K):
        y = y + buf[slot, k * CB_TOK:(k + 1) * CB_TOK, :] * tg[:, k:k + 1]
    x_new = x_ref[...] + gate_ref[...] * (_rms(y) * gpost_ref[...])
    if split_out:
        @pl.when(i < CB_PROMPT_STEPS)
        def _():
            out_refs[0][...] = x_new

        @pl.when(i >= CB_PROMPT_STEPS)
        def _():
            out_refs[1][...] = x_new
    else:
        out_refs[0][...] = x_new


def _combine_post_ffn(expert_out, dest, top_gate, x, gate, g_post, split_out):
    row = pl.BlockSpec((CB_TOK, D_MODEL), lambda i: (i, 0))
    dest3 = dest.reshape(CB_STEPS, 1, CB_ROWS)
    if split_out:
        out_specs = [pl.BlockSpec((CB_TOK, D_MODEL), lambda i: (jnp.minimum(i, CB_PROMPT_STEPS - 1), 0)),
                     pl.BlockSpec((CB_TOK, D_MODEL), lambda i: (jnp.maximum(i - CB_PROMPT_STEPS, 0), 0))]
        out_shape = [jax.ShapeDtypeStruct((N_PROMPT, D_MODEL), F32), jax.ShapeDtypeStruct((N_SAMPLE, D_MODEL), F32)]
    else:
        out_specs = row
        out_shape = jax.ShapeDtypeStruct((N_TOK, D_MODEL), F32)
    return pl.pallas_call(
        functools.partial(_combine_kernel, split_out=split_out),
        name="moe_combine",
        grid=(CB_STEPS,),
        in_specs=[
            pl.BlockSpec((None, 1, CB_ROWS), lambda i: (i, 0, 0), memory_space=pltpu.SMEM),
            pl.BlockSpec((None, 1, CB_ROWS), lambda i: (jnp.minimum(i + 1, CB_STEPS - 1), 0, 0),
                         memory_space=pltpu.SMEM),
            pl.BlockSpec(memory_space=pl.ANY),
            pl.BlockSpec((CB_TOK, LANES), lambda i: (i, 0)),
            row,
            pl.BlockSpec((None, 1, D_MODEL), lambda i: (i * CB_TOK // PB_TOK, 0, 0)),
            pl.BlockSpec((1, D_MODEL), lambda i: (0, 0)),
        ],
        out_specs=out_specs,
        out_shape=out_shape,
        scratch_shapes=[pltpu.VMEM((2, CB_ROWS, D_MODEL), F32), pltpu.SemaphoreType.DMA((2,))],
        compiler_params=_params(("arbitrary",), 40),
    )(dest3, dest3, expert_out, top_gate, x, gate, g_post.reshape(1, D_MODEL))


def _for_valid_rows(nsub, first, rest):
    pair = 2 * MOE_SUB

    @pl.when(nsub == 1)
    def _():
        first(0, MOE_SUB)

    @pl.when(nsub >= 2)
    def _():
        first(0, pair)

        def body(p, carry):
            rest(pl.multiple_of(p * pair, pair), pair)
            return carry

        lax.fori_loop(1, nsub // 2, body, 0)

        @pl.when(nsub % 2 == 1)
        def _():
            rest(pl.multiple_of((nsub - 1) * MOE_SUB, MOE_SUB), MOE_SUB)


GATHER_UNROLL = 8


def _moe_up_kernel(sbe_ref, sbi_ref, nsub_ref, tok_ref, tok_next_ref, h_hbm, wg_ref, wl_ref, bg_ref, bl_ref,
                   o_ref, xg, xb, wgb, wlb, sem):
    s = pl.program_id(0)
    c = pl.program_id(1)
    nsub = nsub_ref[s]
    slot = s % 2

    def gather(idx_ref, dst_slot, n_sub_blocks):
        def body(j, carry):
            for u in range(GATHER_UNROLL):
                a = j * GATHER_UNROLL + u
                pltpu.make_async_copy(h_hbm.at[pl.ds(idx_ref[0, a], 1)], xg.at[dst_slot, pl.ds(a, 1)],
                                      sem.at[dst_slot]).start()
            return carry
        lax.fori_loop(0, n_sub_blocks * (MOE_SUB // GATHER_UNROLL), body, 0)

    @pl.when((nsub > 0) & (c == 0))
    def _():
        @pl.when(s == 0)
        def _():
            gather(tok_ref, 0, nsub)

        s_next = jnp.minimum(s + 1, MOE_NSB - 1)
        n_next = jnp.where(s + 1 < MOE_NSB, nsub_ref[s_next], 0)

        @pl.when(n_next > 0)
        def _():
            gather(tok_next_ref, 1 - slot, n_next)

        def wait_rows(r, carry):
            pltpu.make_async_copy(h_hbm.at[pl.ds(0, MOE_SUB)], xg.at[slot, pl.ds(0, MOE_SUB)], sem.at[slot]).wait()
            return carry

        lax.fori_loop(0, nsub, wait_rows, 0)

        def unpack(r, carry):
            rows = pl.ds(pl.multiple_of(r * MOE_SUB, MOE_SUB), MOE_SUB)
            w = xg[slot, rows, :]
            xb[rows, 0:HALF_D] = pltpu.bitcast(w << SHIFT16, F32).astype(BF16)
            xb[rows, HALF_D:D_MODEL] = pltpu.bitcast(w & HI_MASK, F32).astype(BF16)
            return carry

        lax.fori_loop(0, nsub, unpack, 0)

    def swiglu_store(rows, xs, wg, wl):
        glu = jnp.dot(xs, wg, preferred_element_type=F32) + bg_ref[...]
        lin = jnp.dot(xs, wl, preferred_element_type=F32) + bl_ref[...]
        glu = jnp.minimum(glu, SWIGLU_LIMIT)
        lin = jnp.clip(lin, -SWIGLU_LIMIT, SWIGLU_LIMIT)
        act = glu / (1.0 + jnp.exp(-SWIGLU_ALPHA * glu)) * (lin + 1.0)
        o_ref[rows, :] = act.astype(BF16)

    def first(r0, n):
        wg = wg_ref[...].astype(BF16)
        wl = wl_ref[...].astype(BF16)
        wgb[...] = wg
        wlb[...] = wl
        rows = pl.ds(r0, n)
        swiglu_store(rows, xb[rows, :], wg, wl)

    def rest(r0, n):
        rows = pl.ds(r0, n)
        swiglu_store(rows, xb[rows, :], wgb[...], wlb[...])

    @pl.when(nsub > 0)
    def _():
        _for_valid_rows(nsub, first, rest)


def _moe_up(h_packed, row_tok, sbe, sbi, nsub, w_gate_up, b_gate_up, layer):
    n_c = MOE_D_FF // MOE_FC
    last_c = n_c - 1

    def c_eff(s, c, nsub_ref):
        return jnp.where(nsub_ref[s] > 0, c, last_c)

    grid_spec = pltpu.PrefetchScalarGridSpec(
        num_scalar_prefetch=3,
        grid=(MOE_NSB, n_c),
        in_specs=[
            pl.BlockSpec((None, 1, MOE_R), lambda s, c, e, i, n: (i[s], 0, 0), memory_space=pltpu.SMEM),
            pl.BlockSpec((None, 1, MOE_R), lambda s, c, e, i, n: (i[jnp.minimum(s + 1, MOE_NSB - 1)], 0, 0),
                         memory_space=pltpu.SMEM),
            pl.BlockSpec(memory_space=pl.ANY),
            pl.BlockSpec((None, None, D_MODEL, MOE_FC), lambda s, c, e, i, n: (layer, e[s], 0, c_eff(s, c, n))),
            pl.BlockSpec((None, None, D_MODEL, MOE_FC),
                         lambda s, c, e, i, n: (layer, e[s], 0, n_c + c_eff(s, c, n))),
            pl.BlockSpec((None, None, 1, MOE_FC), lambda s, c, e, i, n: (layer, e[s], 0, c_eff(s, c, n))),
            pl.BlockSpec((None, None, 1, MOE_FC), lambda s, c, e, i, n: (layer, e[s], 0, n_c + c_eff(s, c, n))),
        ],
        out_specs=pl.BlockSpec((MOE_R, MOE_FC), lambda s, c, e, i, n: (i[s], c_eff(s, c, n))),
        scratch_shapes=[pltpu.VMEM((2, MOE_R, HALF_D), U32), pltpu.VMEM((MOE_R, D_MODEL), BF16),
                        pltpu.VMEM((D_MODEL, MOE_FC), BF16), pltpu.VMEM((D_MODEL, MOE_FC), BF16),
                        pltpu.SemaphoreType.DMA((2,))],
    )
    bgu = b_gate_up.reshape(DEPTH, N_EXPERTS, 1, 2 * MOE_D_FF)
    tok3 = row_tok.reshape(MOE_NSB, 1, MOE_R)
    return pl.pallas_call(
        _moe_up_kernel,
        name="moe_up",
        grid_spec=grid_spec,
        out_shape=jax.ShapeDtypeStruct((MOE_CAP, MOE_D_FF), BF16),
        compiler_params=_params(("arbitrary", "arbitrary"), 56),
    )(sbe, sbi, nsub, tok3, tok3, h_packed, w_gate_up, w_gate_up, bgu, bgu)


def _moe_down_kernel(sbe_ref, sbi_ref, nsub_ref, h_ref, w_ref, b_ref, o_ref, wb):
    s = pl.program_id(0)
    nsub = nsub_ref[s]

    def first(r0, n):
        w = w_ref[...].astype(BF16)
        wb[...] = w
        rows = pl.ds(r0, n)
        o_ref[rows, :] = jnp.dot(h_ref[rows, :], w, preferred_element_type=F32) + b_ref[...]

    def rest(r0, n):
        rows = pl.ds(r0, n)
        o_ref[rows, :] = jnp.dot(h_ref[rows, :], wb[...], preferred_element_type=F32) + b_ref[...]

    @pl.when(nsub > 0)
    def _():
        _for_valid_rows(nsub, first, rest)


def _moe_down(h_sorted, sbe, sbi, nsub, w_down, b_down, layer):
    n_c = D_MODEL // MOE_NC
    last_c = n_c - 1

    def c_eff(s, c, nsub_ref):
        return jnp.where(nsub_ref[s] > 0, c, last_c)

    grid_spec = pltpu.PrefetchScalarGridSpec(
        num_scalar_prefetch=3,
        grid=(MOE_NSB, n_c),
        in_specs=[
            pl.BlockSpec((MOE_R, MOE_D_FF), lambda s, c, e, i, n: (i[s], 0)),
            pl.BlockSpec((None, None, MOE_D_FF, MOE_NC), lambda s, c, e, i, n: (layer, e[s], 0, c_eff(s, c, n))),
            pl.BlockSpec((None, None, 1, MOE_NC), lambda s, c, e, i, n: (layer, e[s], 0, c_eff(s, c, n))),
        ],
        out_specs=pl.BlockSpec((MOE_R, MOE_NC), lambda s, c, e, i, n: (i[s], c_eff(s, c, n))),
        scratch_shapes=[pltpu.VMEM((MOE_D_FF, MOE_NC), BF16)],
    )
    return pl.pallas_call(
        _moe_down_kernel,
        name="moe_down",
        grid_spec=grid_spec,
        out_shape=jax.ShapeDtypeStruct((MOE_CAP, D_MODEL), F32),
        compiler_params=_params(("arbitrary", "arbitrary"), 48),
    )(sbe, sbi, nsub, h_sorted, w_down, b_down.reshape(DEPTH, N_EXPERTS, 1, D_MODEL))


def _route(top_idx):
    flat_e = top_idx.reshape(-1)
    onehot = (flat_e[:, None] == jnp.arange(N_EXPERTS, dtype=I32)[None, :]).astype(I32)
    csum = jnp.cumsum(onehot, axis=0)
    rank = jnp.take_along_axis(csum, flat_e[:, None], axis=1)[:, 0] - 1
    counts = csum[-1]
    nsb = (counts + MOE_R - 1) // MOE_R
    sb_end = jnp.cumsum(nsb)
    sb_off = sb_end - nsb
    dest = sb_off[flat_e] * MOE_R + rank
    n_real = sb_end[-1]
    s = jnp.arange(MOE_NSB, dtype=I32)
    s_eff = jnp.minimum(s, n_real - 1)
    sbe = jnp.minimum(jnp.searchsorted(sb_end, s_eff, side='right'), N_EXPERTS - 1).astype(I32)
    valid = jnp.clip(counts[sbe] - (s_eff - sb_off[sbe]) * MOE_R, 0, MOE_R)
    valid = jnp.where(s < n_real, valid, 0)
    nsub = ((valid + MOE_SUB - 1) // MOE_SUB).astype(I32)
    dest = dest.astype(I32)
    row_tok = jnp.zeros((MOE_CAP,), I32).at[dest].set(jnp.arange(N_ASSIGN, dtype=I32) // TOP_K)
    return dest, row_tok, sbe, s_eff.astype(I32), nsub


def _moe_experts(h_packed, top_idx, w_gate_up, b_gate_up, w_down, b_down, layer):
    dest, row_tok, sbe, sbi, nsub = _route(top_idx)
    h_sorted = _moe_up(h_packed, row_tok, sbe, sbi, nsub, w_gate_up, b_gate_up, layer)
    return _moe_down(h_sorted, sbe, sbi, nsub, w_down, b_down, layer), dest


def _pb_rows(mod_piece):
    idx = jnp.array([0] * N_PROMPT_PB + list(range(1, 1 + DEC_BATCH)), I32)
    return mod_piece[idx][:, None, :]


def kernel(x_prompt, x_sample, cache_a_k, cache_a_v, cache_b_k, cache_b_v, cache_c_k, cache_c_v, c, c_ctx, w_ada, b_ada, g_pre_mix, g_post_mix, g_pre_ffn, g_post_ffn, w_in_even, w_out_even, rpb_a, q_norm_b, k_norm_b, w_in_odd, w_out_odd, sink_c, w_router, b_router, w_gate_up, b_gate_up, w_down, b_down):
    x = jnp.concatenate([x_prompt.reshape(N_PROMPT, D_MODEL), x_sample.reshape(N_SAMPLE, D_MODEL)], axis=0)
    cond8 = jnp.concatenate([c_ctx[None, :], c, jnp.zeros((8 - 1 - DEC_BATCH, D_MODEL), F32)], axis=0)
    mods = _ada_mod(cond8, w_ada, b_ada)
    cos_tab, sin_tab = _rope_tables()
    no_sink = jnp.full((max(A_HEADS, B_HEADS),), -jnp.inf, F32)
    states = {}

    for l in range(DEPTH):
        j = l // 2
        m = [_pb_rows(mods[l, :, k * D_MODEL:(k + 1) * D_MODEL]) for k in range(6)]
        if l % 2 == 0:
            w_in = w_in_even[j].astype(BF16)
            w_out = w_out_even[j].astype(BF16)
            nb = EVEN_IN // IN_BN
            qb0 = 3 * A_HEADS * HEAD_DIM // IN_BN
            kb0 = qb0 + B_HEADS * HEAD_DIM // IN_BN
            vb0 = kb0 + B_KV_HEADS * HEAD_DIM // IN_BN
            modes = jnp.array([MODE_PLAIN] * qb0 + [MODE_NORM_Q] * (kb0 - qb0) + [MODE_NORM_K] * (vb0 - kb0)
                              + [MODE_PLAIN] * (nb - vb0), I32)
            gains = jnp.concatenate([q_norm_b[j][None], k_norm_b[j][None], jnp.ones((6, HEAD_DIM), F32)], axis=0)
        else:
            w_in = w_in_odd[j].astype(BF16)
            w_out = w_out_odd[j].astype(BF16)
            nb = ODD_IN // IN_BN
            vc0 = (C_HEADS + C_KV_HEADS) * HEAD_DIM // IN_BN
            modes = jnp.array([MODE_ROPE] * vc0 + [MODE_PLAIN] * (nb - vc0), I32)
            gains = jnp.ones((8, HEAD_DIM), F32)

        proj = _in_proj(x, g_pre_mix[l], m[0], m[1], w_in, modes, gains, cos_tab, sin_tab)

        if l % 2 == 0:
            ah = A_HEADS * HEAD_DIM
            ctx_a = _ctx_attention(proj, no_sink, 0, ah, 2 * ah, A_HEADS, A_HEADS, 8)
            ctx_b = _ctx_attention(proj, no_sink, 3 * ah, 3 * ah + B_HEADS * HEAD_DIM,
                                   3 * ah + (B_HEADS + B_KV_HEADS) * HEAD_DIM, B_HEADS, B_KV_HEADS, B_KV_HEADS)
            lat_a = _na_attention(proj, cache_a_k, cache_a_v, rpb_a[j], j)
            lat_b = _dense_attention(proj, cache_b_k, cache_b_v, j)
            prompt_parts, latent_parts = [ctx_a, ctx_b], [lat_a, lat_b]
            pp = proj[:N_PROMPT]
            kb_c = 3 * ah + B_HEADS * HEAD_DIM
            vb_c = kb_c + B_KV_HEADS * HEAD_DIM
            states.setdefault('ak', []).append(pp[:, ah:2 * ah].reshape(BATCH, SEQ, A_HEADS, HEAD_DIM))
            states.setdefault('av', []).append(pp[:, 2 * ah:3 * ah].reshape(BATCH, SEQ, A_HEADS, HEAD_DIM))
            states.setdefault('bk', []).append(pp[:, kb_c:vb_c].reshape(BATCH, SEQ, B_KV_HEADS, HEAD_DIM))
            states.setdefault('bv', []).append(pp[:, vb_c:].reshape(BATCH, SEQ, B_KV_HEADS, HEAD_DIM))
        else:
            qc = C_HEADS * HEAD_DIM
            kvw = C_KV_HEADS * HEAD_DIM
            ctx_c = _ctx_attention(proj, sink_c[j], 0, qc, qc + kvw, C_HEADS, C_KV_HEADS, C_KV_HEADS)
            lat_c = _window_attention(proj, cache_c_k, cache_c_v, sink_c[j], j)
            prompt_parts, latent_parts = [ctx_c], [lat_c]
            pp = proj[:N_PROMPT]
            states.setdefault('ck', []).append(pp[:, qc:qc + kvw].reshape(BATCH, SEQ, C_KV_HEADS, HEAD_DIM))
            states.setdefault('cv', []).append(pp[:, qc + kvw:].reshape(BATCH, SEQ, C_KV_HEADS, HEAD_DIM))

        y = _out_proj(prompt_parts, latent_parts, w_out)
        wr = jnp.pad(w_router[l], ((0, 0), (0, LANES - N_EXPERTS)))
        wr_hi = wr.astype(BF16)
        wr_lo = (wr - wr_hi.astype(F32)).astype(BF16)
        br = jnp.concatenate([b_router[l], jnp.full((LANES - N_EXPERTS,), NEG_BIG, F32)])[None, :]
        x, h_ffn, top_i, top_g = _post_mix(y, x, m[2], g_post_mix[l], g_pre_ffn[l], m[3], m[4], wr_hi, wr_lo, br)
        expert_out, dest = _moe_experts(h_ffn, top_i[:, :TOP_K], w_gate_up, b_gate_up, w_down, b_down, l)
        x = _combine_post_ffn(expert_out, dest, top_g, x, m[5], g_post_ffn[l], split_out=(l == DEPTH - 1))

    y_prompt = x[0].reshape(BATCH, SEQ, D_MODEL)
    y_sample = x[1].reshape(DEC_BATCH, DEC_SEQ, D_MODEL)
    st = {k: jnp.stack(v, axis=1) for k, v in states.items()}
    return (y_prompt, y_sample, st['ak'], st['av'], st['bk'], st['bv'], st['ck'], st['cv'])
```

```python
import functools

import jax
import jax.numpy as jnp
import numpy as np
from jax import lax
from jax.experimental import pallas as pl
from jax.experimental.pallas import tpu as pltpu

F32 = jnp.float32
BF16 = jnp.bfloat16
I32 = jnp.int32
U32 = jnp.uint32

D_MODEL = 4096
BATCH = 16
SEQ = 256
DEPTH = 2
DEC_BATCH = 4
DEC_SEQ = 2048
PAST_LEN = 512
GRID_W = 64
GRID_ROWS = DEC_SEQ // GRID_W
HEAD_DIM = 128
A_HEADS = 16
B_HEADS = 16
B_KV_HEADS = 4
C_HEADS = 32
C_KV_HEADS = 4
WIN_H = 8
WIN_W = 16
C_WINDOW = 128
N_EXPERTS = 32
TOP_K = 4
MOE_D_FF = 2048
SWIGLU_LIMIT = 7.0
SWIGLU_ALPHA = 1.702
ROPE_THETA = 10000.0
RMS_EPS = 1e-6
ATTN_SCALE = HEAD_DIM ** -0.5
EVEN_IN = (3 * A_HEADS + B_HEADS + 2 * B_KV_HEADS) * HEAD_DIM
ODD_IN = (C_HEADS + 2 * C_KV_HEADS) * HEAD_DIM

N_PROMPT = BATCH * SEQ
N_SAMPLE = DEC_BATCH * DEC_SEQ
N_TOK = N_PROMPT + N_SAMPLE
PB_TOK = DEC_SEQ
N_PB = N_TOK // PB_TOK
N_PROMPT_PB = N_PROMPT // PB_TOK

LANES = 128
MIB = 1024 * 1024

MOE_R = 2048
MOE_SUB = 256
N_ASSIGN = N_TOK * TOP_K
MOE_NSB = N_ASSIGN // MOE_R + N_EXPERTS
MOE_CAP = MOE_NSB * MOE_R
MOE_FC = 256
MOE_NC = 512

NEG_BIG = -1e30
HALF_D = D_MODEL // 2
HI_MASK = np.uint32(0xFFFF0000)
SHIFT16 = np.uint32(16)


def _params(sem, vmem_mib):
    return pltpu.CompilerParams(dimension_semantics=sem, vmem_limit_bytes=vmem_mib * MIB)


def _ada_kernel(c_ref, w_ref, b_ref, o_ref):
    c = c_ref[...]
    s = (c / (1.0 + jnp.exp(-c))).astype(BF16)
    o_ref[...] = jnp.dot(s, w_ref[...].astype(BF16), preferred_element_type=F32) + b_ref[...]


def _ada_mod(cond8, w_ada, b_ada):
    bn = 512
    n_out = 6 * D_MODEL
    return pl.pallas_call(
        _ada_kernel,
        name="ada_mod",
        grid=(DEPTH, n_out // bn),
        in_specs=[
            pl.BlockSpec((8, D_MODEL), lambda l, j: (0, 0)),
            pl.BlockSpec((None, D_MODEL, bn), lambda l, j: (l, 0, j)),
            pl.BlockSpec((None, 1, bn), lambda l, j: (l, 0, j)),
        ],
        out_specs=pl.BlockSpec((None, 8, bn), lambda l, j: (l, 0, j)),
        out_shape=jax.ShapeDtypeStruct((DEPTH, 8, n_out), F32),
        compiler_params=_params(("arbitrary", "arbitrary"), 40),
    )(cond8, w_ada, b_ada.reshape(DEPTH, 1, n_out))


IN_BM = 512
IN_BN = 512
MODE_PLAIN, MODE_NORM_Q, MODE_NORM_K, MODE_ROPE = 0, 1, 2, 3


def _inproj_kernel(mode_ref, x_ref, g_ref, sh_ref, sc_ref, w_ref, gain_ref, cos_ref, sin_ref,
                   o_ref, h_ref):
    j = pl.program_id(1)

    @pl.when(j == 0)
    def _():
        x = x_ref[...]
        r = lax.rsqrt(jnp.mean(x * x, axis=-1, keepdims=True) + RMS_EPS)
        h = (x * r * g_ref[...]) * (1.0 + sc_ref[...]) + sh_ref[...]
        h_ref[...] = h.astype(BF16)

    acc = jnp.dot(h_ref[...], w_ref[...], preferred_element_type=F32)
    mode = mode_ref[j]

    @pl.when(mode == MODE_PLAIN)
    def _():
        o_ref[...] = acc

    def rope_store(hh, y):
        lane = lax.broadcasted_iota(I32, (IN_BM, LANES), 1)
        sw = jnp.where((lane % 64) < 32, pltpu.roll(y, 96, 1), pltpu.roll(y, 32, 1))
        o_ref[:, hh * HEAD_DIM:(hh + 1) * HEAD_DIM] = y * cos_ref[...] + sw * sin_ref[...]

    @pl.when(mode == MODE_ROPE)
    def _():
        for hh in range(IN_BN // HEAD_DIM):
            rope_store(hh, acc[:, hh * HEAD_DIM:(hh + 1) * HEAD_DIM])

    @pl.when((mode == MODE_NORM_Q) | (mode == MODE_NORM_K))
    def _():
        gain = jnp.where(mode == MODE_NORM_Q, gain_ref[0:1, :], gain_ref[1:2, :])
        for hh in range(IN_BN // HEAD_DIM):
            xh = acc[:, hh * HEAD_DIM:(hh + 1) * HEAD_DIM]
            rs = lax.rsqrt(jnp.mean(xh * xh, axis=-1, keepdims=True) + RMS_EPS)
            rope_store(hh, xh * rs * gain)


def _in_proj(x, g_pre, shift, scale, w_bf16, modes, gains, cos_tab, sin_tab):
    n = w_bf16.shape[1]
    n_i = N_TOK // IN_BM
    n_prompt_blocks = N_PROMPT // IN_BM
    blocks_per_seq = DEC_SEQ // IN_BM

    def tab_idx(i, j, m):
        return (jnp.where(i < n_prompt_blocks, 0, 1 + (i - n_prompt_blocks) % blocks_per_seq), 0)

    grid_spec = pltpu.PrefetchScalarGridSpec(
        num_scalar_prefetch=1,
        grid=(n_i, n // IN_BN),
        in_specs=[
            pl.BlockSpec((IN_BM, D_MODEL), lambda i, j, m: (i, 0)),
            pl.BlockSpec((1, D_MODEL), lambda i, j, m: (0, 0)),
            pl.BlockSpec((None, 1, D_MODEL), lambda i, j, m: (i * IN_BM // PB_TOK, 0, 0)),
            pl.BlockSpec((None, 1, D_MODEL), lambda i, j, m: (i * IN_BM // PB_TOK, 0, 0)),
            pl.BlockSpec((D_MODEL, IN_BN), lambda i, j, m: (0, j)),
            pl.BlockSpec((8, LANES), lambda i, j, m: (0, 0)),
            pl.BlockSpec((IN_BM, LANES), tab_idx),
            pl.BlockSpec((IN_BM, LANES), tab_idx),
        ],
        out_specs=pl.BlockSpec((IN_BM, IN_BN), lambda i, j, m: (i, j)),
        scratch_shapes=[pltpu.VMEM((IN_BM, D_MODEL), BF16)],
    )
    return pl.pallas_call(
        _inproj_kernel,
        name="in_proj",
        grid_spec=grid_spec,
        out_shape=jax.ShapeDtypeStruct((N_TOK, n), F32),
        compiler_params=_params(("arbitrary", "arbitrary"), 48),
    )(modes, x, g_pre.reshape(1, D_MODEL), shift, scale, w_bf16, gains, cos_tab, sin_tab)


def _rope_tables():
    t = jnp.arange(DEC_SEQ)
    row = (t // GRID_W).astype(F32)
    col = (t % GRID_W).astype(F32)
    n_freq = HEAD_DIM // 4
    inv = ROPE_THETA ** (-jnp.arange(n_freq, dtype=F32) / n_freq)
    ar = row[:, None] * inv
    ac = col[:, None] * inv
    cos = jnp.concatenate([jnp.cos(ar), jnp.cos(ar), jnp.cos(ac), jnp.cos(ac)], axis=-1)
    sin = jnp.concatenate([-jnp.sin(ar), jnp.sin(ar), -jnp.sin(ac), jnp.sin(ac)], axis=-1)
    cos = jnp.concatenate([jnp.ones((IN_BM, HEAD_DIM), F32), cos], axis=0)
    sin = jnp.concatenate([jnp.zeros((IN_BM, HEAD_DIM), F32), sin], axis=0)
    return cos, sin


MM_BM = 1024
MM_BN = 512


MM_PROMPT_BLOCKS = N_PROMPT // MM_BM


def _out_proj_kernel(*refs, n_parts):
    prompt_refs = refs[:n_parts]
    latent_refs = refs[n_parts:2 * n_parts]
    w_ref, o_ref = refs[2 * n_parts:]
    i = pl.program_id(0)

    def run(parts):
        acc = None
        k0 = 0
        for r in parts:
            kk = r.shape[1]
            d = jnp.dot(r[...], w_ref[k0:k0 + kk, :], preferred_element_type=F32)
            acc = d if acc is None else acc + d
            k0 += kk
        o_ref[...] = acc

    @pl.when(i < MM_PROMPT_BLOCKS)
    def _():
        run(prompt_refs)

    @pl.when(i >= MM_PROMPT_BLOCKS)
    def _():
        run(latent_refs)


def _out_proj(prompt_parts, latent_parts, w_bf16):
    k, n = w_bf16.shape
    n_parts = len(prompt_parts)
    last_prompt = MM_PROMPT_BLOCKS - 1
    in_specs = ([pl.BlockSpec((MM_BM, p.shape[1]), lambda i, j: (jnp.minimum(i, last_prompt), 0))
                 for p in prompt_parts]
                + [pl.BlockSpec((MM_BM, p.shape[1]), lambda i, j: (jnp.maximum(i - MM_PROMPT_BLOCKS, 0), 0))
                   for p in latent_parts]
                + [pl.BlockSpec((k, MM_BN), lambda i, j: (0, j))])
    return pl.pallas_call(
        functools.partial(_out_proj_kernel, n_parts=n_parts),
        name="out_proj",
        grid=(N_TOK // MM_BM, n // MM_BN),
        in_specs=in_specs,
        out_specs=pl.BlockSpec((MM_BM, MM_BN), lambda i, j: (i, j)),
        out_shape=jax.ShapeDtypeStruct((N_TOK, n), F32),
        compiler_params=_params(("arbitrary", "arbitrary"), 56),
    )(*prompt_parts, *latent_parts, w_bf16)


def _softmax_pv(s_list, v_list, sink=None):
    m = jnp.max(s_list[0], axis=-1, keepdims=True)
    for s in s_list[1:]:
        m = jnp.maximum(m, jnp.max(s, axis=-1, keepdims=True))
    if sink is not None:
        m = jnp.maximum(m, sink)
    den = None
    out = None
    for s, v in zip(s_list, v_list):
        p = jnp.exp(s - m)
        d = jnp.sum(p, axis=-1, keepdims=True)
        o = jnp.dot(p.astype(BF16), v, preferred_element_type=F32)
        den = d if den is None else den + d
        out = o if out is None else out + o
    if sink is not None:
        den = den + jnp.exp(sink - m)
    return out / den


def _qk(q, k):
    return lax.dot_general(q, k, (((1,), (1,)), ((), ())), preferred_element_type=F32) * ATTN_SCALE


def _ctx_attn_kernel(sink_ref, q_ref, k_ref, v_ref, o_ref, *, nkv, g):
    kb = pl.program_id(1)
    for kv in range(nkv):
        k = k_ref[:, kv * HEAD_DIM:(kv + 1) * HEAD_DIM].astype(BF16)
        v = v_ref[:, kv * HEAD_DIM:(kv + 1) * HEAD_DIM].astype(BF16)
        for gi in range(g):
            c0 = (kv * g + gi) * HEAD_DIM
            q = q_ref[:, c0:c0 + HEAD_DIM].astype(BF16)
            sink = sink_ref[(kb * nkv + kv) * g + gi]
            o = _softmax_pv([_qk(q, k)], [v], sink)
            o_ref[:, c0:c0 + HEAD_DIM] = o.astype(o_ref.dtype)


def _ctx_attention(proj, sink, q_col, k_col, v_col, hq, hkv, nkv):
    g = hq // hkv
    qw = nkv * g * HEAD_DIM
    kw = nkv * HEAD_DIM
    return pl.pallas_call(
        functools.partial(_ctx_attn_kernel, nkv=nkv, g=g),
        name="ctx_attn",
        grid=(BATCH, hkv // nkv),
        in_specs=[
            pl.BlockSpec(memory_space=pltpu.SMEM),
            pl.BlockSpec((SEQ, qw), lambda b, h: (b, q_col // qw + h)),
            pl.BlockSpec((SEQ, kw), lambda b, h: (b, k_col // kw + h)),
            pl.BlockSpec((SEQ, kw), lambda b, h: (b, v_col // kw + h)),
        ],
        out_specs=pl.BlockSpec((SEQ, qw), lambda b, h: (b, h)),
        out_shape=jax.ShapeDtypeStruct((N_PROMPT, hq * HEAD_DIM), BF16),
        compiler_params=_params(("arbitrary", "arbitrary"), 40),
    )(sink, proj, proj, proj)


DENSE_BQ = 256


def _cache_bf16(step, srcs, dsts):
    @pl.when(step == 0)
    def _():
        for src, dst in zip(srcs, dsts):
            dst[...] = src[...].astype(BF16)


def _kv_scratch(width):
    return [pltpu.VMEM((DEC_SEQ, width), BF16), pltpu.VMEM((DEC_SEQ, width), BF16),
            pltpu.VMEM((PAST_LEN, width), BF16), pltpu.VMEM((PAST_LEN, width), BF16)]


def _dense_attn_kernel(q_ref, k_ref, v_ref, kc_ref, vc_ref, o_ref, kb, vb, kcb, vcb, *, g):
    _cache_bf16(pl.program_id(2), (k_ref, v_ref, kc_ref, vc_ref), (kb, vb, kcb, vcb))
    k = kb[...]
    v = vb[...]
    kc = kcb[...]
    vc = vcb[...]
    for gi in range(g):
        c0 = gi * HEAD_DIM
        q = q_ref[:, c0:c0 + HEAD_DIM].astype(BF16)
        o = _softmax_pv([_qk(q, k), _qk(q, kc)], [v, vc])
        o_ref[:, c0:c0 + HEAD_DIM] = o.astype(o_ref.dtype)


def _dense_attention(proj, cache_k, cache_v, layer_j):
    g = B_HEADS // B_KV_HEADS
    qw = g * HEAD_DIM
    nq = DEC_SEQ // DENSE_BQ
    q_col = 3 * A_HEADS * HEAD_DIM
    k_col = q_col + B_HEADS * HEAD_DIM
    v_col = k_col + B_KV_HEADS * HEAD_DIM
    row0 = N_PROMPT // DENSE_BQ
    ck = cache_k.reshape(DEC_BATCH, -1, PAST_LEN, B_KV_HEADS * HEAD_DIM)
    cv = cache_v.reshape(DEC_BATCH, -1, PAST_LEN, B_KV_HEADS * HEAD_DIM)
    return pl.pallas_call(
        functools.partial(_dense_attn_kernel, g=g),
        name="dense_attn",
        grid=(DEC_BATCH, B_KV_HEADS, nq),
        in_specs=[
            pl.BlockSpec((DENSE_BQ, qw), lambda b, h, i: (row0 + b * nq + i, q_col // qw + h)),
            pl.BlockSpec((DEC_SEQ, HEAD_DIM), lambda b, h, i: (N_PROMPT_PB + b, k_col // HEAD_DIM + h)),
            pl.BlockSpec((DEC_SEQ, HEAD_DIM), lambda b, h, i: (N_PROMPT_PB + b, v_col // HEAD_DIM + h)),
            pl.BlockSpec((None, None, PAST_LEN, HEAD_DIM), lambda b, h, i: (b, layer_j, 0, h)),
            pl.BlockSpec((None, None, PAST_LEN, HEAD_DIM), lambda b, h, i: (b, layer_j, 0, h)),
        ],
        out_specs=pl.BlockSpec((DENSE_BQ, qw), lambda b, h, i: (b * nq + i, h)),
        scratch_shapes=_kv_scratch(HEAD_DIM),
        out_shape=jax.ShapeDtypeStruct((N_SAMPLE, B_HEADS * HEAD_DIM), BF16),
        compiler_params=_params(("arbitrary", "arbitrary", "arbitrary"), 48),
    )(proj, proj, proj, ck, cv)


WIN_BQ = 256
WIN_KEYS = WIN_BQ + 2 * C_WINDOW


def _window_attn_kernel(sink_ref, q_ref, k_ref, v_ref, kc_ref, vc_ref, o_ref, kb, vb, kcb, vcb, *, g):
    h = pl.program_id(1)
    i = pl.program_id(2)
    _cache_bf16(i, (k_ref, v_ref, kc_ref, vc_ref), (kb, vb, kcb, vcb))
    q0 = i * WIN_BQ
    k0 = pl.multiple_of(jnp.clip(q0 - C_WINDOW, 0, DEC_SEQ - WIN_KEYS), C_WINDOW)
    k = kb[pl.ds(k0, WIN_KEYS), :]
    v = vb[pl.ds(k0, WIN_KEYS), :]
    kc = kcb[...]
    vc = vcb[...]
    q_pos = q0 + lax.broadcasted_iota(I32, (WIN_BQ, WIN_KEYS), 0)
    k_pos = k0 + lax.broadcasted_iota(I32, (WIN_BQ, WIN_KEYS), 1)
    ok = jnp.abs(q_pos - k_pos) <= C_WINDOW
    for gi in range(g):
        c0 = gi * HEAD_DIM
        q = q_ref[:, c0:c0 + HEAD_DIM].astype(BF16)
        s_lat = jnp.where(ok, _qk(q, k), -jnp.inf)
        o = _softmax_pv([s_lat, _qk(q, kc)], [v, vc], sink_ref[h * g + gi])
        o_ref[:, c0:c0 + HEAD_DIM] = o.astype(o_ref.dtype)


def _window_attention(proj, cache_k, cache_v, sink, layer_j):
    g = C_HEADS // C_KV_HEADS
    qw = g * HEAD_DIM
    nq = DEC_SEQ // WIN_BQ
    k_col = C_HEADS * HEAD_DIM
    v_col = k_col + C_KV_HEADS * HEAD_DIM
    row0 = N_PROMPT // WIN_BQ
    ck = cache_k.reshape(DEC_BATCH, -1, PAST_LEN, C_KV_HEADS * HEAD_DIM)
    cv = cache_v.reshape(DEC_BATCH, -1, PAST_LEN, C_KV_HEADS * HEAD_DIM)
    return pl.pallas_call(
        functools.partial(_window_attn_kernel, g=g),
        name="window_attn",
        grid=(DEC_BATCH, C_KV_HEADS, nq),
        in_specs=[
            pl.BlockSpec(memory_space=pltpu.SMEM),
            pl.BlockSpec((WIN_BQ, qw), lambda b, h, i: (row0 + b * nq + i, h)),
            pl.BlockSpec((DEC_SEQ, HEAD_DIM), lambda b, h, i: (N_PROMPT_PB + b, k_col // HEAD_DIM + h)),
            pl.BlockSpec((DEC_SEQ, HEAD_DIM), lambda b, h, i: (N_PROMPT_PB + b, v_col // HEAD_DIM + h)),
            pl.BlockSpec((None, None, PAST_LEN, HEAD_DIM), lambda b, h, i: (b, layer_j, 0, h)),
            pl.BlockSpec((None, None, PAST_LEN, HEAD_DIM), lambda b, h, i: (b, layer_j, 0, h)),
        ],
        out_specs=pl.BlockSpec((WIN_BQ, qw), lambda b, h, i: (b * nq + i, h)),
        scratch_shapes=_kv_scratch(HEAD_DIM),
        out_shape=jax.ShapeDtypeStruct((N_SAMPLE, C_HEADS * HEAD_DIM), BF16),
        compiler_params=_params(("arbitrary", "arbitrary", "arbitrary"), 48),
    )(sink, proj, proj, proj, ck, cv)


NA_HC = 4
NA_KEYS = WIN_H * GRID_W


def _na_row_start(i):
    return jnp.clip(i - WIN_H // 2, 0, GRID_ROWS - WIN_H)


def _na_attn_kernel(q_ref, k_ref, v_ref, kc_ref, vc_ref, bias_ref, o_ref, kb, vb, kcb, vcb):
    i = pl.program_id(2)
    _cache_bf16(i, (k_ref, v_ref, kc_ref, vc_ref), (kb, vb, kcb, vcb))
    r0 = pl.multiple_of(_na_row_start(i) * GRID_W, GRID_W)
    for hh in range(NA_HC):
        cs = slice(hh * HEAD_DIM, (hh + 1) * HEAD_DIM)
        q = q_ref[:, cs].astype(BF16)
        s_lat = _qk(q, kb[pl.ds(r0, NA_KEYS), cs]) + bias_ref[hh]
        o = _softmax_pv([s_lat, _qk(q, kcb[:, cs])], [vb[pl.ds(r0, NA_KEYS), cs], vcb[:, cs]])
        o_ref[:, cs] = o.astype(o_ref.dtype)


def _na_bias_table(rpb):
    qc = jnp.arange(GRID_W)[:, None]
    kc = jnp.arange(GRID_W)[None, :]
    ws = jnp.clip(qc - WIN_W // 2, 0, GRID_W - WIN_W)
    ok = (kc >= ws) & (kc < ws + WIN_W)
    dc = jnp.clip(kc - qc + WIN_W - 1, 0, 2 * WIN_W - 2)
    t = rpb[:, :, dc]
    t = jnp.where(ok[None, None], t, -jnp.inf)
    dr = jnp.arange(WIN_H)[:, None] + jnp.arange(WIN_H)[None, :]
    b = t[:, dr]
    b = b.transpose(0, 1, 3, 2, 4).reshape(A_HEADS, WIN_H, GRID_W, NA_KEYS)
    return b.astype(F32)


def _na_attention(proj, cache_k, cache_v, rpb, layer_j):
    cw = NA_HC * HEAD_DIM
    k_col = A_HEADS * HEAD_DIM
    v_col = 2 * A_HEADS * HEAD_DIM
    row0 = N_PROMPT // GRID_W
    ck = cache_k.reshape(DEC_BATCH, -1, PAST_LEN, A_HEADS * HEAD_DIM)
    cv = cache_v.reshape(DEC_BATCH, -1, PAST_LEN, A_HEADS * HEAD_DIM)
    bias = _na_bias_table(rpb)

    def bias_idx(b, h, i):
        return (h, _na_row_start(i) - i + WIN_H - 1, 0, 0)

    return pl.pallas_call(
        _na_attn_kernel,
        name="na_attn",
        grid=(DEC_BATCH, A_HEADS // NA_HC, GRID_ROWS),
        in_specs=[
            pl.BlockSpec((GRID_W, cw), lambda b, h, i: (row0 + b * GRID_ROWS + i, h)),
            pl.BlockSpec((DEC_SEQ, cw), lambda b, h, i: (N_PROMPT_PB + b, k_col // cw + h)),
            pl.BlockSpec((DEC_SEQ, cw), lambda b, h, i: (N_PROMPT_PB + b, v_col // cw + h)),
            pl.BlockSpec((None, None, PAST_LEN, cw), lambda b, h, i: (b, layer_j, 0, h)),
            pl.BlockSpec((None, None, PAST_LEN, cw), lambda b, h, i: (b, layer_j, 0, h)),
            pl.BlockSpec((NA_HC, None, GRID_W, NA_KEYS), bias_idx),
        ],
        out_specs=pl.BlockSpec((GRID_W, cw), lambda b, h, i: (b * GRID_ROWS + i, h)),
        scratch_shapes=_kv_scratch(cw),
        out_shape=jax.ShapeDtypeStruct((N_SAMPLE, A_HEADS * HEAD_DIM), BF16),
        compiler_params=_params(("arbitrary", "arbitrary", "arbitrary"), 48),
    )(proj, proj, proj, ck, cv, bias)


EP_BM = 256


def _rms(x):
    return x * lax.rsqrt(jnp.mean(x * x, axis=-1, keepdims=True) + RMS_EPS)


def _post_mix_kernel(y_ref, x_ref, gate_ref, gpost_ref, gpre_ref, sh_ref, sc_ref,
                     wr_hi_ref, wr_lo_ref, br_ref, xo_ref, h_ref, ti_ref, tg_ref):
    x_new = x_ref[...] + gate_ref[...] * (_rms(y_ref[...]) * gpost_ref[...])
    xo_ref[...] = x_new
    h = (_rms(x_new) * gpre_ref[...]) * (1.0 + sc_ref[...]) + sh_ref[...]
    h_hi = h.astype(BF16)
    h_hi32 = h_hi.astype(F32)
    bits = pltpu.bitcast(h_hi32, U32)
    h_ref[...] = (bits[:, :HALF_D] >> SHIFT16) | (bits[:, HALF_D:] & HI_MASK)
    h_lo = (h - h_hi32).astype(BF16)
    w_hi = wr_hi_ref[...]
    logits = (jnp.dot(h_hi, w_hi, preferred_element_type=F32)
              + jnp.dot(h_lo, w_hi, preferred_element_type=F32)
              + jnp.dot(h_hi, wr_lo_ref[...], preferred_element_type=F32)) + br_ref[...]
    lane = lax.broadcasted_iota(I32, (EP_BM, LANES), 1).astype(F32)
    vals, idxs = [], []
    l = logits
    for _ in range(TOP_K):
        m = jnp.max(l, axis=-1, keepdims=True)
        idx = jnp.min(jnp.where(l == m, lane, float(LANES)), axis=-1, keepdims=True)
        vals.append(m)
        idxs.append(idx)
        l = jnp.where(lane == idx, -jnp.inf, l)
    es = [jnp.exp(v - vals[0]) for v in vals]
    den = es[0] + es[1] + es[2] + es[3]
    ti = jnp.zeros((EP_BM, LANES), F32)
    tg = jnp.zeros((EP_BM, LANES), F32)
    for k in range(TOP_K):
        ti = jnp.where(lane == float(k), idxs[k], ti)
        tg = jnp.where(lane == float(k), es[k] / den, tg)
    ti_ref[...] = ti.astype(I32)
    tg_ref[...] = tg


def _post_mix(y, x, gate, g_post, g_pre, shift, scale, wr_hi, wr_lo, br):
    row = pl.BlockSpec((EP_BM, D_MODEL), lambda i: (i, 0))
    vec = pl.BlockSpec((1, D_MODEL), lambda i: (0, 0))
    mod = pl.BlockSpec((None, 1, D_MODEL), lambda i: (i * EP_BM // PB_TOK, 0, 0))
    wr = pl.BlockSpec((D_MODEL, LANES), lambda i: (0, 0))
    small = pl.BlockSpec((EP_BM, LANES), lambda i: (i, 0))
    packed = pl.BlockSpec((EP_BM, HALF_D), lambda i: (i, 0))
    return pl.pallas_call(
        _post_mix_kernel,
        name="post_mix",
        grid=(N_TOK // EP_BM,),
        in_specs=[row, row, mod, vec, vec, mod, mod, wr, wr, pl.BlockSpec((1, LANES), lambda i: (0, 0))],
        out_specs=[row, packed, small, small],
        out_shape=[jax.ShapeDtypeStruct((N_TOK, D_MODEL), F32),
                   jax.ShapeDtypeStruct((N_TOK, HALF_D), U32),
                   jax.ShapeDtypeStruct((N_TOK, LANES), I32),
                   jax.ShapeDtypeStruct((N_TOK, LANES), F32)],
        compiler_params=_params(("arbitrary",), 48),
    )(y, x, gate, g_post.reshape(1, D_MODEL), g_pre.reshape(1, D_MODEL), shift, scale, wr_hi, wr_lo, br)


CB_TOK = 128
CB_ROWS = CB_TOK * TOP_K
CB_STEPS = N_TOK // CB_TOK


CB_PROMPT_STEPS = N_PROMPT // CB_TOK


def _combine_kernel(dest_ref, dest_next_ref, e_hbm, tg_ref, x_ref, gate_ref, gpost_ref, *rest, split_out):
    out_refs, (buf, sem) = rest[:-2], rest[-2:]
    i = pl.program_id(0)
    slot = i % 2

    def gather(idx_ref, dst_slot):
        def body(t, carry):
            for k in range(TOP_K):
                pltpu.make_async_copy(e_hbm.at[pl.ds(idx_ref[0, t * TOP_K + k], 1)],
                                      buf.at[dst_slot, pl.ds(k * CB_TOK + t, 1)], sem.at[dst_slot]).start()
            return carry
        lax.fori_loop(0, CB_TOK, body, 0, unroll=2)

    @pl.when(i == 0)
    def _():
        gather(dest_ref, 0)

    @pl.when(i + 1 < CB_STEPS)
    def _():
        gather(dest_next_ref, 1 - slot)

    pltpu.make_async_copy(e_hbm.at[pl.ds(0, CB_ROWS)], buf.at[slot], sem.at[slot]).wait()
    tg = tg_ref[...]
    y = buf[slot, 0:CB_TOK, :] * tg[:, 0:1]
    for k in range(1, TOP_K):
        y = y + buf[slot, k * CB_TOK:(k + 1) * CB_TOK, :] * tg[:, k:k + 1]
    x_new = x_ref[...] + gate_ref[...] * (_rms(y) * gpost_ref[...])
    if split_out:
        @pl.when(i < CB_PROMPT_STEPS)
        def _():
            out_refs[0][...] = x_new

        @pl.when(i >= CB_PROMPT_STEPS)
        def _():
            out_refs[1][...] = x_new
    else:
        out_refs[0][...] = x_new


def _combine_post_ffn(expert_out, dest, top_gate, x, gate, g_post, split_out):
    row = pl.BlockSpec((CB_TOK, D_MODEL), lambda i: (i, 0))
    dest3 = dest.reshape(CB_STEPS, 1, CB_ROWS)
    if split_out:
        out_specs = [pl.BlockSpec((CB_TOK, D_MODEL), lambda i: (jnp.minimum(i, CB_PROMPT_STEPS - 1), 0)),
                     pl.BlockSpec((CB_TOK, D_MODEL), lambda i: (jnp.maximum(i - CB_PROMPT_STEPS, 0), 0))]
        out_shape = [jax.ShapeDtypeStruct((N_PROMPT, D_MODEL), F32), jax.ShapeDtypeStruct((N_SAMPLE, D_MODEL), F32)]
    else:
        out_specs = row
        out_shape = jax.ShapeDtypeStruct((N_TOK, D_MODEL), F32)
    return pl.pallas_call(
        functools.partial(_combine_kernel, split_out=split_out),
        name="moe_combine",
        grid=(CB_STEPS,),
        in_specs=[
            pl.BlockSpec((None, 1, CB_ROWS), lambda i: (i, 0, 0), memory_space=pltpu.SMEM),
            pl.BlockSpec((None, 1, CB_ROWS), lambda i: (jnp.minimum(i + 1, CB_STEPS - 1), 0, 0),
                         memory_space=pltpu.SMEM),
            pl.BlockSpec(memory_space=pl.ANY),
            pl.BlockSpec((CB_TOK, LANES), lambda i: (i, 0)),
            row,
            pl.BlockSpec((None, 1, D_MODEL), lambda i: (i * CB_TOK // PB_TOK, 0, 0)),
            pl.BlockSpec((1, D_MODEL), lambda i: (0, 0)),
        ],
        out_specs=out_specs,
        out_shape=out_shape,
        scratch_shapes=[pltpu.VMEM((2, CB_ROWS, D_MODEL), F32), pltpu.SemaphoreType.DMA((2,))],
        compiler_params=_params(("arbitrary",), 40),
    )(dest3, dest3, expert_out, top_gate, x, gate, g_post.reshape(1, D_MODEL))


def _for_valid_rows(nsub, first, rest):
    pair = 2 * MOE_SUB

    @pl.when(nsub == 1)
    def _():
        first(0, MOE_SUB)

    @pl.when(nsub >= 2)
    def _():
        first(0, pair)

        def body(p, carry):
            rest(pl.multiple_of(p * pair, pair), pair)
            return carry

        lax.fori_loop(1, nsub // 2, body, 0)

        @pl.when(nsub % 2 == 1)
        def _():
            rest(pl.multiple_of((nsub - 1) * MOE_SUB, MOE_SUB), MOE_SUB)


GATHER_UNROLL = 8


def _moe_up_kernel(sbe_ref, sbi_ref, nsub_ref, tok_ref, tok_next_ref, h_hbm, wg_ref, wl_ref, bg_ref, bl_ref,
                   o_ref, xg, xb, wgb, wlb, sem):
    s = pl.program_id(0)
    c = pl.program_id(1)
    nsub = nsub_ref[s]

    def gather(idx_ref, n_sub_blocks):
        def body(j, carry):
            for u in range(GATHER_UNROLL):
                a = j * GATHER_UNROLL + u
                pltpu.make_async_copy(h_hbm.at[pl.ds(idx_ref[0, a], 1)], xg.at[pl.ds(a, 1)], sem).start()
            return carry
        lax.fori_loop(0, n_sub_blocks * (MOE_SUB // GATHER_UNROLL), body, 0)

    @pl.when((nsub > 0) & (c == 0))
    def _():
        @pl.when(s == 0)
        def _():
            gather(tok_ref, nsub)

        def wait_rows(r, carry):
            pltpu.make_async_copy(h_hbm.at[pl.ds(0, MOE_SUB)], xg.at[pl.ds(0, MOE_SUB)], sem).wait()
            return carry

        lax.fori_loop(0, nsub, wait_rows, 0)

        def unpack(r, carry):
            rows = pl.ds(pl.multiple_of(r * MOE_SUB, MOE_SUB), MOE_SUB)
            w = xg[rows, :]
            xb[rows, 0:HALF_D] = pltpu.bitcast(w << SHIFT16, F32).astype(BF16)
            xb[rows, HALF_D:D_MODEL] = pltpu.bitcast(w & HI_MASK, F32).astype(BF16)
            return carry

        lax.fori_loop(0, nsub, unpack, 0)

        s_next = jnp.minimum(s + 1, MOE_NSB - 1)
        n_next = jnp.where(s + 1 < MOE_NSB, nsub_ref[s_next], 0)

        @pl.when(n_next > 0)
        def _():
            gather(tok_next_ref, n_next)

    def swiglu_store(rows, xs, wg, wl):
        glu = jnp.dot(xs, wg, preferred_element_type=F32) + bg_ref[...]
        lin = jnp.dot(xs, wl, preferred_element_type=F32) + bl_ref[...]
        glu = jnp.minimum(glu, SWIGLU_LIMIT)
        lin = jnp.clip(lin, -SWIGLU_LIMIT, SWIGLU_LIMIT)
        act = glu / (1.0 + jnp.exp(-SWIGLU_ALPHA * glu)) * (lin + 1.0)
        o_ref[rows, :] = act.astype(BF16)

    def first(r0, n):
        wg = wg_ref[...].astype(BF16)
        wl = wl_ref[...].astype(BF16)
        wgb[...] = wg
        wlb[...] = wl
        rows = pl.ds(r0, n)
        swiglu_store(rows, xb[rows, :], wg, wl)

    def rest(r0, n):
        rows = pl.ds(r0, n)
        swiglu_store(rows, xb[rows, :], wgb[...], wlb[...])

    @pl.when(nsub > 0)
    def _():
        _for_valid_rows(nsub, first, rest)


def _moe_up(h_packed, row_tok, sbe, sbi, nsub, w_gate_up, b_gate_up, layer):
    n_c = MOE_D_FF // MOE_FC
    last_c = n_c - 1

    def c_eff(s, c, nsub_ref):
        return jnp.where(nsub_ref[s] > 0, c, last_c)

    grid_spec = pltpu.PrefetchScalarGridSpec(
        num_scalar_prefetch=3,
        grid=(MOE_NSB, n_c),
        in_specs=[
            pl.BlockSpec((None, 1, MOE_R), lambda s, c, e, i, n: (i[s], 0, 0), memory_space=pltpu.SMEM),
            pl.BlockSpec((None, 1, MOE_R), lambda s, c, e, i, n: (i[jnp.minimum(s + 1, MOE_NSB - 1)], 0, 0),
                         memory_space=pltpu.SMEM),
            pl.BlockSpec(memory_space=pl.ANY),
            pl.BlockSpec((None, None, D_MODEL, MOE_FC), lambda s, c, e, i, n: (layer, e[s], 0, c_eff(s, c, n))),
            pl.BlockSpec((None, None, D_MODEL, MOE_FC),
                         lambda s, c, e, i, n: (layer, e[s], 0, n_c + c_eff(s, c, n))),
            pl.BlockSpec((None, None, 1, MOE_FC), lambda s, c, e, i, n: (layer, e[s], 0, c_eff(s, c, n))),
            pl.BlockSpec((None, None, 1, MOE_FC), lambda s, c, e, i, n: (layer, e[s], 0, n_c + c_eff(s, c, n))),
        ],
        out_specs=pl.BlockSpec((MOE_R, MOE_FC), lambda s, c, e, i, n: (i[s], c_eff(s, c, n))),
        scratch_shapes=[pltpu.VMEM((MOE_R, HALF_D), U32), pltpu.VMEM((MOE_R, D_MODEL), BF16),
                        pltpu.VMEM((D_MODEL, MOE_FC), BF16), pltpu.VMEM((D_MODEL, MOE_FC), BF16),
                        pltpu.SemaphoreType.DMA],
    )
    bgu = b_gate_up.reshape(DEPTH, N_EXPERTS, 1, 2 * MOE_D_FF)
    tok3 = row_tok.reshape(MOE_NSB, 1, MOE_R)
    return pl.pallas_call(
        _moe_up_kernel,
        name="moe_up",
        grid_spec=grid_spec,
        out_shape=jax.ShapeDtypeStruct((MOE_CAP, MOE_D_FF), BF16),
        compiler_params=_params(("arbitrary", "arbitrary"), 58),
    )(sbe, sbi, nsub, tok3, tok3, h_packed, w_gate_up, w_gate_up, bgu, bgu)


def _moe_down_kernel(sbe_ref, sbi_ref, nsub_ref, h_ref, w_ref, b_ref, o_ref, wb):
    s = pl.program_id(0)
    nsub = nsub_ref[s]

    def first(r0, n):
        w = w_ref[...].astype(BF16)
        wb[...] = w
        rows = pl.ds(r0, n)
        o_ref[rows, :] = jnp.dot(h_ref[rows, :], w, preferred_element_type=F32) + b_ref[...]

    def rest(r0, n):
        rows = pl.ds(r0, n)
        o_ref[rows, :] = jnp.dot(h_ref[rows, :], wb[...], preferred_element_type=F32) + b_ref[...]

    @pl.when(nsub > 0)
    def _():
        _for_valid_rows(nsub, first, rest)


def _moe_down(h_sorted, sbe, sbi, nsub, w_down, b_down, layer):
    n_c = D_MODEL // MOE_NC
    last_c = n_c - 1

    def c_eff(s, c, nsub_ref):
        return jnp.where(nsub_ref[s] > 0, c, last_c)

    grid_spec = pltpu.PrefetchScalarGridSpec(
        num_scalar_prefetch=3,
        grid=(MOE_NSB, n_c),
        in_specs=[
            pl.BlockSpec((MOE_R, MOE_D_FF), lambda s, c, e, i, n: (i[s], 0)),
            pl.BlockSpec((None, None, MOE_D_FF, MOE_NC), lambda s, c, e, i, n: (layer, e[s], 0, c_eff(s, c, n))),
            pl.BlockSpec((None, None, 1, MOE_NC), lambda s, c, e, i, n: (layer, e[s], 0, c_eff(s, c, n))),
        ],
        out_specs=pl.BlockSpec((MOE_R, MOE_NC), lambda s, c, e, i, n: (i[s], c_eff(s, c, n))),
        scratch_shapes=[pltpu.VMEM((MOE_D_FF, MOE_NC), BF16)],
    )
    return pl.pallas_call(
        _moe_down_kernel,
        name="moe_down",
        grid_spec=grid_spec,
        out_shape=jax.ShapeDtypeStruct((MOE_CAP, D_MODEL), F32),
        compiler_params=_params(("arbitrary", "arbitrary"), 48),
    )(sbe, sbi, nsub, h_sorted, w_down, b_down.reshape(DEPTH, N_EXPERTS, 1, D_MODEL))


def _route(top_idx):
    flat_e = top_idx.reshape(-1)
    onehot = (flat_e[:, None] == jnp.arange(N_EXPERTS, dtype=I32)[None, :]).astype(I32)
    csum = jnp.cumsum(onehot, axis=0)
    rank = jnp.take_along_axis(csum, flat_e[:, None], axis=1)[:, 0] - 1
    counts = csum[-1]
    nsb = (counts + MOE_R - 1) // MOE_R
    sb_end = jnp.cumsum(nsb)
    sb_off = sb_end - nsb
    dest = sb_off[flat_e] * MOE_R + rank
    n_real = sb_end[-1]
    s = jnp.arange(MOE_NSB, dtype=I32)
    s_eff = jnp.minimum(s, n_real - 1)
    sbe = jnp.minimum(jnp.searchsorted(sb_end, s_eff, side='right'), N_EXPERTS - 1).astype(I32)
    valid = jnp.clip(counts[sbe] - (s_eff - sb_off[sbe]) * MOE_R, 0, MOE_R)
    valid = jnp.where(s < n_real, valid, 0)
    nsub = ((valid + MOE_SUB - 1) // MOE_SUB).astype(I32)
    dest = dest.astype(I32)
    row_tok = jnp.zeros((MOE_CAP,), I32).at[dest].set(jnp.arange(N_ASSIGN, dtype=I32) // TOP_K)
    return dest, row_tok, sbe, s_eff.astype(I32), nsub


def _moe_experts(h_packed, top_idx, w_gate_up, b_gate_up, w_down, b_down, layer):
    dest, row_tok, sbe, sbi, nsub = _route(top_idx)
    h_sorted = _moe_up(h_packed, row_tok, sbe, sbi, nsub, w_gate_up, b_gate_up, layer)
    return _moe_down(h_sorted, sbe, sbi, nsub, w_down, b_down, layer), dest


def _pb_rows(mod_piece):
    idx = jnp.array([0] * N_PROMPT_PB + list(range(1, 1 + DEC_BATCH)), I32)
    return mod_piece[idx][:, None, :]


def kernel(x_prompt, x_sample, cache_a_k, cache_a_v, cache_b_k, cache_b_v, cache_c_k, cache_c_v, c, c_ctx, w_ada, b_ada, g_pre_mix, g_post_mix, g_pre_ffn, g_post_ffn, w_in_even, w_out_even, rpb_a, q_norm_b, k_norm_b, w_in_odd, w_out_odd, sink_c, w_router, b_router, w_gate_up, b_gate_up, w_down, b_down):
    x = jnp.concatenate([x_prompt.reshape(N_PROMPT, D_MODEL), x_sample.reshape(N_SAMPLE, D_MODEL)], axis=0)
    cond8 = jnp.concatenate([c_ctx[None, :], c, jnp.zeros((8 - 1 - DEC_BATCH, D_MODEL), F32)], axis=0)
    mods = _ada_mod(cond8, w_ada, b_ada)
    cos_tab, sin_tab = _rope_tables()
    no_sink = jnp.full((max(A_HEADS, B_HEADS),), -jnp.inf, F32)
    states = {}

    for l in range(DEPTH):
        j = l // 2
        m = [_pb_rows(mods[l, :, k * D_MODEL:(k + 1) * D_MODEL]) for k in range(6)]
        if l % 2 == 0:
            w_in = w_in_even[j].astype(BF16)
            w_out = w_out_even[j].astype(BF16)
            nb = EVEN_IN // IN_BN
            qb0 = 3 * A_HEADS * HEAD_DIM // IN_BN
            kb0 = qb0 + B_HEADS * HEAD_DIM // IN_BN
            vb0 = kb0 + B_KV_HEADS * HEAD_DIM // IN_BN
            modes = jnp.array([MODE_PLAIN] * qb0 + [MODE_NORM_Q] * (kb0 - qb0) + [MODE_NORM_K] * (vb0 - kb0)
                              + [MODE_PLAIN] * (nb - vb0), I32)
            gains = jnp.concatenate([q_norm_b[j][None], k_norm_b[j][None], jnp.ones((6, HEAD_DIM), F32)], axis=0)
        else:
            w_in = w_in_odd[j].astype(BF16)
            w_out = w_out_odd[j].astype(BF16)
            nb = ODD_IN // IN_BN
            vc0 = (C_HEADS + C_KV_HEADS) * HEAD_DIM // IN_BN
            modes = jnp.array([MODE_ROPE] * vc0 + [MODE_PLAIN] * (nb - vc0), I32)
            gains = jnp.ones((8, HEAD_DIM), F32)

        proj = _in_proj(x, g_pre_mix[l], m[0], m[1], w_in, modes, gains, cos_tab, sin_tab)

        if l % 2 == 0:
            ah = A_HEADS * HEAD_DIM
            ctx_a = _ctx_attention(proj, no_sink, 0, ah, 2 * ah, A_HEADS, A_HEADS, 8)
            ctx_b = _ctx_attention(proj, no_sink, 3 * ah, 3 * ah + B_HEADS * HEAD_DIM,
                                   3 * ah + (B_HEADS + B_KV_HEADS) * HEAD_DIM, B_HEADS, B_KV_HEADS, B_KV_HEADS)
            lat_a = _na_attention(proj, cache_a_k, cache_a_v, rpb_a[j], j)
            lat_b = _dense_attention(proj, cache_b_k, cache_b_v, j)
            prompt_parts, latent_parts = [ctx_a, ctx_b], [lat_a, lat_b]
            pp = proj[:N_PROMPT]
            kb_c = 3 * ah + B_HEADS * HEAD_DIM
            vb_c = kb_c + B_KV_HEADS * HEAD_DIM
            states.setdefault('ak', []).append(pp[:, ah:2 * ah].reshape(BATCH, SEQ, A_HEADS, HEAD_DIM))
            states.setdefault('av', []).append(pp[:, 2 * ah:3 * ah].reshape(BATCH, SEQ, A_HEADS, HEAD_DIM))
            states.setdefault('bk', []).append(pp[:, kb_c:vb_c].reshape(BATCH, SEQ, B_KV_HEADS, HEAD_DIM))
            states.setdefault('bv', []).append(pp[:, vb_c:].reshape(BATCH, SEQ, B_KV_HEADS, HEAD_DIM))
        else:
            qc = C_HEADS * HEAD_DIM
            kvw = C_KV_HEADS * HEAD_DIM
            ctx_c = _ctx_attention(proj, sink_c[j], 0, qc, qc + kvw, C_HEADS, C_KV_HEADS, C_KV_HEADS)
            lat_c = _window_attention(proj, cache_c_k, cache_c_v, sink_c[j], j)
            prompt_parts, latent_parts = [ctx_c], [lat_c]
            pp = proj[:N_PROMPT]
            states.setdefault('ck', []).append(pp[:, qc:qc + kvw].reshape(BATCH, SEQ, C_KV_HEADS, HEAD_DIM))
            states.setdefault('cv', []).append(pp[:, qc + kvw:].reshape(BATCH, SEQ, C_KV_HEADS, HEAD_DIM))

        y = _out_proj(prompt_parts, latent_parts, w_out)
        wr = jnp.pad(w_router[l], ((0, 0), (0, LANES - N_EXPERTS)))
        wr_hi = wr.astype(BF16)
        wr_lo = (wr - wr_hi.astype(F32)).astype(BF16)
        br = jnp.concatenate([b_router[l], jnp.full((LANES - N_EXPERTS,), NEG_BIG, F32)])[None, :]
        x, h_ffn, top_i, top_g = _post_mix(y, x, m[2], g_post_mix[l], g_pre_ffn[l], m[3], m[4], wr_hi, wr_lo, br)
        expert_out, dest = _moe_experts(h_ffn, top_i[:, :TOP_K], w_gate_up, b_gate_up, w_down, b_down, l)
        x = _combine_post_ffn(expert_out, dest, top_g, x, m[5], g_post_ffn[l], split_out=(l == DEPTH - 1))

    y_prompt = x[0].reshape(BATCH, SEQ, D_MODEL)
    y_sample = x[1].reshape(DEC_BATCH, DEC_SEQ, D_MODEL)
    st = {k: jnp.stack(v, axis=1) for k, v in states.items()}
    return (y_prompt, y_sample, st['ak'], st['av'], st['bk'], st['bv'], st['ck'], st['cv'])
```

```python
import functools

import jax
import jax.numpy as jnp
import numpy as np
from jax import lax
from jax.experimental import pallas as pl
from jax.experimental.pallas import tpu as pltpu

F32 = jnp.float32
BF16 = jnp.bfloat16
I32 = jnp.int32
U32 = jnp.uint32

D_MODEL = 4096
BATCH = 16
SEQ = 256
DEPTH = 2
DEC_BATCH = 4
DEC_SEQ = 2048
PAST_LEN = 512
GRID_W = 64
GRID_ROWS = DEC_SEQ // GRID_W
HEAD_DIM = 128
A_HEADS = 16
B_HEADS = 16
B_KV_HEADS = 4
C_HEADS = 32
C_KV_HEADS = 4
WIN_H = 8
WIN_W = 16
C_WINDOW = 128
N_EXPERTS = 32
TOP_K = 4
MOE_D_FF = 2048
SWIGLU_LIMIT = 7.0
SWIGLU_ALPHA = 1.702
ROPE_THETA = 10000.0
RMS_EPS = 1e-6
ATTN_SCALE = HEAD_DIM ** -0.5
EVEN_IN = (3 * A_HEADS + B_HEADS + 2 * B_KV_HEADS) * HEAD_DIM
ODD_IN = (C_HEADS + 2 * C_KV_HEADS) * HEAD_DIM

N_PROMPT = BATCH * SEQ
N_SAMPLE = DEC_BATCH * DEC_SEQ
N_TOK = N_PROMPT + N_SAMPLE
PB_TOK = DEC_SEQ
N_PB = N_TOK // PB_TOK
N_PROMPT_PB = N_PROMPT // PB_TOK

LANES = 128
MIB = 1024 * 1024

MOE_R = 2048
MOE_SUB = 256
N_ASSIGN = N_TOK * TOP_K
MOE_NSB = N_ASSIGN // MOE_R + N_EXPERTS
MOE_CAP = MOE_NSB * MOE_R
MOE_FC = 256
MOE_NC = 512

NEG_BIG = -1e30
HALF_D = D_MODEL // 2
HI_MASK = np.uint32(0xFFFF0000)
SHIFT16 = np.uint32(16)


def _params(sem, vmem_mib):
    return pltpu.CompilerParams(dimension_semantics=sem, vmem_limit_bytes=vmem_mib * MIB)


def _ada_kernel(c_ref, w_ref, b_ref, o_ref):
    c = c_ref[...]
    s = (c / (1.0 + jnp.exp(-c))).astype(BF16)
    o_ref[...] = jnp.dot(s, w_ref[...].astype(BF16), preferred_element_type=F32) + b_ref[...]


def _ada_mod(cond8, w_ada, b_ada):
    bn = 512
    n_out = 6 * D_MODEL
    return pl.pallas_call(
        _ada_kernel,
        name="ada_mod",
        grid=(DEPTH, n_out // bn),
        in_specs=[
            pl.BlockSpec((8, D_MODEL), lambda l, j: (0, 0)),
            pl.BlockSpec((None, D_MODEL, bn), lambda l, j: (l, 0, j)),
            pl.BlockSpec((None, 1, bn), lambda l, j: (l, 0, j)),
        ],
        out_specs=pl.BlockSpec((None, 8, bn), lambda l, j: (l, 0, j)),
        out_shape=jax.ShapeDtypeStruct((DEPTH, 8, n_out), F32),
        compiler_params=_params(("arbitrary", "arbitrary"), 40),
    )(cond8, w_ada, b_ada.reshape(DEPTH, 1, n_out))


IN_BM = 512
IN_BN = 512
MODE_PLAIN, MODE_NORM_Q, MODE_NORM_K, MODE_ROPE = 0, 1, 2, 3


def _inproj_kernel(mode_ref, x_ref, g_ref, sh_ref, sc_ref, w_ref, gain_ref, cos_ref, sin_ref,
                   o_ref, h_ref):
    j = pl.program_id(1)

    @pl.when(j == 0)
    def _():
        x = x_ref[...]
        r = lax.rsqrt(jnp.mean(x * x, axis=-1, keepdims=True) + RMS_EPS)
        h = (x * r * g_ref[...]) * (1.0 + sc_ref[...]) + sh_ref[...]
        h_ref[...] = h.astype(BF16)

    acc = jnp.dot(h_ref[...], w_ref[...], preferred_element_type=F32)
    mode = mode_ref[j]

    @pl.when(mode == MODE_PLAIN)
    def _():
        o_ref[...] = acc

    def rope_store(hh, y):
        lane = lax.broadcasted_iota(I32, (IN_BM, LANES), 1)
        sw = jnp.where((lane % 64) < 32, pltpu.roll(y, 96, 1), pltpu.roll(y, 32, 1))
        o_ref[:, hh * HEAD_DIM:(hh + 1) * HEAD_DIM] = y * cos_ref[...] + sw * sin_ref[...]

    @pl.when(mode == MODE_ROPE)
    def _():
        for hh in range(IN_BN // HEAD_DIM):
            rope_store(hh, acc[:, hh * HEAD_DIM:(hh + 1) * HEAD_DIM])

    @pl.when((mode == MODE_NORM_Q) | (mode == MODE_NORM_K))
    def _():
        gain = jnp.where(mode == MODE_NORM_Q, gain_ref[0:1, :], gain_ref[1:2, :])
        for hh in range(IN_BN // HEAD_DIM):
            xh = acc[:, hh * HEAD_DIM:(hh + 1) * HEAD_DIM]
            rs = lax.rsqrt(jnp.mean(xh * xh, axis=-1, keepdims=True) + RMS_EPS)
            rope_store(hh, xh * rs * gain)


def _in_proj(x, g_pre, shift, scale, w_bf16, modes, gains, cos_tab, sin_tab):
    n = w_bf16.shape[1]
    n_i = N_TOK // IN_BM
    n_prompt_blocks = N_PROMPT // IN_BM
    blocks_per_seq = DEC_SEQ // IN_BM

    def tab_idx(i, j, m):
        return (jnp.where(i < n_prompt_blocks, 0, 1 + (i - n_prompt_blocks) % blocks_per_seq), 0)

    grid_spec = pltpu.PrefetchScalarGridSpec(
        num_scalar_prefetch=1,
        grid=(n_i, n // IN_BN),
        in_specs=[
            pl.BlockSpec((IN_BM, D_MODEL), lambda i, j, m: (i, 0)),
            pl.BlockSpec((1, D_MODEL), lambda i, j, m: (0, 0)),
            pl.BlockSpec((None, 1, D_MODEL), lambda i, j, m: (i * IN_BM // PB_TOK, 0, 0)),
            pl.BlockSpec((None, 1, D_MODEL), lambda i, j, m: (i * IN_BM // PB_TOK, 0, 0)),
            pl.BlockSpec((D_MODEL, IN_BN), lambda i, j, m: (0, j)),
            pl.BlockSpec((8, LANES), lambda i, j, m: (0, 0)),
            pl.BlockSpec((IN_BM, LANES), tab_idx),
            pl.BlockSpec((IN_BM, LANES), tab_idx),
        ],
        out_specs=pl.BlockSpec((IN_BM, IN_BN), lambda i, j, m: (i, j)),
        scratch_shapes=[pltpu.VMEM((IN_BM, D_MODEL), BF16)],
    )
    return pl.pallas_call(
        _inproj_kernel,
        name="in_proj",
        grid_spec=grid_spec,
        out_shape=jax.ShapeDtypeStruct((N_TOK, n), F32),
        compiler_params=_params(("arbitrary", "arbitrary"), 48),
    )(modes, x, g_pre.reshape(1, D_MODEL), shift, scale, w_bf16, gains, cos_tab, sin_tab)


def _rope_tables():
    t = jnp.arange(DEC_SEQ)
    row = (t // GRID_W).astype(F32)
    col = (t % GRID_W).astype(F32)
    n_freq = HEAD_DIM // 4
    inv = ROPE_THETA ** (-jnp.arange(n_freq, dtype=F32) / n_freq)
    ar = row[:, None] * inv
    ac = col[:, None] * inv
    cos = jnp.concatenate([jnp.cos(ar), jnp.cos(ar), jnp.cos(ac), jnp.cos(ac)], axis=-1)
    sin = jnp.concatenate([-jnp.sin(ar), jnp.sin(ar), -jnp.sin(ac), jnp.sin(ac)], axis=-1)
    cos = jnp.concatenate([jnp.ones((IN_BM, HEAD_DIM), F32), cos], axis=0)
    sin = jnp.concatenate([jnp.zeros((IN_BM, HEAD_DIM), F32), sin], axis=0)
    return cos, sin


MM_BM = 1024
MM_BN = 512


MM_PROMPT_BLOCKS = N_PROMPT // MM_BM


def _out_proj_kernel(*refs, n_parts):
    prompt_refs = refs[:n_parts]
    latent_refs = refs[n_parts:2 * n_parts]
    w_ref, o_ref = refs[2 * n_parts:]
    i = pl.program_id(0)

    def run(parts):
        acc = None
        k0 = 0
        for r in parts:
            kk = r.shape[1]
            d = jnp.dot(r[...], w_ref[k0:k0 + kk, :], preferred_element_type=F32)
            acc = d if acc is None else acc + d
            k0 += kk
        o_ref[...] = acc

    @pl.when(i < MM_PROMPT_BLOCKS)
    def _():
        run(prompt_refs)

    @pl.when(i >= MM_PROMPT_BLOCKS)
    def _():
        run(latent_refs)


def _out_proj(prompt_parts, latent_parts, w_bf16):
    k, n = w_bf16.shape
    n_parts = len(prompt_parts)
    last_prompt = MM_PROMPT_BLOCKS - 1
    in_specs = ([pl.BlockSpec((MM_BM, p.shape[1]), lambda i, j: (jnp.minimum(i, last_prompt), 0))
                 for p in prompt_parts]
                + [pl.BlockSpec((MM_BM, p.shape[1]), lambda i, j: (jnp.maximum(i - MM_PROMPT_BLOCKS, 0), 0))
                   for p in latent_parts]
                + [pl.BlockSpec((k, MM_BN), lambda i, j: (0, j))])
    return pl.pallas_call(
        functools.partial(_out_proj_kernel, n_parts=n_parts),
        name="out_proj",
        grid=(N_TOK // MM_BM, n // MM_BN),
        in_specs=in_specs,
        out_specs=pl.BlockSpec((MM_BM, MM_BN), lambda i, j: (i, j)),
        out_shape=jax.ShapeDtypeStruct((N_TOK, n), F32),
        compiler_params=_params(("arbitrary", "arbitrary"), 56),
    )(*prompt_parts, *latent_parts, w_bf16)


def _attend_heads(scores, values, sinks=None):
    n = len(scores)
    ms = []
    for h in range(n):
        m = jnp.max(scores[h][0], axis=-1, keepdims=True)
        for s in scores[h][1:]:
            m = jnp.maximum(m, jnp.max(s, axis=-1, keepdims=True))
        if sinks is not None:
            m = jnp.maximum(m, sinks[h])
        ms.append(m)
    ps = [[jnp.exp(s - ms[h]) for s in scores[h]] for h in range(n)]
    dens = []
    for h in range(n):
        d = jnp.sum(ps[h][0], axis=-1, keepdims=True)
        for p in ps[h][1:]:
            d = d + jnp.sum(p, axis=-1, keepdims=True)
        if sinks is not None:
            d = d + jnp.exp(sinks[h] - ms[h])
        dens.append(d)
    outs = []
    for h in range(n):
        o = jnp.dot(ps[h][0].astype(BF16), values[h][0], preferred_element_type=F32)
        for p, v in zip(ps[h][1:], values[h][1:]):
            o = o + jnp.dot(p.astype(BF16), v, preferred_element_type=F32)
        outs.append(o / dens[h])
    return outs


def _qk(q, k):
    return lax.dot_general(q, k, (((1,), (1,)), ((), ())), preferred_element_type=F32) * ATTN_SCALE


def _ctx_attn_kernel(sink_ref, q_ref, k_ref, v_ref, o_ref, *, nkv, g):
    kb = pl.program_id(1)
    for kv in range(nkv):
        k = k_ref[:, kv * HEAD_DIM:(kv + 1) * HEAD_DIM].astype(BF16)
        v = v_ref[:, kv * HEAD_DIM:(kv + 1) * HEAD_DIM].astype(BF16)
        cols = [slice((kv * g + gi) * HEAD_DIM, (kv * g + gi + 1) * HEAD_DIM) for gi in range(g)]
        scores = [[_qk(q_ref[:, cs].astype(BF16), k)] for cs in cols]
        sinks = [sink_ref[(kb * nkv + kv) * g + gi] for gi in range(g)]
        for cs, o in zip(cols, _attend_heads(scores, [[v]] * g, sinks)):
            o_ref[:, cs] = o.astype(o_ref.dtype)


def _ctx_attention(proj, sink, q_col, k_col, v_col, hq, hkv, nkv):
    g = hq // hkv
    qw = nkv * g * HEAD_DIM
    kw = nkv * HEAD_DIM
    return pl.pallas_call(
        functools.partial(_ctx_attn_kernel, nkv=nkv, g=g),
        name="ctx_attn",
        grid=(BATCH, hkv // nkv),
        in_specs=[
            pl.BlockSpec(memory_space=pltpu.SMEM),
            pl.BlockSpec((SEQ, qw), lambda b, h: (b, q_col // qw + h)),
            pl.BlockSpec((SEQ, kw), lambda b, h: (b, k_col // kw + h)),
            pl.BlockSpec((SEQ, kw), lambda b, h: (b, v_col // kw + h)),
        ],
        out_specs=pl.BlockSpec((SEQ, qw), lambda b, h: (b, h)),
        out_shape=jax.ShapeDtypeStruct((N_PROMPT, hq * HEAD_DIM), BF16),
        compiler_params=_params(("arbitrary", "arbitrary"), 40),
    )(sink, proj, proj, proj)


DENSE_BQ = 256


def _cache_bf16(step, srcs, dsts):
    @pl.when(step == 0)
    def _():
        for src, dst in zip(srcs, dsts):
            dst[...] = src[...].astype(BF16)


def _kv_scratch(width):
    return [pltpu.VMEM((DEC_SEQ, width), BF16), pltpu.VMEM((DEC_SEQ, width), BF16),
            pltpu.VMEM((PAST_LEN, width), BF16), pltpu.VMEM((PAST_LEN, width), BF16)]


def _dense_attn_kernel(q_ref, k_ref, v_ref, kc_ref, vc_ref, o_ref, kb, vb, kcb, vcb, *, g):
    _cache_bf16(pl.program_id(2), (k_ref, v_ref, kc_ref, vc_ref), (kb, vb, kcb, vcb))
    k = kb[...]
    v = vb[...]
    kc = kcb[...]
    vc = vcb[...]
    cols = [slice(gi * HEAD_DIM, (gi + 1) * HEAD_DIM) for gi in range(g)]
    qs = [q_ref[:, cs].astype(BF16) for cs in cols]
    scores = [[_qk(q, k), _qk(q, kc)] for q in qs]
    for cs, o in zip(cols, _attend_heads(scores, [[v, vc]] * g)):
        o_ref[:, cs] = o.astype(o_ref.dtype)


def _dense_attention(proj, cache_k, cache_v, layer_j):
    g = B_HEADS // B_KV_HEADS
    qw = g * HEAD_DIM
    nq = DEC_SEQ // DENSE_BQ
    q_col = 3 * A_HEADS * HEAD_DIM
    k_col = q_col + B_HEADS * HEAD_DIM
    v_col = k_col + B_KV_HEADS * HEAD_DIM
    row0 = N_PROMPT // DENSE_BQ
    ck = cache_k.reshape(DEC_BATCH, -1, PAST_LEN, B_KV_HEADS * HEAD_DIM)
    cv = cache_v.reshape(DEC_BATCH, -1, PAST_LEN, B_KV_HEADS * HEAD_DIM)
    return pl.pallas_call(
        functools.partial(_dense_attn_kernel, g=g),
        name="dense_attn",
        grid=(DEC_BATCH, B_KV_HEADS, nq),
        in_specs=[
            pl.BlockSpec((DENSE_BQ, qw), lambda b, h, i: (row0 + b * nq + i, q_col // qw + h)),
            pl.BlockSpec((DEC_SEQ, HEAD_DIM), lambda b, h, i: (N_PROMPT_PB + b, k_col // HEAD_DIM + h)),
            pl.BlockSpec((DEC_SEQ, HEAD_DIM), lambda b, h, i: (N_PROMPT_PB + b, v_col // HEAD_DIM + h)),
            pl.BlockSpec((None, None, PAST_LEN, HEAD_DIM), lambda b, h, i: (b, layer_j, 0, h)),
            pl.BlockSpec((None, None, PAST_LEN, HEAD_DIM), lambda b, h, i: (b, layer_j, 0, h)),
        ],
        out_specs=pl.BlockSpec((DENSE_BQ, qw), lambda b, h, i: (b * nq + i, h)),
        scratch_shapes=_kv_scratch(HEAD_DIM),
        out_shape=jax.ShapeDtypeStruct((N_SAMPLE, B_HEADS * HEAD_DIM), BF16),
        compiler_params=_params(("arbitrary", "arbitrary", "arbitrary"), 48),
    )(proj, proj, proj, ck, cv)


WIN_BQ = 256
WIN_KEYS = WIN_BQ + 2 * C_WINDOW


def _window_attn_kernel(sink_ref, q_ref, k_ref, v_ref, kc_ref, vc_ref, o_ref, kb, vb, kcb, vcb, *, g):
    h = pl.program_id(1)
    i = pl.program_id(2)
    _cache_bf16(i, (k_ref, v_ref, kc_ref, vc_ref), (kb, vb, kcb, vcb))
    q0 = i * WIN_BQ
    k0 = pl.multiple_of(jnp.clip(q0 - C_WINDOW, 0, DEC_SEQ - WIN_KEYS), C_WINDOW)
    k = kb[pl.ds(k0, WIN_KEYS), :]
    v = vb[pl.ds(k0, WIN_KEYS), :]
    kc = kcb[...]
    vc = vcb[...]
    q_pos = q0 + lax.broadcasted_iota(I32, (WIN_BQ, WIN_KEYS), 0)
    k_pos = k0 + lax.broadcasted_iota(I32, (WIN_BQ, WIN_KEYS), 1)
    ok = jnp.abs(q_pos - k_pos) <= C_WINDOW
    cols = [slice(gi * HEAD_DIM, (gi + 1) * HEAD_DIM) for gi in range(g)]
    qs = [q_ref[:, cs].astype(BF16) for cs in cols]
    scores = [[jnp.where(ok, _qk(q, k), -jnp.inf), _qk(q, kc)] for q in qs]
    sinks = [sink_ref[h * g + gi] for gi in range(g)]
    for cs, o in zip(cols, _attend_heads(scores, [[v, vc]] * g, sinks)):
        o_ref[:, cs] = o.astype(o_ref.dtype)


def _window_attention(proj, cache_k, cache_v, sink, layer_j):
    g = C_HEADS // C_KV_HEADS
    qw = g * HEAD_DIM
    nq = DEC_SEQ // WIN_BQ
    k_col = C_HEADS * HEAD_DIM
    v_col = k_col + C_KV_HEADS * HEAD_DIM
    row0 = N_PROMPT // WIN_BQ
    ck = cache_k.reshape(DEC_BATCH, -1, PAST_LEN, C_KV_HEADS * HEAD_DIM)
    cv = cache_v.reshape(DEC_BATCH, -1, PAST_LEN, C_KV_HEADS * HEAD_DIM)
    return pl.pallas_call(
        functools.partial(_window_attn_kernel, g=g),
        name="window_attn",
        grid=(DEC_BATCH, C_KV_HEADS, nq),
        in_specs=[
            pl.BlockSpec(memory_space=pltpu.SMEM),
            pl.BlockSpec((WIN_BQ, qw), lambda b, h, i: (row0 + b * nq + i, h)),
            pl.BlockSpec((DEC_SEQ, HEAD_DIM), lambda b, h, i: (N_PROMPT_PB + b, k_col // HEAD_DIM + h)),
            pl.BlockSpec((DEC_SEQ, HEAD_DIM), lambda b, h, i: (N_PROMPT_PB + b, v_col // HEAD_DIM + h)),
            pl.BlockSpec((None, None, PAST_LEN, HEAD_DIM), lambda b, h, i: (b, layer_j, 0, h)),
            pl.BlockSpec((None, None, PAST_LEN, HEAD_DIM), lambda b, h, i: (b, layer_j, 0, h)),
        ],
        out_specs=pl.BlockSpec((WIN_BQ, qw), lambda b, h, i: (b * nq + i, h)),
        scratch_shapes=_kv_scratch(HEAD_DIM),
        out_shape=jax.ShapeDtypeStruct((N_SAMPLE, C_HEADS * HEAD_DIM), BF16),
        compiler_params=_params(("arbitrary", "arbitrary", "arbitrary"), 48),
    )(sink, proj, proj, proj, ck, cv)


NA_HC = 4
NA_KEYS = WIN_H * GRID_W


def _na_row_start(i):
    return jnp.clip(i - WIN_H // 2, 0, GRID_ROWS - WIN_H)


def _na_attn_kernel(q_ref, k_ref, v_ref, kc_ref, vc_ref, bias_ref, o_ref, kb, vb, kcb, vcb):
    i = pl.program_id(2)
    _cache_bf16(i, (k_ref, v_ref, kc_ref, vc_ref), (kb, vb, kcb, vcb))
    r0 = pl.multiple_of(_na_row_start(i) * GRID_W, GRID_W)
    rows = pl.ds(r0, NA_KEYS)
    heads = [slice(hh * HEAD_DIM, (hh + 1) * HEAD_DIM) for hh in range(NA_HC)]
    qs = [q_ref[:, cs].astype(BF16) for cs in heads]
    scores = [[_qk(q, kb[rows, cs]) + bias_ref[hh], _qk(q, kcb[:, cs])] for hh, (q, cs) in enumerate(zip(qs, heads))]
    values = [[vb[rows, cs], vcb[:, cs]] for cs in heads]
    for cs, o in zip(heads, _attend_heads(scores, values)):
        o_ref[:, cs] = o.astype(o_ref.dtype)


def _na_bias_table(rpb):
    qc = jnp.arange(GRID_W)[:, None]
    kc = jnp.arange(GRID_W)[None, :]
    ws = jnp.clip(qc - WIN_W // 2, 0, GRID_W - WIN_W)
    ok = (kc >= ws) & (kc < ws + WIN_W)
    dc = jnp.clip(kc - qc + WIN_W - 1, 0, 2 * WIN_W - 2)
    t = rpb[:, :, dc]
    t = jnp.where(ok[None, None], t, -jnp.inf)
    dr = jnp.arange(WIN_H)[:, None] + jnp.arange(WIN_H)[None, :]
    b = t[:, dr]
    b = b.transpose(0, 1, 3, 2, 4).reshape(A_HEADS, WIN_H, GRID_W, NA_KEYS)
    return b.astype(F32)


def _na_attention(proj, cache_k, cache_v, rpb, layer_j):
    cw = NA_HC * HEAD_DIM
    k_col = A_HEADS * HEAD_DIM
    v_col = 2 * A_HEADS * HEAD_DIM
    row0 = N_PROMPT // GRID_W
    ck = cache_k.reshape(DEC_BATCH, -1, PAST_LEN, A_HEADS * HEAD_DIM)
    cv = cache_v.reshape(DEC_BATCH, -1, PAST_LEN, A_HEADS * HEAD_DIM)
    bias = _na_bias_table(rpb)

    def bias_idx(b, h, i):
        return (h, _na_row_start(i) - i + WIN_H - 1, 0, 0)

    return pl.pallas_call(
        _na_attn_kernel,
        name="na_attn",
        grid=(DEC_BATCH, A_HEADS // NA_HC, GRID_ROWS),
        in_specs=[
            pl.BlockSpec((GRID_W, cw), lambda b, h, i: (row0 + b * GRID_ROWS + i, h)),
            pl.BlockSpec((DEC_SEQ, cw), lambda b, h, i: (N_PROMPT_PB + b, k_col // cw + h)),
            pl.BlockSpec((DEC_SEQ, cw), lambda b, h, i: (N_PROMPT_PB + b, v_col // cw + h)),
            pl.BlockSpec((None, None, PAST_LEN, cw), lambda b, h, i: (b, layer_j, 0, h)),
            pl.BlockSpec((None, None, PAST_LEN, cw), lambda b, h, i: (b, layer_j, 0, h)),
            pl.BlockSpec((NA_HC, None, GRID_W, NA_KEYS), bias_idx),
        ],
        out_specs=pl.BlockSpec((GRID_W, cw), lambda b, h, i: (b * GRID_ROWS + i, h)),
        scratch_shapes=_kv_scratch(cw),
        out_shape=jax.ShapeDtypeStruct((N_SAMPLE, A_HEADS * HEAD_DIM), BF16),
        compiler_params=_params(("arbitrary", "arbitrary", "arbitrary"), 48),
    )(proj, proj, proj, ck, cv, bias)


EP_BM = 256


def _rms(x):
    return x * lax.rsqrt(jnp.mean(x * x, axis=-1, keepdims=True) + RMS_EPS)


def _post_mix_kernel(y_ref, x_ref, gate_ref, gpost_ref, gpre_ref, sh_ref, sc_ref,
                     wr_hi_ref, wr_lo_ref, br_ref, xo_ref, h_ref, ti_ref, tg_ref):
    x_new = x_ref[...] + gate_ref[...] * (_rms(y_ref[...]) * gpost_ref[...])
    xo_ref[...] = x_new
    h = (_rms(x_new) * gpre_ref[...]) * (1.0 + sc_ref[...]) + sh_ref[...]
    h_hi = h.astype(BF16)
    h_hi32 = h_hi.astype(F32)
    bits = pltpu.bitcast(h_hi32, U32)
    h_ref[...] = (bits[:, :HALF_D] >> SHIFT16) | (bits[:, HALF_D:] & HI_MASK)
    h_lo = (h - h_hi32).astype(BF16)
    w_hi = wr_hi_ref[...]
    logits = (jnp.dot(h_hi, w_hi, preferred_element_type=F32)
              + jnp.dot(h_lo, w_hi, preferred_element_type=F32)
              + jnp.dot(h_hi, wr_lo_ref[...], preferred_element_type=F32)) + br_ref[...]
    lane = lax.broadcasted_iota(I32, (EP_BM, LANES), 1).astype(F32)
    vals, idxs = [], []
    l = logits
    for _ in range(TOP_K):
        m = jnp.max(l, axis=-1, keepdims=True)
        idx = jnp.min(jnp.where(l == m, lane, float(LANES)), axis=-1, keepdims=True)
        vals.append(m)
        idxs.append(idx)
        l = jnp.where(lane == idx, -jnp.inf, l)
    es = [jnp.exp(v - vals[0]) for v in vals]
    den = es[0] + es[1] + es[2] + es[3]
    ti = jnp.zeros((EP_BM, LANES), F32)
    tg = jnp.zeros((EP_BM, LANES), F32)
    for k in range(TOP_K):
        ti = jnp.where(lane == float(k), idxs[k], ti)
        tg = jnp.where(lane == float(k), es[k] / den, tg)
    ti_ref[...] = ti.astype(I32)
    tg_ref[...] = tg


def _post_mix(y, x, gate, g_post, g_pre, shift, scale, wr_hi, wr_lo, br):
    row = pl.BlockSpec((EP_BM, D_MODEL), lambda i: (i, 0))
    vec = pl.BlockSpec((1, D_MODEL), lambda i: (0, 0))
    mod = pl.BlockSpec((None, 1, D_MODEL), lambda i: (i * EP_BM // PB_TOK, 0, 0))
    wr = pl.BlockSpec((D_MODEL, LANES), lambda i: (0, 0))
    small = pl.BlockSpec((EP_BM, LANES), lambda i: (i, 0))
    packed = pl.BlockSpec((EP_BM, HALF_D), lambda i: (i, 0))
    return pl.pallas_call(
        _post_mix_kernel,
        name="post_mix",
        grid=(N_TOK // EP_BM,),
        in_specs=[row, row, mod, vec, vec, mod, mod, wr, wr, pl.BlockSpec((1, LANES), lambda i: (0, 0))],
        out_specs=[row, packed, small, small],
        out_shape=[jax.ShapeDtypeStruct((N_TOK, D_MODEL), F32),
                   jax.ShapeDtypeStruct((N_TOK, HALF_D), U32),
                   jax.ShapeDtypeStruct((N_TOK, LANES), I32),
                   jax.ShapeDtypeStruct((N_TOK, LANES), F32)],
        compiler_params=_params(("arbitrary",), 48),
    )(y, x, gate, g_post.reshape(1, D_MODEL), g_pre.reshape(1, D_MODEL), shift, scale, wr_hi, wr_lo, br)


CB_TOK = 128
CB_ROWS = CB_TOK * TOP_K
CB_STEPS = N_TOK // CB_TOK


CB_PROMPT_STEPS = N_PROMPT // CB_TOK


def _combine_kernel(dest_ref, dest_next_ref, e_hbm, tg_ref, x_ref, gate_ref, gpost_ref, *rest, split_out):
    out_refs, (buf, sem) = rest[:-2], rest[-2:]
    i = pl.program_id(0)
    slot = i % 2

    def gather(idx_ref, dst_slot):
        def body(t, carry):
            for k in range(TOP_K):
                pltpu.make_async_copy(e_hbm.at[pl.ds(idx_ref[0, t * TOP_K + k], 1)],
                                      buf.at[dst_slot, pl.ds(k * CB_TOK + t, 1)], sem.at[dst_slot]).start()
            return carry
        lax.fori_loop(0, CB_TOK, body, 0, unroll=2)

    @pl.when(i == 0)
    def _():
        gather(dest_ref, 0)

    @pl.when(i + 1 < CB_STEPS)
    def _():
        gather(dest_next_ref, 1 - slot)

    pltpu.make_async_copy(e_hbm.at[pl.ds(0, CB_ROWS)], buf.at[slot], sem.at[slot]).wait()
    tg = tg_ref[...]
    y = buf[slot, 0:CB_TOK, :] * tg[:, 0:1]
    for k in range(1, TOP_K):
        y = y + buf[slot, k * CB_TOK:(k + 1) * CB_TOK, :] * tg[:, k:k + 1]
    x_new = x_ref[...] + gate_ref[...] * (_rms(y) * gpost_ref[...])
    if split_out:
        @pl.when(i < CB_PROMPT_STEPS)
        def _():
            out_refs[0][...] = x_new

        @pl.when(i >= CB_PROMPT_STEPS)
        def _():
            out_refs[1][...] = x_new
    else:
        out_refs[0][...] = x_new


def _combine_post_ffn(expert_out, dest, top_gate, x, gate, g_post, split_out):
    row = pl.BlockSpec((CB_TOK, D_MODEL), lambda i: (i, 0))
    dest3 = dest.reshape(CB_STEPS, 1, CB_ROWS)
    if split_out:
        out_specs = [pl.BlockSpec((CB_TOK, D_MODEL), lambda i: (jnp.minimum(i, CB_PROMPT_STEPS - 1), 0)),
                     pl.BlockSpec((CB_TOK, D_MODEL), lambda i: (jnp.maximum(i - CB_PROMPT_STEPS, 0), 0))]
        out_shape = [jax.ShapeDtypeStruct((N_PROMPT, D_MODEL), F32), jax.ShapeDtypeStruct((N_SAMPLE, D_MODEL), F32)]
    else:
        out_specs = row
        out_shape = jax.ShapeDtypeStruct((N_TOK, D_MODEL), F32)
    return pl.pallas_call(
        functools.partial(_combine_kernel, split_out=split_out),
        name="moe_combine",
        grid=(CB_STEPS,),
        in_specs=[
            pl.BlockSpec((None, 1, CB_ROWS), lambda i: (i, 0, 0), memory_space=pltpu.SMEM),
            pl.BlockSpec((None, 1, CB_ROWS), lambda i: (jnp.minimum(i + 1, CB_STEPS - 1), 0, 0),
                         memory_space=pltpu.SMEM),
            pl.BlockSpec(memory_space=pl.ANY),
            pl.BlockSpec((CB_TOK, LANES), lambda i: (i, 0)),
            row,
            pl.BlockSpec((None, 1, D_MODEL), lambda i: (i * CB_TOK // PB_TOK, 0, 0)),
            pl.BlockSpec((1, D_MODEL), lambda i: (0, 0)),
        ],
        out_specs=out_specs,
        out_shape=out_shape,
        scratch_shapes=[pltpu.VMEM((2, CB_ROWS, D_MODEL), F32), pltpu.SemaphoreType.DMA((2,))],
        compiler_params=_params(("arbitrary",), 40),
    )(dest3, dest3, expert_out, top_gate, x, gate, g_post.reshape(1, D_MODEL))


def _for_valid_rows(nsub, first, rest):
    pair = 2 * MOE_SUB

    @pl.when(nsub == 1)
    def _():
        first(0, MOE_SUB)

    @pl.when(nsub >= 2)
    def _():
        first(0, pair)

        def body(p, carry):
            rest(pl.multiple_of(p * pair, pair), pair)
            return carry

        lax.fori_loop(1, nsub // 2, body, 0)

        @pl.when(nsub % 2 == 1)
        def _():
            rest(pl.multiple_of((nsub - 1) * MOE_SUB, MOE_SUB), MOE_SUB)


GATHER_UNROLL = 8


def _moe_up_kernel(sbe_ref, sbi_ref, nsub_ref, tok_ref, tok_next_ref, h_hbm, wg_ref, wl_ref, bg_ref, bl_ref,
                   o_ref, xg, xb, wgb, wlb, sem):
    s = pl.program_id(0)
    c = pl.program_id(1)
    nsub = nsub_ref[s]

    def gather(idx_ref, n_sub_blocks):
        def body(j, carry):
            for u in range(GATHER_UNROLL):
                a = j * GATHER_UNROLL + u
                pltpu.make_async_copy(h_hbm.at[pl.ds(idx_ref[0, a], 1)], xg.at[pl.ds(a, 1)], sem).start()
            return carry
        lax.fori_loop(0, n_sub_blocks * (MOE_SUB // GATHER_UNROLL), body, 0)

    @pl.when((nsub > 0) & (c == 0))
    def _():
        @pl.when(s == 0)
        def _():
            gather(tok_ref, nsub)

        def wait_rows(r, carry):
            pltpu.make_async_copy(h_hbm.at[pl.ds(0, MOE_SUB)], xg.at[pl.ds(0, MOE_SUB)], sem).wait()
            return carry

        lax.fori_loop(0, nsub, wait_rows, 0)

        def unpack(r, carry):
            rows = pl.ds(pl.multiple_of(r * MOE_SUB, MOE_SUB), MOE_SUB)
            w = xg[rows, :]
            xb[rows, 0:HALF_D] = pltpu.bitcast(w << SHIFT16, F32).astype(BF16)
            xb[rows, HALF_D:D_MODEL] = pltpu.bitcast(w & HI_MASK, F32).astype(BF16)
            return carry

        lax.fori_loop(0, nsub, unpack, 0)

        s_next = jnp.minimum(s + 1, MOE_NSB - 1)
        n_next = jnp.where(s + 1 < MOE_NSB, nsub_ref[s_next], 0)

        @pl.when(n_next > 0)
        def _():
            gather(tok_next_ref, n_next)

    def swiglu_store(rows, xs, wg, wl):
        glu = jnp.dot(xs, wg, preferred_element_type=F32) + bg_ref[...]
        lin = jnp.dot(xs, wl, preferred_element_type=F32) + bl_ref[...]
        glu = jnp.minimum(glu, SWIGLU_LIMIT)
        lin = jnp.clip(lin, -SWIGLU_LIMIT, SWIGLU_LIMIT)
        act = glu / (1.0 + jnp.exp(-SWIGLU_ALPHA * glu)) * (lin + 1.0)
        o_ref[rows, :] = act.astype(BF16)

    def first(r0, n):
        wg = wg_ref[...].astype(BF16)
        wl = wl_ref[...].astype(BF16)
        wgb[...] = wg
        wlb[...] = wl
        rows = pl.ds(r0, n)
        swiglu_store(rows, xb[rows, :], wg, wl)

    def rest(r0, n):
        rows = pl.ds(r0, n)
        swiglu_store(rows, xb[rows, :], wgb[...], wlb[...])

    @pl.when(nsub > 0)
    def _():
        _for_valid_rows(nsub, first, rest)


def _moe_up(h_packed, row_tok, sbe, sbi, nsub, w_gate_up, b_gate_up, layer):
    n_c = MOE_D_FF // MOE_FC
    last_c = n_c - 1

    def c_eff(s, c, nsub_ref):
        return jnp.where(nsub_ref[s] > 0, c, last_c)

    grid_spec = pltpu.PrefetchScalarGridSpec(
        num_scalar_prefetch=3,
        grid=(MOE_NSB, n_c),
        in_specs=[
            pl.BlockSpec((None, 1, MOE_R), lambda s, c, e, i, n: (i[s], 0, 0), memory_space=pltpu.SMEM),
            pl.BlockSpec((None, 1, MOE_R), lambda s, c, e, i, n: (i[jnp.minimum(s + 1, MOE_NSB - 1)], 0, 0),
                         memory_space=pltpu.SMEM),
            pl.BlockSpec(memory_space=pl.ANY),
            pl.BlockSpec((None, None, D_MODEL, MOE_FC), lambda s, c, e, i, n: (layer, e[s], 0, c_eff(s, c, n))),
            pl.BlockSpec((None, None, D_MODEL, MOE_FC),
                         lambda s, c, e, i, n: (layer, e[s], 0, n_c + c_eff(s, c, n))),
            pl.BlockSpec((None, None, 1, MOE_FC), lambda s, c, e, i, n: (layer, e[s], 0, c_eff(s, c, n))),
            pl.BlockSpec((None, None, 1, MOE_FC), lambda s, c, e, i, n: (layer, e[s], 0, n_c + c_eff(s, c, n))),
        ],
        out_specs=pl.BlockSpec((MOE_R, MOE_FC), lambda s, c, e, i, n: (i[s], c_eff(s, c, n))),
        scratch_shapes=[pltpu.VMEM((MOE_R, HALF_D), U32), pltpu.VMEM((MOE_R, D_MODEL), BF16),
                        pltpu.VMEM((D_MODEL, MOE_FC), BF16), pltpu.VMEM((D_MODEL, MOE_FC), BF16),
                        pltpu.SemaphoreType.DMA],
    )
    bgu = b_gate_up.reshape(DEPTH, N_EXPERTS, 1, 2 * MOE_D_FF)
    tok3 = row_tok.reshape(MOE_NSB, 1, MOE_R)
    return pl.pallas_call(
        _moe_up_kernel,
        name="moe_up",
        grid_spec=grid_spec,
        out_shape=jax.ShapeDtypeStruct((MOE_CAP, MOE_D_FF), BF16),
        compiler_params=_params(("arbitrary", "arbitrary"), 58),
    )(sbe, sbi, nsub, tok3, tok3, h_packed, w_gate_up, w_gate_up, bgu, bgu)


def _moe_down_kernel(sbe_ref, sbi_ref, nsub_ref, h_ref, w_ref, b_ref, o_ref, wb):
    s = pl.program_id(0)
    nsub = nsub_ref[s]

    def first(r0, n):
        w = w_ref[...].astype(BF16)
        wb[...] = w
        rows = pl.ds(r0, n)
        o_ref[rows, :] = jnp.dot(h_ref[rows, :], w, preferred_element_type=F32) + b_ref[...]

    def rest(r0, n):
        rows = pl.ds(r0, n)
        o_ref[rows, :] = jnp.dot(h_ref[rows, :], wb[...], preferred_element_type=F32) + b_ref[...]

    @pl.when(nsub > 0)
    def _():
        _for_valid_rows(nsub, first, rest)


def _moe_down(h_sorted, sbe, sbi, nsub, w_down, b_down, layer):
    n_c = D_MODEL // MOE_NC
    last_c = n_c - 1

    def c_eff(s, c, nsub_ref):
        return jnp.where(nsub_ref[s] > 0, c, last_c)

    grid_spec = pltpu.PrefetchScalarGridSpec(
        num_scalar_prefetch=3,
        grid=(MOE_NSB, n_c),
        in_specs=[
            pl.BlockSpec((MOE_R, MOE_D_FF), lambda s, c, e, i, n: (i[s], 0)),
            pl.BlockSpec((None, None, MOE_D_FF, MOE_NC), lambda s, c, e, i, n: (layer, e[s], 0, c_eff(s, c, n))),
            pl.BlockSpec((None, None, 1, MOE_NC), lambda s, c, e, i, n: (layer, e[s], 0, c_eff(s, c, n))),
        ],
        out_specs=pl.BlockSpec((MOE_R, MOE_NC), lambda s, c, e, i, n: (i[s], c_eff(s, c, n))),
        scratch_shapes=[pltpu.VMEM((MOE_D_FF, MOE_NC), BF16)],
    )
    return pl.pallas_call(
        _moe_down_kernel,
        name="moe_down",
        grid_spec=grid_spec,
        out_shape=jax.ShapeDtypeStruct((MOE_CAP, D_MODEL), F32),
        compiler_params=_params(("arbitrary", "arbitrary"), 48),
    )(sbe, sbi, nsub, h_sorted, w_down, b_down.reshape(DEPTH, N_EXPERTS, 1, D_MODEL))


def _route(top_idx):
    flat_e = top_idx.reshape(-1)
    onehot = (flat_e[:, None] == jnp.arange(N_EXPERTS, dtype=I32)[None, :]).astype(I32)
    csum = jnp.cumsum(onehot, axis=0)
    rank = jnp.take_along_axis(csum, flat_e[:, None], axis=1)[:, 0] - 1
    counts = csum[-1]
    nsb = (counts + MOE_R - 1) // MOE_R
    sb_end = jnp.cumsum(nsb)
    sb_off = sb_end - nsb
    dest = sb_off[flat_e] * MOE_R + rank
    n_real = sb_end[-1]
    s = jnp.arange(MOE_NSB, dtype=I32)
    s_eff = jnp.minimum(s, n_real - 1)
    sbe = jnp.minimum(jnp.searchsorted(sb_end, s_eff, side='right'), N_EXPERTS - 1).astype(I32)
    valid = jnp.clip(counts[sbe] - (s_eff - sb_off[sbe]) * MOE_R, 0, MOE_R)
    valid = jnp.where(s < n_real, valid, 0)
    nsub = ((valid + MOE_SUB - 1) // MOE_SUB).astype(I32)
    dest = dest.astype(I32)
    row_tok = jnp.zeros((MOE_CAP,), I32).at[dest].set(jnp.arange(N_ASSIGN, dtype=I32) // TOP_K)
    return dest, row_tok, sbe, s_eff.astype(I32), nsub


def _moe_experts(h_packed, top_idx, w_gate_up, b_gate_up, w_down, b_down, layer):
    dest, row_tok, sbe, sbi, nsub = _route(top_idx)
    h_sorted = _moe_up(h_packed, row_tok, sbe, sbi, nsub, w_gate_up, b_gate_up, layer)
    return _moe_down(h_sorted, sbe, sbi, nsub, w_down, b_down, layer), dest


def _pb_rows(mod_piece):
    idx = jnp.array([0] * N_PROMPT_PB + list(range(1, 1 + DEC_BATCH)), I32)
    return mod_piece[idx][:, None, :]


def kernel(x_prompt, x_sample, cache_a_k, cache_a_v, cache_b_k, cache_b_v, cache_c_k, cache_c_v, c, c_ctx, w_ada, b_ada, g_pre_mix, g_post_mix, g_pre_ffn, g_post_ffn, w_in_even, w_out_even, rpb_a, q_norm_b, k_norm_b, w_in_odd, w_out_odd, sink_c, w_router, b_router, w_gate_up, b_gate_up, w_down, b_down):
    x = jnp.concatenate([x_prompt.reshape(N_PROMPT, D_MODEL), x_sample.reshape(N_SAMPLE, D_MODEL)], axis=0)
    cond8 = jnp.concatenate([c_ctx[None, :], c, jnp.zeros((8 - 1 - DEC_BATCH, D_MODEL), F32)], axis=0)
    mods = _ada_mod(cond8, w_ada, b_ada)
    cos_tab, sin_tab = _rope_tables()
    no_sink = jnp.full((max(A_HEADS, B_HEADS),), -jnp.inf, F32)
    states = {}

    for l in range(DEPTH):
        j = l // 2
        m = [_pb_rows(mods[l, :, k * D_MODEL:(k + 1) * D_MODEL]) for k in range(6)]
        if l % 2 == 0:
            w_in = w_in_even[j].astype(BF16)
            w_out = w_out_even[j].astype(BF16)
            nb = EVEN_IN // IN_BN
            qb0 = 3 * A_HEADS * HEAD_DIM // IN_BN
            kb0 = qb0 + B_HEADS * HEAD_DIM // IN_BN
            vb0 = kb0 + B_KV_HEADS * HEAD_DIM // IN_BN
            modes = jnp.array([MODE_PLAIN] * qb0 + [MODE_NORM_Q] * (kb0 - qb0) + [MODE_NORM_K] * (vb0 - kb0)
                              + [MODE_PLAIN] * (nb - vb0), I32)
            gains = jnp.concatenate([q_norm_b[j][None], k_norm_b[j][None], jnp.ones((6, HEAD_DIM), F32)], axis=0)
        else:
            w_in = w_in_odd[j].astype(BF16)
            w_out = w_out_odd[j].astype(BF16)
            nb = ODD_IN // IN_BN
            vc0 = (C_HEADS + C_KV_HEADS) * HEAD_DIM // IN_BN
            modes = jnp.array([MODE_ROPE] * vc0 + [MODE_PLAIN] * (nb - vc0), I32)
            gains = jnp.ones((8, HEAD_DIM), F32)

        proj = _in_proj(x, g_pre_mix[l], m[0], m[1], w_in, modes, gains, cos_tab, sin_tab)

        if l % 2 == 0:
            ah = A_HEADS * HEAD_DIM
            ctx_a = _ctx_attention(proj, no_sink, 0, ah, 2 * ah, A_HEADS, A_HEADS, 8)
            ctx_b = _ctx_attention(proj, no_sink, 3 * ah, 3 * ah + B_HEADS * HEAD_DIM,
                                   3 * ah + (B_HEADS + B_KV_HEADS) * HEAD_DIM, B_HEADS, B_KV_HEADS, B_KV_HEADS)
            lat_a = _na_attention(proj, cache_a_k, cache_a_v, rpb_a[j], j)
            lat_b = _dense_attention(proj, cache_b_k, cache_b_v, j)
            prompt_parts, latent_parts = [ctx_a, ctx_b], [lat_a, lat_b]
            pp = proj[:N_PROMPT]
            kb_c = 3 * ah + B_HEADS * HEAD_DIM
            vb_c = kb_c + B_KV_HEADS * HEAD_DIM
            states.setdefault('ak', []).append(pp[:, ah:2 * ah].reshape(BATCH, SEQ, A_HEADS, HEAD_DIM))
            states.setdefault('av', []).append(pp[:, 2 * ah:3 * ah].reshape(BATCH, SEQ, A_HEADS, HEAD_DIM))
            states.setdefault('bk', []).append(pp[:, kb_c:vb_c].reshape(BATCH, SEQ, B_KV_HEADS, HEAD_DIM))
            states.setdefault('bv', []).append(pp[:, vb_c:].reshape(BATCH, SEQ, B_KV_HEADS, HEAD_DIM))
        else:
            qc = C_HEADS * HEAD_DIM
            kvw = C_KV_HEADS * HEAD_DIM
            ctx_c = _ctx_attention(proj, sink_c[j], 0, qc, qc + kvw, C_HEADS, C_KV_HEADS, C_KV_HEADS)
            lat_c = _window_attention(proj, cache_c_k, cache_c_v, sink_c[j], j)
            prompt_parts, latent_parts = [ctx_c], [lat_c]
            pp = proj[:N_PROMPT]
            states.setdefault('ck', []).append(pp[:, qc:qc + kvw].reshape(BATCH, SEQ, C_KV_HEADS, HEAD_DIM))
            states.setdefault('cv', []).append(pp[:, qc + kvw:].reshape(BATCH, SEQ, C_KV_HEADS, HEAD_DIM))

        y = _out_proj(prompt_parts, latent_parts, w_out)
        wr = jnp.pad(w_router[l], ((0, 0), (0, LANES - N_EXPERTS)))
        wr_hi = wr.astype(BF16)
        wr_lo = (wr - wr_hi.astype(F32)).astype(BF16)
        br = jnp.concatenate([b_router[l], jnp.full((LANES - N_EXPERTS,), NEG_BIG, F32)])[None, :]
        x, h_ffn, top_i, top_g = _post_mix(y, x, m[2], g_post_mix[l], g_pre_ffn[l], m[3], m[4], wr_hi, wr_lo, br)
        expert_out, dest = _moe_experts(h_ffn, top_i[:, :TOP_K], w_gate_up, b_gate_up, w_down, b_down, l)
        x = _combine_post_ffn(expert_out, dest, top_g, x, m[5], g_post_ffn[l], split_out=(l == DEPTH - 1))

    y_prompt = x[0].reshape(BATCH, SEQ, D_MODEL)
    y_sample = x[1].reshape(DEC_BATCH, DEC_SEQ, D_MODEL)
    st = {k: jnp.stack(v, axis=1) for k, v in states.items()}
    return (y_prompt, y_sample, st['ak'], st['av'], st['bk'], st['bv'], st['ck'], st['cv'])
```

```python
import functools

import jax
import jax.numpy as jnp
import numpy as np
from jax import lax
from jax.experimental import pallas as pl
from jax.experimental.pallas import tpu as pltpu

F32 = jnp.float32
BF16 = jnp.bfloat16
I32 = jnp.int32
U32 = jnp.uint32

D_MODEL = 4096
BATCH = 16
SEQ = 256
DEPTH = 2
DEC_BATCH = 4
DEC_SEQ = 2048
PAST_LEN = 512
GRID_W = 64
GRID_ROWS = DEC_SEQ // GRID_W
HEAD_DIM = 128
A_HEADS = 16
B_HEADS = 16
B_KV_HEADS = 4
C_HEADS = 32
C_KV_HEADS = 4
WIN_H = 8
WIN_W = 16
C_WINDOW = 128
N_EXPERTS = 32
TOP_K = 4
MOE_D_FF = 2048
SWIGLU_LIMIT = 7.0
SWIGLU_ALPHA = 1.702
ROPE_THETA = 10000.0
RMS_EPS = 1e-6
ATTN_SCALE = HEAD_DIM ** -0.5
EVEN_IN = (3 * A_HEADS + B_HEADS + 2 * B_KV_HEADS) * HEAD_DIM
ODD_IN = (C_HEADS + 2 * C_KV_HEADS) * HEAD_DIM

N_PROMPT = BATCH * SEQ
N_SAMPLE = DEC_BATCH * DEC_SEQ
N_TOK = N_PROMPT + N_SAMPLE
PB_TOK = DEC_SEQ
N_PB = N_TOK // PB_TOK
N_PROMPT_PB = N_PROMPT // PB_TOK

LANES = 128
MIB = 1024 * 1024

MOE_R = 2048
MOE_SUB = 256
N_ASSIGN = N_TOK * TOP_K
MOE_NSB = N_ASSIGN // MOE_R + N_EXPERTS
MOE_CAP = MOE_NSB * MOE_R
MOE_FC = 256
MOE_NC = 1024

NEG_BIG = -1e30
HALF_D = D_MODEL // 2
HI_MASK = np.uint32(0xFFFF0000)
SHIFT16 = np.uint32(16)


def _params(sem, vmem_mib):
    return pltpu.CompilerParams(dimension_semantics=sem, vmem_limit_bytes=vmem_mib * MIB)


def _ada_kernel(c_ref, w_ref, b_ref, o_ref):
    c = c_ref[...]
    s = (c / (1.0 + jnp.exp(-c))).astype(BF16)
    o_ref[...] = jnp.dot(s, w_ref[...].astype(BF16), preferred_element_type=F32) + b_ref[...]


def _ada_mod(cond8, w_ada, b_ada):
    bn = 512
    n_out = 6 * D_MODEL
    return pl.pallas_call(
        _ada_kernel,
        name="ada_mod",
        grid=(DEPTH, n_out // bn),
        in_specs=[
            pl.BlockSpec((8, D_MODEL), lambda l, j: (0, 0)),
            pl.BlockSpec((None, D_MODEL, bn), lambda l, j: (l, 0, j)),
            pl.BlockSpec((None, 1, bn), lambda l, j: (l, 0, j)),
        ],
        out_specs=pl.BlockSpec((None, 8, bn), lambda l, j: (l, 0, j)),
        out_shape=jax.ShapeDtypeStruct((DEPTH, 8, n_out), F32),
        compiler_params=_params(("arbitrary", "arbitrary"), 40),
    )(cond8, w_ada, b_ada.reshape(DEPTH, 1, n_out))


IN_BM = 512
IN_BN = 512
MODE_PLAIN, MODE_NORM_Q, MODE_NORM_K, MODE_ROPE = 0, 1, 2, 3


def _inproj_kernel(mode_ref, x_ref, g_ref, sh_ref, sc_ref, w_ref, gain_ref, cos_ref, sin_ref,
                   o_ref, h_ref):
    j = pl.program_id(1)

    @pl.when(j == 0)
    def _():
        x = x_ref[...]
        r = lax.rsqrt(jnp.mean(x * x, axis=-1, keepdims=True) + RMS_EPS)
        h = (x * r * g_ref[...]) * (1.0 + sc_ref[...]) + sh_ref[...]
        h_ref[...] = h.astype(BF16)

    acc = jnp.dot(h_ref[...], w_ref[...], preferred_element_type=F32)
    mode = mode_ref[j]

    @pl.when(mode == MODE_PLAIN)
    def _():
        o_ref[...] = acc

    def rope_store(hh, y):
        lane = lax.broadcasted_iota(I32, (IN_BM, LANES), 1)
        sw = jnp.where((lane % 64) < 32, pltpu.roll(y, 96, 1), pltpu.roll(y, 32, 1))
        o_ref[:, hh * HEAD_DIM:(hh + 1) * HEAD_DIM] = y * cos_ref[...] + sw * sin_ref[...]

    @pl.when(mode == MODE_ROPE)
    def _():
        for hh in range(IN_BN // HEAD_DIM):
            rope_store(hh, acc[:, hh * HEAD_DIM:(hh + 1) * HEAD_DIM])

    @pl.when((mode == MODE_NORM_Q) | (mode == MODE_NORM_K))
    def _():
        gain = jnp.where(mode == MODE_NORM_Q, gain_ref[0:1, :], gain_ref[1:2, :])
        for hh in range(IN_BN // HEAD_DIM):
            xh = acc[:, hh * HEAD_DIM:(hh + 1) * HEAD_DIM]
            rs = lax.rsqrt(jnp.mean(xh * xh, axis=-1, keepdims=True) + RMS_EPS)
            rope_store(hh, xh * rs * gain)


def _in_proj(x, g_pre, shift, scale, w_bf16, modes, gains, cos_tab, sin_tab):
    n = w_bf16.shape[1]
    n_i = N_TOK // IN_BM
    n_prompt_blocks = N_PROMPT // IN_BM
    blocks_per_seq = DEC_SEQ // IN_BM

    def tab_idx(i, j, m):
        return (jnp.where(i < n_prompt_blocks, 0, 1 + (i - n_prompt_blocks) % blocks_per_seq), 0)

    grid_spec = pltpu.PrefetchScalarGridSpec(
        num_scalar_prefetch=1,
        grid=(n_i, n // IN_BN),
        in_specs=[
            pl.BlockSpec((IN_BM, D_MODEL), lambda i, j, m: (i, 0)),
            pl.BlockSpec((1, D_MODEL), lambda i, j, m: (0, 0)),
            pl.BlockSpec((None, 1, D_MODEL), lambda i, j, m: (i * IN_BM // PB_TOK, 0, 0)),
            pl.BlockSpec((None, 1, D_MODEL), lambda i, j, m: (i * IN_BM // PB_TOK, 0, 0)),
            pl.BlockSpec((D_MODEL, IN_BN), lambda i, j, m: (0, j)),
            pl.BlockSpec((8, LANES), lambda i, j, m: (0, 0)),
            pl.BlockSpec((IN_BM, LANES), tab_idx),
            pl.BlockSpec((IN_BM, LANES), tab_idx),
        ],
        out_specs=pl.BlockSpec((IN_BM, IN_BN), lambda i, j, m: (i, j)),
        scratch_shapes=[pltpu.VMEM((IN_BM, D_MODEL), BF16)],
    )
    return pl.pallas_call(
        _inproj_kernel,
        name="in_proj",
        grid_spec=grid_spec,
        out_shape=jax.ShapeDtypeStruct((N_TOK, n), F32),
        compiler_params=_params(("arbitrary", "arbitrary"), 48),
    )(modes, x, g_pre.reshape(1, D_MODEL), shift, scale, w_bf16, gains, cos_tab, sin_tab)


def _rope_tables():
    t = jnp.arange(DEC_SEQ)
    row = (t // GRID_W).astype(F32)
    col = (t % GRID_W).astype(F32)
    n_freq = HEAD_DIM // 4
    inv = ROPE_THETA ** (-jnp.arange(n_freq, dtype=F32) / n_freq)
    ar = row[:, None] * inv
    ac = col[:, None] * inv
    cos = jnp.concatenate([jnp.cos(ar), jnp.cos(ar), jnp.cos(ac), jnp.cos(ac)], axis=-1)
    sin = jnp.concatenate([-jnp.sin(ar), jnp.sin(ar), -jnp.sin(ac), jnp.sin(ac)], axis=-1)
    cos = jnp.concatenate([jnp.ones((IN_BM, HEAD_DIM), F32), cos], axis=0)
    sin = jnp.concatenate([jnp.zeros((IN_BM, HEAD_DIM), F32), sin], axis=0)
    return cos, sin


MM_BM = 1024
MM_BN = 512


MM_PROMPT_BLOCKS = N_PROMPT // MM_BM


def _out_proj_kernel(*refs, n_parts):
    prompt_refs = refs[:n_parts]
    latent_refs = refs[n_parts:2 * n_parts]
    w_ref, o_ref = refs[2 * n_parts:]
    i = pl.program_id(0)

    def run(parts):
        acc = None
        k0 = 0
        for r in parts:
            kk = r.shape[1]
            d = jnp.dot(r[...], w_ref[k0:k0 + kk, :], preferred_element_type=F32)
            acc = d if acc is None else acc + d
            k0 += kk
        o_ref[...] = acc

    @pl.when(i < MM_PROMPT_BLOCKS)
    def _():
        run(prompt_refs)

    @pl.when(i >= MM_PROMPT_BLOCKS)
    def _():
        run(latent_refs)


def _out_proj(prompt_parts, latent_parts, w_bf16):
    k, n = w_bf16.shape
    n_parts = len(prompt_parts)
    last_prompt = MM_PROMPT_BLOCKS - 1
    in_specs = ([pl.BlockSpec((MM_BM, p.shape[1]), lambda i, j: (jnp.minimum(i, last_prompt), 0))
                 for p in prompt_parts]
                + [pl.BlockSpec((MM_BM, p.shape[1]), lambda i, j: (jnp.maximum(i - MM_PROMPT_BLOCKS, 0), 0))
                   for p in latent_parts]
                + [pl.BlockSpec((k, MM_BN), lambda i, j: (0, j))])
    return pl.pallas_call(
        functools.partial(_out_proj_kernel, n_parts=n_parts),
        name="out_proj",
        grid=(N_TOK // MM_BM, n // MM_BN),
        in_specs=in_specs,
        out_specs=pl.BlockSpec((MM_BM, MM_BN), lambda i, j: (i, j)),
        out_shape=jax.ShapeDtypeStruct((N_TOK, n), F32),
        compiler_params=_params(("arbitrary", "arbitrary"), 56),
    )(*prompt_parts, *latent_parts, w_bf16)


def _attend_heads(scores, values, sinks=None):
    n = len(scores)
    ms = []
    for h in range(n):
        m = jnp.max(scores[h][0], axis=-1, keepdims=True)
        for s in scores[h][1:]:
            m = jnp.maximum(m, jnp.max(s, axis=-1, keepdims=True))
        if sinks is not None:
            m = jnp.maximum(m, sinks[h])
        ms.append(m)
    ps = [[jnp.exp(s - ms[h]) for s in scores[h]] for h in range(n)]
    dens = []
    for h in range(n):
        d = jnp.sum(ps[h][0], axis=-1, keepdims=True)
        for p in ps[h][1:]:
            d = d + jnp.sum(p, axis=-1, keepdims=True)
        if sinks is not None:
            d = d + jnp.exp(sinks[h] - ms[h])
        dens.append(d)
    outs = []
    for h in range(n):
        o = jnp.dot(ps[h][0].astype(BF16), values[h][0], preferred_element_type=F32)
        for p, v in zip(ps[h][1:], values[h][1:]):
            o = o + jnp.dot(p.astype(BF16), v, preferred_element_type=F32)
        outs.append(o / dens[h])
    return outs


def _qk(q, k):
    return lax.dot_general(q, k, (((1,), (1,)), ((), ())), preferred_element_type=F32) * ATTN_SCALE


def _ctx_attn_kernel(sink_ref, q_ref, k_ref, v_ref, o_ref, *, nkv, g):
    kb = pl.program_id(1)
    for kv in range(nkv):
        k = k_ref[:, kv * HEAD_DIM:(kv + 1) * HEAD_DIM].astype(BF16)
        v = v_ref[:, kv * HEAD_DIM:(kv + 1) * HEAD_DIM].astype(BF16)
        cols = [slice((kv * g + gi) * HEAD_DIM, (kv * g + gi + 1) * HEAD_DIM) for gi in range(g)]
        scores = [[_qk(q_ref[:, cs].astype(BF16), k)] for cs in cols]
        sinks = [sink_ref[(kb * nkv + kv) * g + gi] for gi in range(g)]
        for cs, o in zip(cols, _attend_heads(scores, [[v]] * g, sinks)):
            o_ref[:, cs] = o.astype(o_ref.dtype)


def _ctx_attention(proj, sink, q_col, k_col, v_col, hq, hkv, nkv):
    g = hq // hkv
    qw = nkv * g * HEAD_DIM
    kw = nkv * HEAD_DIM
    return pl.pallas_call(
        functools.partial(_ctx_attn_kernel, nkv=nkv, g=g),
        name="ctx_attn",
        grid=(BATCH, hkv // nkv),
        in_specs=[
            pl.BlockSpec(memory_space=pltpu.SMEM),
            pl.BlockSpec((SEQ, qw), lambda b, h: (b, q_col // qw + h)),
            pl.BlockSpec((SEQ, kw), lambda b, h: (b, k_col // kw + h)),
            pl.BlockSpec((SEQ, kw), lambda b, h: (b, v_col // kw + h)),
        ],
        out_specs=pl.BlockSpec((SEQ, qw), lambda b, h: (b, h)),
        out_shape=jax.ShapeDtypeStruct((N_PROMPT, hq * HEAD_DIM), BF16),
        compiler_params=_params(("arbitrary", "arbitrary"), 40),
    )(sink, proj, proj, proj)


DENSE_BQ = 256


def _cache_bf16(step, srcs, dsts):
    @pl.when(step == 0)
    def _():
        for src, dst in zip(srcs, dsts):
            dst[...] = src[...].astype(BF16)


def _kv_scratch(width):
    return [pltpu.VMEM((DEC_SEQ, width), BF16), pltpu.VMEM((DEC_SEQ, width), BF16),
            pltpu.VMEM((PAST_LEN, width), BF16), pltpu.VMEM((PAST_LEN, width), BF16)]


def _dense_attn_kernel(q_ref, k_ref, v_ref, kc_ref, vc_ref, o_ref, kb, vb, kcb, vcb, *, g):
    _cache_bf16(pl.program_id(2), (k_ref, v_ref, kc_ref, vc_ref), (kb, vb, kcb, vcb))
    k = kb[...]
    v = vb[...]
    kc = kcb[...]
    vc = vcb[...]
    cols = [slice(gi * HEAD_DIM, (gi + 1) * HEAD_DIM) for gi in range(g)]
    qs = [q_ref[:, cs].astype(BF16) for cs in cols]
    scores = [[_qk(q, k), _qk(q, kc)] for q in qs]
    for cs, o in zip(cols, _attend_heads(scores, [[v, vc]] * g)):
        o_ref[:, cs] = o.astype(o_ref.dtype)


def _dense_attention(proj, cache_k, cache_v, layer_j):
    g = B_HEADS // B_KV_HEADS
    qw = g * HEAD_DIM
    nq = DEC_SEQ // DENSE_BQ
    q_col = 3 * A_HEADS * HEAD_DIM
    k_col = q_col + B_HEADS * HEAD_DIM
    v_col = k_col + B_KV_HEADS * HEAD_DIM
    row0 = N_PROMPT // DENSE_BQ
    ck = cache_k.reshape(DEC_BATCH, -1, PAST_LEN, B_KV_HEADS * HEAD_DIM)
    cv = cache_v.reshape(DEC_BATCH, -1, PAST_LEN, B_KV_HEADS * HEAD_DIM)
    return pl.pallas_call(
        functools.partial(_dense_attn_kernel, g=g),
        name="dense_attn",
        grid=(DEC_BATCH, B_KV_HEADS, nq),
        in_specs=[
            pl.BlockSpec((DENSE_BQ, qw), lambda b, h, i: (row0 + b * nq + i, q_col // qw + h)),
            pl.BlockSpec((DEC_SEQ, HEAD_DIM), lambda b, h, i: (N_PROMPT_PB + b, k_col // HEAD_DIM + h)),
            pl.BlockSpec((DEC_SEQ, HEAD_DIM), lambda b, h, i: (N_PROMPT_PB + b, v_col // HEAD_DIM + h)),
            pl.BlockSpec((None, None, PAST_LEN, HEAD_DIM), lambda b, h, i: (b, layer_j, 0, h)),
            pl.BlockSpec((None, None, PAST_LEN, HEAD_DIM), lambda b, h, i: (b, layer_j, 0, h)),
        ],
        out_specs=pl.BlockSpec((DENSE_BQ, qw), lambda b, h, i: (b * nq + i, h)),
        scratch_shapes=_kv_scratch(HEAD_DIM),
        out_shape=jax.ShapeDtypeStruct((N_SAMPLE, B_HEADS * HEAD_DIM), BF16),
        compiler_params=_params(("arbitrary", "arbitrary", "arbitrary"), 48),
    )(proj, proj, proj, ck, cv)


WIN_BQ = 256
WIN_KEYS = WIN_BQ + 2 * C_WINDOW


def _window_attn_kernel(sink_ref, q_ref, k_ref, v_ref, kc_ref, vc_ref, o_ref, kb, vb, kcb, vcb, *, g):
    h = pl.program_id(1)
    i = pl.program_id(2)
    _cache_bf16(i, (k_ref, v_ref, kc_ref, vc_ref), (kb, vb, kcb, vcb))
    q0 = i * WIN_BQ
    k0 = pl.multiple_of(jnp.clip(q0 - C_WINDOW, 0, DEC_SEQ - WIN_KEYS), C_WINDOW)
    k = kb[pl.ds(k0, WIN_KEYS), :]
    v = vb[pl.ds(k0, WIN_KEYS), :]
    kc = kcb[...]
    vc = vcb[...]
    q_pos = q0 + lax.broadcasted_iota(I32, (WIN_BQ, WIN_KEYS), 0)
    k_pos = k0 + lax.broadcasted_iota(I32, (WIN_BQ, WIN_KEYS), 1)
    ok = jnp.abs(q_pos - k_pos) <= C_WINDOW
    cols = [slice(gi * HEAD_DIM, (gi + 1) * HEAD_DIM) for gi in range(g)]
    qs = [q_ref[:, cs].astype(BF16) for cs in cols]
    scores = [[jnp.where(ok, _qk(q, k), -jnp.inf), _qk(q, kc)] for q in qs]
    sinks = [sink_ref[h * g + gi] for gi in range(g)]
    for cs, o in zip(cols, _attend_heads(scores, [[v, vc]] * g, sinks)):
        o_ref[:, cs] = o.astype(o_ref.dtype)


def _window_attention(proj, cache_k, cache_v, sink, layer_j):
    g = C_HEADS // C_KV_HEADS
    qw = g * HEAD_DIM
    nq = DEC_SEQ // WIN_BQ
    k_col = C_HEADS * HEAD_DIM
    v_col = k_col + C_KV_HEADS * HEAD_DIM
    row0 = N_PROMPT // WIN_BQ
    ck = cache_k.reshape(DEC_BATCH, -1, PAST_LEN, C_KV_HEADS * HEAD_DIM)
    cv = cache_v.reshape(DEC_BATCH, -1, PAST_LEN, C_KV_HEADS * HEAD_DIM)
    return pl.pallas_call(
        functools.partial(_window_attn_kernel, g=g),
        name="window_attn",
        grid=(DEC_BATCH, C_KV_HEADS, nq),
        in_specs=[
            pl.BlockSpec(memory_space=pltpu.SMEM),
            pl.BlockSpec((WIN_BQ, qw), lambda b, h, i: (row0 + b * nq + i, h)),
            pl.BlockSpec((DEC_SEQ, HEAD_DIM), lambda b, h, i: (N_PROMPT_PB + b, k_col // HEAD_DIM + h)),
            pl.BlockSpec((DEC_SEQ, HEAD_DIM), lambda b, h, i: (N_PROMPT_PB + b, v_col // HEAD_DIM + h)),
            pl.BlockSpec((None, None, PAST_LEN, HEAD_DIM), lambda b, h, i: (b, layer_j, 0, h)),
            pl.BlockSpec((None, None, PAST_LEN, HEAD_DIM), lambda b, h, i: (b, layer_j, 0, h)),
        ],
        out_specs=pl.BlockSpec((WIN_BQ, qw), lambda b, h, i: (b * nq + i, h)),
        scratch_shapes=_kv_scratch(HEAD_DIM),
        out_shape=jax.ShapeDtypeStruct((N_SAMPLE, C_HEADS * HEAD_DIM), BF16),
        compiler_params=_params(("arbitrary", "arbitrary", "arbitrary"), 48),
    )(sink, proj, proj, proj, ck, cv)


NA_HC = 4
NA_KEYS = WIN_H * GRID_W


def _na_row_start(i):
    return jnp.clip(i - WIN_H // 2, 0, GRID_ROWS - WIN_H)


def _na_attn_kernel(q_ref, k_ref, v_ref, kc_ref, vc_ref, bias_ref, o_ref, kb, vb, kcb, vcb):
    i = pl.program_id(2)
    _cache_bf16(i, (k_ref, v_ref, kc_ref, vc_ref), (kb, vb, kcb, vcb))
    r0 = pl.multiple_of(_na_row_start(i) * GRID_W, GRID_W)
    rows = pl.ds(r0, NA_KEYS)
    heads = [slice(hh * HEAD_DIM, (hh + 1) * HEAD_DIM) for hh in range(NA_HC)]
    qs = [q_ref[:, cs].astype(BF16) for cs in heads]
    scores = [[_qk(q, kb[rows, cs]) + bias_ref[hh], _qk(q, kcb[:, cs])] for hh, (q, cs) in enumerate(zip(qs, heads))]
    values = [[vb[rows, cs], vcb[:, cs]] for cs in heads]
    for cs, o in zip(heads, _attend_heads(scores, values)):
        o_ref[:, cs] = o.astype(o_ref.dtype)


def _na_bias_table(rpb):
    qc = jnp.arange(GRID_W)[:, None]
    kc = jnp.arange(GRID_W)[None, :]
    ws = jnp.clip(qc - WIN_W // 2, 0, GRID_W - WIN_W)
    ok = (kc >= ws) & (kc < ws + WIN_W)
    dc = jnp.clip(kc - qc + WIN_W - 1, 0, 2 * WIN_W - 2)
    t = rpb[:, :, dc]
    t = jnp.where(ok[None, None], t, -jnp.inf)
    dr = jnp.arange(WIN_H)[:, None] + jnp.arange(WIN_H)[None, :]
    b = t[:, dr]
    b = b.transpose(0, 1, 3, 2, 4).reshape(A_HEADS, WIN_H, GRID_W, NA_KEYS)
    return b.astype(F32)


def _na_attention(proj, cache_k, cache_v, rpb, layer_j):
    cw = NA_HC * HEAD_DIM
    k_col = A_HEADS * HEAD_DIM
    v_col = 2 * A_HEADS * HEAD_DIM
    row0 = N_PROMPT // GRID_W
    ck = cache_k.reshape(DEC_BATCH, -1, PAST_LEN, A_HEADS * HEAD_DIM)
    cv = cache_v.reshape(DEC_BATCH, -1, PAST_LEN, A_HEADS * HEAD_DIM)
    bias = _na_bias_table(rpb)

    def bias_idx(b, h, i):
        return (h, _na_row_start(i) - i + WIN_H - 1, 0, 0)

    return pl.pallas_call(
        _na_attn_kernel,
        name="na_attn",
        grid=(DEC_BATCH, A_HEADS // NA_HC, GRID_ROWS),
        in_specs=[
            pl.BlockSpec((GRID_W, cw), lambda b, h, i: (row0 + b * GRID_ROWS + i, h)),
            pl.BlockSpec((DEC_SEQ, cw), lambda b, h, i: (N_PROMPT_PB + b, k_col // cw + h)),
            pl.BlockSpec((DEC_SEQ, cw), lambda b, h, i: (N_PROMPT_PB + b, v_col // cw + h)),
            pl.BlockSpec((None, None, PAST_LEN, cw), lambda b, h, i: (b, layer_j, 0, h)),
            pl.BlockSpec((None, None, PAST_LEN, cw), lambda b, h, i: (b, layer_j, 0, h)),
            pl.BlockSpec((NA_HC, None, GRID_W, NA_KEYS), bias_idx),
        ],
        out_specs=pl.BlockSpec((GRID_W, cw), lambda b, h, i: (b * GRID_ROWS + i, h)),
        scratch_shapes=_kv_scratch(cw),
        out_shape=jax.ShapeDtypeStruct((N_SAMPLE, A_HEADS * HEAD_DIM), BF16),
        compiler_params=_params(("arbitrary", "arbitrary", "arbitrary"), 48),
    )(proj, proj, proj, ck, cv, bias)


EP_BM = 256


def _rms(x):
    return x * lax.rsqrt(jnp.mean(x * x, axis=-1, keepdims=True) + RMS_EPS)


def _post_mix_kernel(y_ref, x_ref, gate_ref, gpost_ref, gpre_ref, sh_ref, sc_ref,
                     wr_hi_ref, wr_lo_ref, br_ref, xo_ref, h_ref, ti_ref, tg_ref):
    x_new = x_ref[...] + gate_ref[...] * (_rms(y_ref[...]) * gpost_ref[...])
    xo_ref[...] = x_new
    h = (_rms(x_new) * gpre_ref[...]) * (1.0 + sc_ref[...]) + sh_ref[...]
    h_hi = h.astype(BF16)
    h_hi32 = h_hi.astype(F32)
    bits = pltpu.bitcast(h_hi32, U32)
    h_ref[...] = (bits[:, :HALF_D] >> SHIFT16) | (bits[:, HALF_D:] & HI_MASK)
    h_lo = (h - h_hi32).astype(BF16)
    w_hi = wr_hi_ref[...]
    logits = (jnp.dot(h_hi, w_hi, preferred_element_type=F32)
              + jnp.dot(h_lo, w_hi, preferred_element_type=F32)
              + jnp.dot(h_hi, wr_lo_ref[...], preferred_element_type=F32)) + br_ref[...]
    lane = lax.broadcasted_iota(I32, (EP_BM, LANES), 1).astype(F32)
    vals, idxs = [], []
    l = logits
    for _ in range(TOP_K):
        m = jnp.max(l, axis=-1, keepdims=True)
        idx = jnp.min(jnp.where(l == m, lane, float(LANES)), axis=-1, keepdims=True)
        vals.append(m)
        idxs.append(idx)
        l = jnp.where(lane == idx, -jnp.inf, l)
    es = [jnp.exp(v - vals[0]) for v in vals]
    den = es[0] + es[1] + es[2] + es[3]
    ti = jnp.zeros((EP_BM, LANES), F32)
    tg = jnp.zeros((EP_BM, LANES), F32)
    for k in range(TOP_K):
        ti = jnp.where(lane == float(k), idxs[k], ti)
        tg = jnp.where(lane == float(k), es[k] / den, tg)
    ti_ref[...] = ti.astype(I32)
    tg_ref[...] = tg


def _post_mix(y, x, gate, g_post, g_pre, shift, scale, wr_hi, wr_lo, br):
    row = pl.BlockSpec((EP_BM, D_MODEL), lambda i: (i, 0))
    vec = pl.BlockSpec((1, D_MODEL), lambda i: (0, 0))
    mod = pl.BlockSpec((None, 1, D_MODEL), lambda i: (i * EP_BM // PB_TOK, 0, 0))
    wr = pl.BlockSpec((D_MODEL, LANES), lambda i: (0, 0))
    small = pl.BlockSpec((EP_BM, LANES), lambda i: (i, 0))
    packed = pl.BlockSpec((EP_BM, HALF_D), lambda i: (i, 0))
    return pl.pallas_call(
        _post_mix_kernel,
        name="post_mix",
        grid=(N_TOK // EP_BM,),
        in_specs=[row, row, mod, vec, vec, mod, mod, wr, wr, pl.BlockSpec((1, LANES), lambda i: (0, 0))],
        out_specs=[row, packed, small, small],
        out_shape=[jax.ShapeDtypeStruct((N_TOK, D_MODEL), F32),
                   jax.ShapeDtypeStruct((N_TOK, HALF_D), U32),
                   jax.ShapeDtypeStruct((N_TOK, LANES), I32),
                   jax.ShapeDtypeStruct((N_TOK, LANES), F32)],
        compiler_params=_params(("arbitrary",), 48),
    )(y, x, gate, g_post.reshape(1, D_MODEL), g_pre.reshape(1, D_MODEL), shift, scale, wr_hi, wr_lo, br)


CB_TOK = 128
CB_ROWS = CB_TOK * TOP_K
CB_STEPS = N_TOK // CB_TOK


CB_PROMPT_STEPS = N_PROMPT // CB_TOK


def _combine_kernel(dest_ref, dest_next_ref, e_hbm, tg_ref, x_ref, gate_ref, gpost_ref, *rest, split_out):
    out_refs, (buf, sem) = rest[:-2], rest[-2:]
    i = pl.program_id(0)
    slot = i % 2

    def gather(idx_ref, dst_slot):
        def body(t, carry):
            for k in range(TOP_K):
                pltpu.make_async_copy(e_hbm.at[pl.ds(idx_ref[0, t * TOP_K + k], 1)],
                                      buf.at[dst_slot, pl.ds(k * CB_TOK + t, 1)], sem.at[dst_slot]).start()
            return carry
        lax.fori_loop(0, CB_TOK, body, 0, unroll=2)

    @pl.when(i == 0)
    def _():
        gather(dest_ref, 0)

    @pl.when(i + 1 < CB_STEPS)
    def _():
        gather(dest_next_ref, 1 - slot)

    pltpu.make_async_copy(e_hbm.at[pl.ds(0, CB_ROWS)], buf.at[slot], sem.at[slot]).wait()
    tg = tg_ref[...]
    y = buf[slot, 0:CB_TOK, :] * tg[:, 0:1]
    for k in range(1, TOP_K):
        y = y + buf[slot, k * CB_TOK:(k + 1) * CB_TOK, :] * tg[:, k:k + 1]
    x_new = x_ref[...] + gate_ref[...] * (_rms(y) * gpost_ref[...])
    if split_out:
        @pl.when(i < CB_PROMPT_STEPS)
        def _():
            out_refs[0][...] = x_new

        @pl.when(i >= CB_PROMPT_STEPS)
        def _():
            out_refs[1][...] = x_new
    else:
        out_refs[0][...] = x_new


def _combine_post_ffn(expert_out, dest, top_gate, x, gate, g_post, split_out):
    row = pl.BlockSpec((CB_TOK, D_MODEL), lambda i: (i, 0))
    dest3 = dest.reshape(CB_STEPS, 1, CB_ROWS)
    if split_out:
        out_specs = [pl.BlockSpec((CB_TOK, D_MODEL), lambda i: (jnp.minimum(i, CB_PROMPT_STEPS - 1), 0)),
                     pl.BlockSpec((CB_TOK, D_MODEL), lambda i: (jnp.maximum(i - CB_PROMPT_STEPS, 0), 0))]
        out_shape = [jax.ShapeDtypeStruct((N_PROMPT, D_MODEL), F32), jax.ShapeDtypeStruct((N_SAMPLE, D_MODEL), F32)]
    else:
        out_specs = row
        out_shape = jax.ShapeDtypeStruct((N_TOK, D_MODEL), F32)
    return pl.pallas_call(
        functools.partial(_combine_kernel, split_out=split_out),
        name="moe_combine",
        grid=(CB_STEPS,),
        in_specs=[
            pl.BlockSpec((None, 1, CB_ROWS), lambda i: (i, 0, 0), memory_space=pltpu.SMEM),
            pl.BlockSpec((None, 1, CB_ROWS), lambda i: (jnp.minimum(i + 1, CB_STEPS - 1), 0, 0),
                         memory_space=pltpu.SMEM),
            pl.BlockSpec(memory_space=pl.ANY),
            pl.BlockSpec((CB_TOK, LANES), lambda i: (i, 0)),
            row,
            pl.BlockSpec((None, 1, D_MODEL), lambda i: (i * CB_TOK // PB_TOK, 0, 0)),
            pl.BlockSpec((1, D_MODEL), lambda i: (0, 0)),
        ],
        out_specs=out_specs,
        out_shape=out_shape,
        scratch_shapes=[pltpu.VMEM((2, CB_ROWS, D_MODEL), F32), pltpu.SemaphoreType.DMA((2,))],
        compiler_params=_params(("arbitrary",), 40),
    )(dest3, dest3, expert_out, top_gate, x, gate, g_post.reshape(1, D_MODEL))


def _for_valid_rows(nsub, first, rest):
    pair = 2 * MOE_SUB

    @pl.when(nsub == 1)
    def _():
        first(0, MOE_SUB)

    @pl.when(nsub >= 2)
    def _():
        first(0, pair)

        def body(p, carry):
            rest(pl.multiple_of(p * pair, pair), pair)
            return carry

        lax.fori_loop(1, nsub // 2, body, 0)

        @pl.when(nsub % 2 == 1)
        def _():
            rest(pl.multiple_of((nsub - 1) * MOE_SUB, MOE_SUB), MOE_SUB)


GATHER_UNROLL = 8


def _moe_up_kernel(sbe_ref, sbi_ref, nsub_ref, tok_ref, tok_next_ref, h_hbm, wg_ref, wl_ref, bg_ref, bl_ref,
                   o_ref, xg, xb, wgb, wlb, sem):
    s = pl.program_id(0)
    c = pl.program_id(1)
    nsub = nsub_ref[s]

    def gather(idx_ref, n_sub_blocks):
        def body(j, carry):
            for u in range(GATHER_UNROLL):
                a = j * GATHER_UNROLL + u
                pltpu.make_async_copy(h_hbm.at[pl.ds(idx_ref[0, a], 1)], xg.at[pl.ds(a, 1)], sem).start()
            return carry
        lax.fori_loop(0, n_sub_blocks * (MOE_SUB // GATHER_UNROLL), body, 0)

    @pl.when((nsub > 0) & (c == 0))
    def _():
        @pl.when(s == 0)
        def _():
            gather(tok_ref, nsub)

        def wait_rows(r, carry):
            pltpu.make_async_copy(h_hbm.at[pl.ds(0, MOE_SUB)], xg.at[pl.ds(0, MOE_SUB)], sem).wait()
            return carry

        lax.fori_loop(0, nsub, wait_rows, 0)

        def unpack(r, carry):
            rows = pl.ds(pl.multiple_of(r * MOE_SUB, MOE_SUB), MOE_SUB)
            w = xg[rows, :]
            xb[rows, 0:HALF_D] = pltpu.bitcast(w << SHIFT16, F32).astype(BF16)
            xb[rows, HALF_D:D_MODEL] = pltpu.bitcast(w & HI_MASK, F32).astype(BF16)
            return carry

        lax.fori_loop(0, nsub, unpack, 0)

        s_next = jnp.minimum(s + 1, MOE_NSB - 1)
        n_next = jnp.where(s + 1 < MOE_NSB, nsub_ref[s_next], 0)

        @pl.when(n_next > 0)
        def _():
            gather(tok_next_ref, n_next)

    def swiglu_store(rows, xs, wg, wl):
        glu = jnp.dot(xs, wg, preferred_element_type=F32) + bg_ref[...]
        lin = jnp.dot(xs, wl, preferred_element_type=F32) + bl_ref[...]
        glu = jnp.minimum(glu, SWIGLU_LIMIT)
        lin = jnp.clip(lin, -SWIGLU_LIMIT, SWIGLU_LIMIT)
        act = glu / (1.0 + jnp.exp(-SWIGLU_ALPHA * glu)) * (lin + 1.0)
        o_ref[rows, :] = act.astype(BF16)

    def first(r0, n):
        wg = wg_ref[...].astype(BF16)
        wl = wl_ref[...].astype(BF16)
        wgb[...] = wg
        wlb[...] = wl
        rows = pl.ds(r0, n)
        swiglu_store(rows, xb[rows, :], wg, wl)

    def rest(r0, n):
        rows = pl.ds(r0, n)
        swiglu_store(rows, xb[rows, :], wgb[...], wlb[...])

    @pl.when(nsub > 0)
    def _():
        _for_valid_rows(nsub, first, rest)


def _moe_up(h_packed, row_tok, sbe, sbi, nsub, w_gate_up, b_gate_up, layer):
    n_c = MOE_D_FF // MOE_FC
    last_c = n_c - 1

    def c_eff(s, c, nsub_ref):
        return jnp.where(nsub_ref[s] > 0, c, last_c)

    grid_spec = pltpu.PrefetchScalarGridSpec(
        num_scalar_prefetch=3,
        grid=(MOE_NSB, n_c),
        in_specs=[
            pl.BlockSpec((None, 1, MOE_R), lambda s, c, e, i, n: (i[s], 0, 0), memory_space=pltpu.SMEM),
            pl.BlockSpec((None, 1, MOE_R), lambda s, c, e, i, n: (i[jnp.minimum(s + 1, MOE_NSB - 1)], 0, 0),
                         memory_space=pltpu.SMEM),
            pl.BlockSpec(memory_space=pl.ANY),
            pl.BlockSpec((None, None, D_MODEL, MOE_FC), lambda s, c, e, i, n: (layer, e[s], 0, c_eff(s, c, n))),
            pl.BlockSpec((None, None, D_MODEL, MOE_FC),
                         lambda s, c, e, i, n: (layer, e[s], 0, n_c + c_eff(s, c, n))),
            pl.BlockSpec((None, None, 1, MOE_FC), lambda s, c, e, i, n: (layer, e[s], 0, c_eff(s, c, n))),
            pl.BlockSpec((None, None, 1, MOE_FC), lambda s, c, e, i, n: (layer, e[s], 0, n_c + c_eff(s, c, n))),
        ],
        out_specs=pl.BlockSpec((MOE_R, MOE_FC), lambda s, c, e, i, n: (i[s], c_eff(s, c, n))),
        scratch_shapes=[pltpu.VMEM((MOE_R, HALF_D), U32), pltpu.VMEM((MOE_R, D_MODEL), BF16),
                        pltpu.VMEM((D_MODEL, MOE_FC), BF16), pltpu.VMEM((D_MODEL, MOE_FC), BF16),
                        pltpu.SemaphoreType.DMA],
    )
    bgu = b_gate_up.reshape(DEPTH, N_EXPERTS, 1, 2 * MOE_D_FF)
    tok3 = row_tok.reshape(MOE_NSB, 1, MOE_R)
    return pl.pallas_call(
        _moe_up_kernel,
        name="moe_up",
        grid_spec=grid_spec,
        out_shape=jax.ShapeDtypeStruct((MOE_CAP, MOE_D_FF), BF16),
        compiler_params=_params(("arbitrary", "arbitrary"), 58),
    )(sbe, sbi, nsub, tok3, tok3, h_packed, w_gate_up, w_gate_up, bgu, bgu)


def _moe_down_kernel(sbe_ref, sbi_ref, nsub_ref, h_ref, w_ref, b_ref, o_ref, wb):
    s = pl.program_id(0)
    nsub = nsub_ref[s]

    def first(r0, n):
        w = w_ref[...].astype(BF16)
        wb[...] = w
        rows = pl.ds(r0, n)
        o_ref[rows, :] = jnp.dot(h_ref[rows, :], w, preferred_element_type=F32) + b_ref[...]

    def rest(r0, n):
        rows = pl.ds(r0, n)
        o_ref[rows, :] = jnp.dot(h_ref[rows, :], wb[...], preferred_element_type=F32) + b_ref[...]

    @pl.when(nsub > 0)
    def _():
        _for_valid_rows(nsub, first, rest)


def _moe_down(h_sorted, sbe, sbi, nsub, w_down, b_down, layer):
    n_c = D_MODEL // MOE_NC
    last_c = n_c - 1

    def c_eff(s, c, nsub_ref):
        return jnp.where(nsub_ref[s] > 0, c, last_c)

    grid_spec = pltpu.PrefetchScalarGridSpec(
        num_scalar_prefetch=3,
        grid=(MOE_NSB, n_c),
        in_specs=[
            pl.BlockSpec((MOE_R, MOE_D_FF), lambda s, c, e, i, n: (i[s], 0)),
            pl.BlockSpec((None, None, MOE_D_FF, MOE_NC), lambda s, c, e, i, n: (layer, e[s], 0, c_eff(s, c, n))),
            pl.BlockSpec((None, None, 1, MOE_NC), lambda s, c, e, i, n: (layer, e[s], 0, c_eff(s, c, n))),
        ],
        out_specs=pl.BlockSpec((MOE_R, MOE_NC), lambda s, c, e, i, n: (i[s], c_eff(s, c, n))),
        scratch_shapes=[pltpu.VMEM((MOE_D_FF, MOE_NC), BF16)],
    )
    return pl.pallas_call(
        _moe_down_kernel,
        name="moe_down",
        grid_spec=grid_spec,
        out_shape=jax.ShapeDtypeStruct((MOE_CAP, D_MODEL), F32),
        compiler_params=_params(("arbitrary", "arbitrary"), 58),
    )(sbe, sbi, nsub, h_sorted, w_down, b_down.reshape(DEPTH, N_EXPERTS, 1, D_MODEL))


def _route(top_idx):
    flat_e = top_idx.reshape(-1)
    onehot = (flat_e[:, None] == jnp.arange(N_EXPERTS, dtype=I32)[None, :]).astype(I32)
    csum = jnp.cumsum(onehot, axis=0)
    rank = jnp.take_along_axis(csum, flat_e[:, None], axis=1)[:, 0] - 1
    counts = csum[-1]
    nsb = (counts + MOE_R - 1) // MOE_R
    sb_end = jnp.cumsum(nsb)
    sb_off = sb_end - nsb
    dest = sb_off[flat_e] * MOE_R + rank
    n_real = sb_end[-1]
    s = jnp.arange(MOE_NSB, dtype=I32)
    s_eff = jnp.minimum(s, n_real - 1)
    sbe = jnp.minimum(jnp.searchsorted(sb_end, s_eff, side='right'), N_EXPERTS - 1).astype(I32)
    valid = jnp.clip(counts[sbe] - (s_eff - sb_off[sbe]) * MOE_R, 0, MOE_R)
    valid = jnp.where(s < n_real, valid, 0)
    nsub = ((valid + MOE_SUB - 1) // MOE_SUB).astype(I32)
    dest = dest.astype(I32)
    row_tok = jnp.zeros((MOE_CAP,), I32).at[dest].set(jnp.arange(N_ASSIGN, dtype=I32) // TOP_K)
    return dest, row_tok, sbe, s_eff.astype(I32), nsub


def _moe_experts(h_packed, top_idx, w_gate_up, b_gate_up, w_down, b_down, layer):
    dest, row_tok, sbe, sbi, nsub = _route(top_idx)
    h_sorted = _moe_up(h_packed, row_tok, sbe, sbi, nsub, w_gate_up, b_gate_up, layer)
    return _moe_down(h_sorted, sbe, sbi, nsub, w_down, b_down, layer), dest


def _pb_rows(mod_piece):
    idx = jnp.array([0] * N_PROMPT_PB + list(range(1, 1 + DEC_BATCH)), I32)
    return mod_piece[idx][:, None, :]


def kernel(x_prompt, x_sample, cache_a_k, cache_a_v, cache_b_k, cache_b_v, cache_c_k, cache_c_v, c, c_ctx, w_ada, b_ada, g_pre_mix, g_post_mix, g_pre_ffn, g_post_ffn, w_in_even, w_out_even, rpb_a, q_norm_b, k_norm_b, w_in_odd, w_out_odd, sink_c, w_router, b_router, w_gate_up, b_gate_up, w_down, b_down):
    x = jnp.concatenate([x_prompt.reshape(N_PROMPT, D_MODEL), x_sample.reshape(N_SAMPLE, D_MODEL)], axis=0)
    cond8 = jnp.concatenate([c_ctx[None, :], c, jnp.zeros((8 - 1 - DEC_BATCH, D_MODEL), F32)], axis=0)
    mods = _ada_mod(cond8, w_ada, b_ada)
    cos_tab, sin_tab = _rope_tables()
    no_sink = jnp.full((max(A_HEADS, B_HEADS),), -jnp.inf, F32)
    states = {}

    for l in range(DEPTH):
        j = l // 2
        m = [_pb_rows(mods[l, :, k * D_MODEL:(k + 1) * D_MODEL]) for k in range(6)]
        if l % 2 == 0:
            w_in = w_in_even[j].astype(BF16)
            w_out = w_out_even[j].astype(BF16)
            nb = EVEN_IN // IN_BN
            qb0 = 3 * A_HEADS * HEAD_DIM // IN_BN
            kb0 = qb0 + B_HEADS * HEAD_DIM // IN_BN
            vb0 = kb0 + B_KV_HEADS * HEAD_DIM // IN_BN
            modes = jnp.array([MODE_PLAIN] * qb0 + [MODE_NORM_Q] * (kb0 - qb0) + [MODE_NORM_K] * (vb0 - kb0)
                              + [MODE_PLAIN] * (nb - vb0), I32)
            gains = jnp.concatenate([q_norm_b[j][None], k_norm_b[j][None], jnp.ones((6, HEAD_DIM), F32)], axis=0)
        else:
            w_in = w_in_odd[j].astype(BF16)
            w_out = w_out_odd[j].astype(BF16)
            nb = ODD_IN // IN_BN
            vc0 = (C_HEADS + C_KV_HEADS) * HEAD_DIM // IN_BN
            modes = jnp.array([MODE_ROPE] * vc0 + [MODE_PLAIN] * (nb - vc0), I32)
            gains = jnp.ones((8, HEAD_DIM), F32)

        proj = _in_proj(x, g_pre_mix[l], m[0], m[1], w_in, modes, gains, cos_tab, sin_tab)

        if l % 2 == 0:
            ah = A_HEADS * HEAD_DIM
            ctx_a = _ctx_attention(proj, no_sink, 0, ah, 2 * ah, A_HEADS, A_HEADS, 8)
            ctx_b = _ctx_attention(proj, no_sink, 3 * ah, 3 * ah + B_HEADS * HEAD_DIM,
                                   3 * ah + (B_HEADS + B_KV_HEADS) * HEAD_DIM, B_HEADS, B_KV_HEADS, B_KV_HEADS)
            lat_a = _na_attention(proj, cache_a_k, cache_a_v, rpb_a[j], j)
            lat_b = _dense_attention(proj, cache_b_k, cache_b_v, j)
            prompt_parts, latent_parts = [ctx_a, ctx_b], [lat_a, lat_b]
            pp = proj[:N_PROMPT]
            kb_c = 3 * ah + B_HEADS * HEAD_DIM
            vb_c = kb_c + B_KV_HEADS * HEAD_DIM
            states.setdefault('ak', []).append(pp[:, ah:2 * ah].reshape(BATCH, SEQ, A_HEADS, HEAD_DIM))
            states.setdefault('av', []).append(pp[:, 2 * ah:3 * ah].reshape(BATCH, SEQ, A_HEADS, HEAD_DIM))
            states.setdefault('bk', []).append(pp[:, kb_c:vb_c].reshape(BATCH, SEQ, B_KV_HEADS, HEAD_DIM))
            states.setdefault('bv', []).append(pp[:, vb_c:].reshape(BATCH, SEQ, B_KV_HEADS, HEAD_DIM))
        else:
            qc = C_HEADS * HEAD_DIM
            kvw = C_KV_HEADS * HEAD_DIM
            ctx_c = _ctx_attention(proj, sink_c[j], 0, qc, qc + kvw, C_HEADS, C_KV_HEADS, C_KV_HEADS)
            lat_c = _window_attention(proj, cache_c_k, cache_c_v, sink_c[j], j)
            prompt_parts, latent_parts = [ctx_c], [lat_c]
            pp = proj[:N_PROMPT]
            states.setdefault('ck', []).append(pp[:, qc:qc + kvw].reshape(BATCH, SEQ, C_KV_HEADS, HEAD_DIM))
            states.setdefault('cv', []).append(pp[:, qc + kvw:].reshape(BATCH, SEQ, C_KV_HEADS, HEAD_DIM))

        y = _out_proj(prompt_parts, latent_parts, w_out)
        wr = jnp.pad(w_router[l], ((0, 0), (0, LANES - N_EXPERTS)))
        wr_hi = wr.astype(BF16)
        wr_lo = (wr - wr_hi.astype(F32)).astype(BF16)
        br = jnp.concatenate([b_router[l], jnp.full((LANES - N_EXPERTS,), NEG_BIG, F32)])[None, :]
        x, h_ffn, top_i, top_g = _post_mix(y, x, m[2], g_post_mix[l], g_pre_ffn[l], m[3], m[4], wr_hi, wr_lo, br)
        expert_out, dest = _moe_experts(h_ffn, top_i[:, :TOP_K], w_gate_up, b_gate_up, w_down, b_down, l)
        x = _combine_post_ffn(expert_out, dest, top_g, x, m[5], g_post_ffn[l], split_out=(l == DEPTH - 1))

    y_prompt = x[0].reshape(BATCH, SEQ, D_MODEL)
    y_sample = x[1].reshape(DEC_BATCH, DEC_SEQ, D_MODEL)
    st = {k: jnp.stack(v, axis=1) for k, v in states.items()}
    return (y_prompt, y_sample, st['ak'], st['av'], st['bk'], st['bv'], st['ck'], st['cv'])
```

```python
import functools

import jax
import jax.numpy as jnp
import numpy as np
from jax import lax
from jax.experimental import pallas as pl
from jax.experimental.pallas import tpu as pltpu

F32 = jnp.float32
BF16 = jnp.bfloat16
I32 = jnp.int32
U32 = jnp.uint32

D_MODEL = 4096
BATCH = 16
SEQ = 256
DEPTH = 2
DEC_BATCH = 4
DEC_SEQ = 2048
PAST_LEN = 512
GRID_W = 64
GRID_ROWS = DEC_SEQ // GRID_W
HEAD_DIM = 128
A_HEADS = 16
B_HEADS = 16
B_KV_HEADS = 4
C_HEADS = 32
C_KV_HEADS = 4
WIN_H = 8
WIN_W = 16
C_WINDOW = 128
N_EXPERTS = 32
TOP_K = 4
MOE_D_FF = 2048
SWIGLU_LIMIT = 7.0
SWIGLU_ALPHA = 1.702
ROPE_THETA = 10000.0
RMS_EPS = 1e-6
ATTN_SCALE = HEAD_DIM ** -0.5
EVEN_IN = (3 * A_HEADS + B_HEADS + 2 * B_KV_HEADS) * HEAD_DIM
ODD_IN = (C_HEADS + 2 * C_KV_HEADS) * HEAD_DIM

N_PROMPT = BATCH * SEQ
N_SAMPLE = DEC_BATCH * DEC_SEQ
N_TOK = N_PROMPT + N_SAMPLE
PB_TOK = DEC_SEQ
N_PB = N_TOK // PB_TOK
N_PROMPT_PB = N_PROMPT // PB_TOK

LANES = 128
MIB = 1024 * 1024

MOE_R = 2048
MOE_SUB = 256
N_ASSIGN = N_TOK * TOP_K
MOE_NSB = N_ASSIGN // MOE_R + N_EXPERTS
MOE_CAP = MOE_NSB * MOE_R
MOE_FC = 256
MOE_NC = 1024

NEG_BIG = -1e30
HALF_D = D_MODEL // 2
HI_MASK = np.uint32(0xFFFF0000)
SHIFT16 = np.uint32(16)


def _params(sem, vmem_mib):
    return pltpu.CompilerParams(dimension_semantics=sem, vmem_limit_bytes=vmem_mib * MIB)


def _ada_kernel(c_ref, w_ref, b_ref, o_ref):
    c = c_ref[...]
    s = (c / (1.0 + jnp.exp(-c))).astype(BF16)
    o_ref[...] = jnp.dot(s, w_ref[...].astype(BF16), preferred_element_type=F32) + b_ref[...]


def _ada_mod(cond8, w_ada, b_ada):
    bn = 512
    n_out = 6 * D_MODEL
    return pl.pallas_call(
        _ada_kernel,
        name="ada_mod",
        grid=(DEPTH, n_out // bn),
        in_specs=[
            pl.BlockSpec((8, D_MODEL), lambda l, j: (0, 0)),
            pl.BlockSpec((None, D_MODEL, bn), lambda l, j: (l, 0, j)),
            pl.BlockSpec((None, 1, bn), lambda l, j: (l, 0, j)),
        ],
        out_specs=pl.BlockSpec((None, 8, bn), lambda l, j: (l, 0, j)),
        out_shape=jax.ShapeDtypeStruct((DEPTH, 8, n_out), F32),
        compiler_params=_params(("arbitrary", "arbitrary"), 40),
    )(cond8, w_ada, b_ada.reshape(DEPTH, 1, n_out))


IN_BM = 512
IN_BN = 512
MODE_PLAIN, MODE_NORM_Q, MODE_NORM_K, MODE_ROPE = 0, 1, 2, 3


def _inproj_kernel(mode_ref, x_ref, g_ref, sh_ref, sc_ref, w_ref, gain_ref, cos_ref, sin_ref,
                   o_ref, h_ref):
    j = pl.program_id(1)

    @pl.when(j == 0)
    def _():
        x = x_ref[...]
        r = lax.rsqrt(jnp.mean(x * x, axis=-1, keepdims=True) + RMS_EPS)
        h = (x * r * g_ref[...]) * (1.0 + sc_ref[...]) + sh_ref[...]
        h_ref[...] = h.astype(BF16)

    acc = jnp.dot(h_ref[...], w_ref[...], preferred_element_type=F32)
    mode = mode_ref[j]

    @pl.when(mode == MODE_PLAIN)
    def _():
        o_ref[...] = acc

    def rope_store(hh, y):
        lane = lax.broadcasted_iota(I32, (IN_BM, LANES), 1)
        sw = jnp.where((lane % 64) < 32, pltpu.roll(y, 96, 1), pltpu.roll(y, 32, 1))
        o_ref[:, hh * HEAD_DIM:(hh + 1) * HEAD_DIM] = y * cos_ref[...] + sw * sin_ref[...]

    @pl.when(mode == MODE_ROPE)
    def _():
        for hh in range(IN_BN // HEAD_DIM):
            rope_store(hh, acc[:, hh * HEAD_DIM:(hh + 1) * HEAD_DIM])

    @pl.when((mode == MODE_NORM_Q) | (mode == MODE_NORM_K))
    def _():
        gain = jnp.where(mode == MODE_NORM_Q, gain_ref[0:1, :], gain_ref[1:2, :])
        for hh in range(IN_BN // HEAD_DIM):
            xh = acc[:, hh * HEAD_DIM:(hh + 1) * HEAD_DIM]
            rs = lax.rsqrt(jnp.mean(xh * xh, axis=-1, keepdims=True) + RMS_EPS)
            rope_store(hh, xh * rs * gain)


def _in_proj(x, g_pre, shift, scale, w_bf16, modes, gains, cos_tab, sin_tab):
    n = w_bf16.shape[1]
    n_i = N_TOK // IN_BM
    n_prompt_blocks = N_PROMPT // IN_BM
    blocks_per_seq = DEC_SEQ // IN_BM

    def tab_idx(i, j, m):
        return (jnp.where(i < n_prompt_blocks, 0, 1 + (i - n_prompt_blocks) % blocks_per_seq), 0)

    grid_spec = pltpu.PrefetchScalarGridSpec(
        num_scalar_prefetch=1,
        grid=(n_i, n // IN_BN),
        in_specs=[
            pl.BlockSpec((IN_BM, D_MODEL), lambda i, j, m: (i, 0)),
            pl.BlockSpec((1, D_MODEL), lambda i, j, m: (0, 0)),
            pl.BlockSpec((None, 1, D_MODEL), lambda i, j, m: (i * IN_BM // PB_TOK, 0, 0)),
            pl.BlockSpec((None, 1, D_MODEL), lambda i, j, m: (i * IN_BM // PB_TOK, 0, 0)),
            pl.BlockSpec((D_MODEL, IN_BN), lambda i, j, m: (0, j)),
            pl.BlockSpec((8, LANES), lambda i, j, m: (0, 0)),
            pl.BlockSpec((IN_BM, LANES), tab_idx),
            pl.BlockSpec((IN_BM, LANES), tab_idx),
        ],
        out_specs=pl.BlockSpec((IN_BM, IN_BN), lambda i, j, m: (i, j)),
        scratch_shapes=[pltpu.VMEM((IN_BM, D_MODEL), BF16)],
    )
    return pl.pallas_call(
        _inproj_kernel,
        name="in_proj",
        grid_spec=grid_spec,
        out_shape=jax.ShapeDtypeStruct((N_TOK, n), F32),
        compiler_params=_params(("arbitrary", "arbitrary"), 48),
    )(modes, x, g_pre.reshape(1, D_MODEL), shift, scale, w_bf16, gains, cos_tab, sin_tab)


def _rope_tables():
    t = jnp.arange(DEC_SEQ)
    row = (t // GRID_W).astype(F32)
    col = (t % GRID_W).astype(F32)
    n_freq = HEAD_DIM // 4
    inv = ROPE_THETA ** (-jnp.arange(n_freq, dtype=F32) / n_freq)
    ar = row[:, None] * inv
    ac = col[:, None] * inv
    cos = jnp.concatenate([jnp.cos(ar), jnp.cos(ar), jnp.cos(ac), jnp.cos(ac)], axis=-1)
    sin = jnp.concatenate([-jnp.sin(ar), jnp.sin(ar), -jnp.sin(ac), jnp.sin(ac)], axis=-1)
    cos = jnp.concatenate([jnp.ones((IN_BM, HEAD_DIM), F32), cos], axis=0)
    sin = jnp.concatenate([jnp.zeros((IN_BM, HEAD_DIM), F32), sin], axis=0)
    return cos, sin


MM_BM = 1024
MM_BN = 512


MM_PROMPT_BLOCKS = N_PROMPT // MM_BM


def _out_proj_kernel(*refs, n_parts):
    prompt_refs = refs[:n_parts]
    latent_refs = refs[n_parts:2 * n_parts]
    w_ref, o_ref = refs[2 * n_parts:]
    i = pl.program_id(0)

    def run(parts):
        acc = None
        k0 = 0
        for r in parts:
            kk = r.shape[1]
            d = jnp.dot(r[...], w_ref[k0:k0 + kk, :], preferred_element_type=F32)
            acc = d if acc is None else acc + d
            k0 += kk
        o_ref[...] = acc

    @pl.when(i < MM_PROMPT_BLOCKS)
    def _():
        run(prompt_refs)

    @pl.when(i >= MM_PROMPT_BLOCKS)
    def _():
        run(latent_refs)


def _out_proj(prompt_parts, latent_parts, w_bf16):
    k, n = w_bf16.shape
    n_parts = len(prompt_parts)
    last_prompt = MM_PROMPT_BLOCKS - 1
    in_specs = ([pl.BlockSpec((MM_BM, p.shape[1]), lambda i, j: (jnp.minimum(i, last_prompt), 0))
                 for p in prompt_parts]
                + [pl.BlockSpec((MM_BM, p.shape[1]), lambda i, j: (jnp.maximum(i - MM_PROMPT_BLOCKS, 0), 0))
                   for p in latent_parts]
                + [pl.BlockSpec((k, MM_BN), lambda i, j: (0, j))])
    return pl.pallas_call(
        functools.partial(_out_proj_kernel, n_parts=n_parts),
        name="out_proj",
        grid=(N_TOK // MM_BM, n // MM_BN),
        in_specs=in_specs,
        out_specs=pl.BlockSpec((MM_BM, MM_BN), lambda i, j: (i, j)),
        out_shape=jax.ShapeDtypeStruct((N_TOK, n), F32),
        compiler_params=_params(("arbitrary", "arbitrary"), 56),
    )(*prompt_parts, *latent_parts, w_bf16)


def _attend_heads(scores, values, sinks=None):
    n = len(scores)
    ms = []
    for h in range(n):
        m = jnp.max(scores[h][0], axis=-1, keepdims=True)
        for s in scores[h][1:]:
            m = jnp.maximum(m, jnp.max(s, axis=-1, keepdims=True))
        if sinks is not None:
            m = jnp.maximum(m, sinks[h])
        ms.append(m)
    ps = [[jnp.exp(s - ms[h]) for s in scores[h]] for h in range(n)]
    dens = []
    for h in range(n):
        d = jnp.sum(ps[h][0], axis=-1, keepdims=True)
        for p in ps[h][1:]:
            d = d + jnp.sum(p, axis=-1, keepdims=True)
        if sinks is not None:
            d = d + jnp.exp(sinks[h] - ms[h])
        dens.append(d)
    outs = []
    for h in range(n):
        o = jnp.dot(ps[h][0].astype(BF16), values[h][0], preferred_element_type=F32)
        for p, v in zip(ps[h][1:], values[h][1:]):
            o = o + jnp.dot(p.astype(BF16), v, preferred_element_type=F32)
        outs.append(o / dens[h])
    return outs


def _qk(q, k):
    return lax.dot_general(q, k, (((1,), (1,)), ((), ())), preferred_element_type=F32) * ATTN_SCALE


def _ctx_attn_kernel(sink_ref, q_ref, k_ref, v_ref, o_ref, *, nkv, g):
    kb = pl.program_id(1)
    for kv in range(nkv):
        k = k_ref[:, kv * HEAD_DIM:(kv + 1) * HEAD_DIM].astype(BF16)
        v = v_ref[:, kv * HEAD_DIM:(kv + 1) * HEAD_DIM].astype(BF16)
        cols = [slice((kv * g + gi) * HEAD_DIM, (kv * g + gi + 1) * HEAD_DIM) for gi in range(g)]
        scores = [[_qk(q_ref[:, cs].astype(BF16), k)] for cs in cols]
        sinks = [sink_ref[(kb * nkv + kv) * g + gi] for gi in range(g)]
        for cs, o in zip(cols, _attend_heads(scores, [[v]] * g, sinks)):
            o_ref[:, cs] = o.astype(o_ref.dtype)


def _ctx_attention(proj, sink, q_col, k_col, v_col, hq, hkv, nkv):
    g = hq // hkv
    qw = nkv * g * HEAD_DIM
    kw = nkv * HEAD_DIM
    return pl.pallas_call(
        functools.partial(_ctx_attn_kernel, nkv=nkv, g=g),
        name="ctx_attn",
        grid=(BATCH, hkv // nkv),
        in_specs=[
            pl.BlockSpec(memory_space=pltpu.SMEM),
            pl.BlockSpec((SEQ, qw), lambda b, h: (b, q_col // qw + h)),
            pl.BlockSpec((SEQ, kw), lambda b, h: (b, k_col // kw + h)),
            pl.BlockSpec((SEQ, kw), lambda b, h: (b, v_col // kw + h)),
        ],
        out_specs=pl.BlockSpec((SEQ, qw), lambda b, h: (b, h)),
        out_shape=jax.ShapeDtypeStruct((N_PROMPT, hq * HEAD_DIM), BF16),
        compiler_params=_params(("arbitrary", "arbitrary"), 40),
    )(sink, proj, proj, proj)


DENSE_BQ = 256


def _cache_bf16(step, srcs, dsts):
    @pl.when(step == 0)
    def _():
        for src, dst in zip(srcs, dsts):
            dst[...] = src[...].astype(BF16)


def _kv_scratch(width):
    return [pltpu.VMEM((DEC_SEQ, width), BF16), pltpu.VMEM((DEC_SEQ, width), BF16),
            pltpu.VMEM((PAST_LEN, width), BF16), pltpu.VMEM((PAST_LEN, width), BF16)]


def _dense_attn_kernel(q_ref, k_ref, v_ref, kc_ref, vc_ref, o_ref, kb, vb, kcb, vcb, *, g):
    _cache_bf16(pl.program_id(2), (k_ref, v_ref, kc_ref, vc_ref), (kb, vb, kcb, vcb))
    k = kb[...]
    v = vb[...]
    kc = kcb[...]
    vc = vcb[...]
    cols = [slice(gi * HEAD_DIM, (gi + 1) * HEAD_DIM) for gi in range(g)]
    qs = [q_ref[:, cs].astype(BF16) for cs in cols]
    scores = [[_qk(q, k), _qk(q, kc)] for q in qs]
    for cs, o in zip(cols, _attend_heads(scores, [[v, vc]] * g)):
        o_ref[:, cs] = o.astype(o_ref.dtype)


def _dense_attention(proj, cache_k, cache_v, layer_j):
    g = B_HEADS // B_KV_HEADS
    qw = g * HEAD_DIM
    nq = DEC_SEQ // DENSE_BQ
    q_col = 3 * A_HEADS * HEAD_DIM
    k_col = q_col + B_HEADS * HEAD_DIM
    v_col = k_col + B_KV_HEADS * HEAD_DIM
    row0 = N_PROMPT // DENSE_BQ
    ck = cache_k.reshape(DEC_BATCH, -1, PAST_LEN, B_KV_HEADS * HEAD_DIM)
    cv = cache_v.reshape(DEC_BATCH, -1, PAST_LEN, B_KV_HEADS * HEAD_DIM)
    return pl.pallas_call(
        functools.partial(_dense_attn_kernel, g=g),
        name="dense_attn",
        grid=(DEC_BATCH, B_KV_HEADS, nq),
        in_specs=[
            pl.BlockSpec((DENSE_BQ, qw), lambda b, h, i: (row0 + b * nq + i, q_col // qw + h)),
            pl.BlockSpec((DEC_SEQ, HEAD_DIM), lambda b, h, i: (N_PROMPT_PB + b, k_col // HEAD_DIM + h)),
            pl.BlockSpec((DEC_SEQ, HEAD_DIM), lambda b, h, i: (N_PROMPT_PB + b, v_col // HEAD_DIM + h)),
            pl.BlockSpec((None, None, PAST_LEN, HEAD_DIM), lambda b, h, i: (b, layer_j, 0, h)),
            pl.BlockSpec((None, None, PAST_LEN, HEAD_DIM), lambda b, h, i: (b, layer_j, 0, h)),
        ],
        out_specs=pl.BlockSpec((DENSE_BQ, qw), lambda b, h, i: (b * nq + i, h)),
        scratch_shapes=_kv_scratch(HEAD_DIM),
        out_shape=jax.ShapeDtypeStruct((N_SAMPLE, B_HEADS * HEAD_DIM), BF16),
        compiler_params=_params(("arbitrary", "arbitrary", "arbitrary"), 48),
    )(proj, proj, proj, ck, cv)


WIN_BQ = 256
WIN_KEYS = WIN_BQ + 2 * C_WINDOW


def _window_attn_kernel(sink_ref, q_ref, k_ref, v_ref, kc_ref, vc_ref, o_ref, kb, vb, kcb, vcb, *, g):
    h = pl.program_id(1)
    i = pl.program_id(2)
    _cache_bf16(i, (k_ref, v_ref, kc_ref, vc_ref), (kb, vb, kcb, vcb))
    q0 = i * WIN_BQ
    k0 = pl.multiple_of(jnp.clip(q0 - C_WINDOW, 0, DEC_SEQ - WIN_KEYS), C_WINDOW)
    k = kb[pl.ds(k0, WIN_KEYS), :]
    v = vb[pl.ds(k0, WIN_KEYS), :]
    kc = kcb[...]
    vc = vcb[...]
    q_pos = q0 + lax.broadcasted_iota(I32, (WIN_BQ, WIN_KEYS), 0)
    k_pos = k0 + lax.broadcasted_iota(I32, (WIN_BQ, WIN_KEYS), 1)
    ok = jnp.abs(q_pos - k_pos) <= C_WINDOW
    cols = [slice(gi * HEAD_DIM, (gi + 1) * HEAD_DIM) for gi in range(g)]
    qs = [q_ref[:, cs].astype(BF16) for cs in cols]
    scores = [[jnp.where(ok, _qk(q, k), -jnp.inf), _qk(q, kc)] for q in qs]
    sinks = [sink_ref[h * g + gi] for gi in range(g)]
    for cs, o in zip(cols, _attend_heads(scores, [[v, vc]] * g, sinks)):
        o_ref[:, cs] = o.astype(o_ref.dtype)


def _window_attention(proj, cache_k, cache_v, sink, layer_j):
    g = C_HEADS // C_KV_HEADS
    qw = g * HEAD_DIM
    nq = DEC_SEQ // WIN_BQ
    k_col = C_HEADS * HEAD_DIM
    v_col = k_col + C_KV_HEADS * HEAD_DIM
    row0 = N_PROMPT // WIN_BQ
    ck = cache_k.reshape(DEC_BATCH, -1, PAST_LEN, C_KV_HEADS * HEAD_DIM)
    cv = cache_v.reshape(DEC_BATCH, -1, PAST_LEN, C_KV_HEADS * HEAD_DIM)
    return pl.pallas_call(
        functools.partial(_window_attn_kernel, g=g),
        name="window_attn",
        grid=(DEC_BATCH, C_KV_HEADS, nq),
        in_specs=[
            pl.BlockSpec(memory_space=pltpu.SMEM),
            pl.BlockSpec((WIN_BQ, qw), lambda b, h, i: (row0 + b * nq + i, h)),
            pl.BlockSpec((DEC_SEQ, HEAD_DIM), lambda b, h, i: (N_PROMPT_PB + b, k_col // HEAD_DIM + h)),
            pl.BlockSpec((DEC_SEQ, HEAD_DIM), lambda b, h, i: (N_PROMPT_PB + b, v_col // HEAD_DIM + h)),
            pl.BlockSpec((None, None, PAST_LEN, HEAD_DIM), lambda b, h, i: (b, layer_j, 0, h)),
            pl.BlockSpec((None, None, PAST_LEN, HEAD_DIM), lambda b, h, i: (b, layer_j, 0, h)),
        ],
        out_specs=pl.BlockSpec((WIN_BQ, qw), lambda b, h, i: (b * nq + i, h)),
        scratch_shapes=_kv_scratch(HEAD_DIM),
        out_shape=jax.ShapeDtypeStruct((N_SAMPLE, C_HEADS * HEAD_DIM), BF16),
        compiler_params=_params(("arbitrary", "arbitrary", "arbitrary"), 48),
    )(sink, proj, proj, proj, ck, cv)


NA_HC = 4
NA_KEYS = WIN_H * GRID_W


def _na_row_start(i):
    return jnp.clip(i - WIN_H // 2, 0, GRID_ROWS - WIN_H)


def _na_attn_kernel(q_ref, k_ref, v_ref, kc_ref, vc_ref, bias_ref, o_ref, kb, vb, kcb, vcb):
    i = pl.program_id(2)
    _cache_bf16(i, (k_ref, v_ref, kc_ref, vc_ref), (kb, vb, kcb, vcb))
    r0 = pl.multiple_of(_na_row_start(i) * GRID_W, GRID_W)
    rows = pl.ds(r0, NA_KEYS)
    heads = [slice(hh * HEAD_DIM, (hh + 1) * HEAD_DIM) for hh in range(NA_HC)]
    qs = [q_ref[:, cs].astype(BF16) for cs in heads]
    scores = [[_qk(q, kb[rows, cs]) + bias_ref[hh], _qk(q, kcb[:, cs])] for hh, (q, cs) in enumerate(zip(qs, heads))]
    values = [[vb[rows, cs], vcb[:, cs]] for cs in heads]
    for cs, o in zip(heads, _attend_heads(scores, values)):
        o_ref[:, cs] = o.astype(o_ref.dtype)


def _na_bias_table(rpb):
    qc = jnp.arange(GRID_W)[:, None]
    kc = jnp.arange(GRID_W)[None, :]
    ws = jnp.clip(qc - WIN_W // 2, 0, GRID_W - WIN_W)
    ok = (kc >= ws) & (kc < ws + WIN_W)
    dc = jnp.clip(kc - qc + WIN_W - 1, 0, 2 * WIN_W - 2)
    t = rpb[:, :, dc]
    t = jnp.where(ok[None, None], t, -jnp.inf)
    dr = jnp.arange(WIN_H)[:, None] + jnp.arange(WIN_H)[None, :]
    b = t[:, dr]
    b = b.transpose(0, 1, 3, 2, 4).reshape(A_HEADS, WIN_H, GRID_W, NA_KEYS)
    return b.astype(F32)


def _na_attention(proj, cache_k, cache_v, rpb, layer_j):
    cw = NA_HC * HEAD_DIM
    k_col = A_HEADS * HEAD_DIM
    v_col = 2 * A_HEADS * HEAD_DIM
    row0 = N_PROMPT // GRID_W
    ck = cache_k.reshape(DEC_BATCH, -1, PAST_LEN, A_HEADS * HEAD_DIM)
    cv = cache_v.reshape(DEC_BATCH, -1, PAST_LEN, A_HEADS * HEAD_DIM)
    bias = _na_bias_table(rpb)

    def bias_idx(b, h, i):
        return (h, _na_row_start(i) - i + WIN_H - 1, 0, 0)

    return pl.pallas_call(
        _na_attn_kernel,
        name="na_attn",
        grid=(DEC_BATCH, A_HEADS // NA_HC, GRID_ROWS),
        in_specs=[
            pl.BlockSpec((GRID_W, cw), lambda b, h, i: (row0 + b * GRID_ROWS + i, h)),
            pl.BlockSpec((DEC_SEQ, cw), lambda b, h, i: (N_PROMPT_PB + b, k_col // cw + h)),
            pl.BlockSpec((DEC_SEQ, cw), lambda b, h, i: (N_PROMPT_PB + b, v_col // cw + h)),
            pl.BlockSpec((None, None, PAST_LEN, cw), lambda b, h, i: (b, layer_j, 0, h)),
            pl.BlockSpec((None, None, PAST_LEN, cw), lambda b, h, i: (b, layer_j, 0, h)),
            pl.BlockSpec((NA_HC, None, GRID_W, NA_KEYS), bias_idx),
        ],
        out_specs=pl.BlockSpec((GRID_W, cw), lambda b, h, i: (b * GRID_ROWS + i, h)),
        scratch_shapes=_kv_scratch(cw),
        out_shape=jax.ShapeDtypeStruct((N_SAMPLE, A_HEADS * HEAD_DIM), BF16),
        compiler_params=_params(("arbitrary", "arbitrary", "arbitrary"), 48),
    )(proj, proj, proj, ck, cv, bias)


EP_BM = 256


def _rms(x):
    return x * lax.rsqrt(jnp.mean(x * x, axis=-1, keepdims=True) + RMS_EPS)


def _post_mix_kernel(y_ref, x_ref, gate_ref, gpost_ref, gpre_ref, sh_ref, sc_ref,
                     wr_hi_ref, wr_lo_ref, br_ref, xo_ref, h_ref, ti_ref, tg_ref):
    x_new = x_ref[...] + gate_ref[...] * (_rms(y_ref[...]) * gpost_ref[...])
    xo_ref[...] = x_new
    h = (_rms(x_new) * gpre_ref[...]) * (1.0 + sc_ref[...]) + sh_ref[...]
    h_hi = h.astype(BF16)
    h_hi32 = h_hi.astype(F32)
    bits = pltpu.bitcast(h_hi32, U32)
    h_ref[...] = (bits[:, :HALF_D] >> SHIFT16) | (bits[:, HALF_D:] & HI_MASK)
    h_lo = (h - h_hi32).astype(BF16)
    w_hi = wr_hi_ref[...]
    logits = (jnp.dot(h_hi, w_hi, preferred_element_type=F32)
              + jnp.dot(h_lo, w_hi, preferred_element_type=F32)
              + jnp.dot(h_hi, wr_lo_ref[...], preferred_element_type=F32)) + br_ref[...]
    lane = lax.broadcasted_iota(I32, (EP_BM, LANES), 1).astype(F32)
    vals, idxs = [], []
    l = logits
    for _ in range(TOP_K):
        m = jnp.max(l, axis=-1, keepdims=True)
        idx = jnp.min(jnp.where(l == m, lane, float(LANES)), axis=-1, keepdims=True)
        vals.append(m)
        idxs.append(idx)
        l = jnp.where(lane == idx, -jnp.inf, l)
    es = [jnp.exp(v - vals[0]) for v in vals]
    den = es[0] + es[1] + es[2] + es[3]
    ti = jnp.zeros((EP_BM, LANES), F32)
    tg = jnp.zeros((EP_BM, LANES), F32)
    for k in range(TOP_K):
        ti = jnp.where(lane == float(k), idxs[k], ti)
        tg = jnp.where(lane == float(k), es[k] / den, tg)
    ti_ref[...] = ti.astype(I32)
    tg_ref[...] = tg


def _post_mix(y, x, gate, g_post, g_pre, shift, scale, wr_hi, wr_lo, br):
    row = pl.BlockSpec((EP_BM, D_MODEL), lambda i: (i, 0))
    vec = pl.BlockSpec((1, D_MODEL), lambda i: (0, 0))
    mod = pl.BlockSpec((None, 1, D_MODEL), lambda i: (i * EP_BM // PB_TOK, 0, 0))
    wr = pl.BlockSpec((D_MODEL, LANES), lambda i: (0, 0))
    small = pl.BlockSpec((EP_BM, LANES), lambda i: (i, 0))
    packed = pl.BlockSpec((EP_BM, HALF_D), lambda i: (i, 0))
    return pl.pallas_call(
        _post_mix_kernel,
        name="post_mix",
        grid=(N_TOK // EP_BM,),
        in_specs=[row, row, mod, vec, vec, mod, mod, wr, wr, pl.BlockSpec((1, LANES), lambda i: (0, 0))],
        out_specs=[row, packed, small, small],
        out_shape=[jax.ShapeDtypeStruct((N_TOK, D_MODEL), F32),
                   jax.ShapeDtypeStruct((N_TOK, HALF_D), U32),
                   jax.ShapeDtypeStruct((N_TOK, LANES), I32),
                   jax.ShapeDtypeStruct((N_TOK, LANES), F32)],
        compiler_params=_params(("arbitrary",), 48),
    )(y, x, gate, g_post.reshape(1, D_MODEL), g_pre.reshape(1, D_MODEL), shift, scale, wr_hi, wr_lo, br)


CB_TOK = 128
CB_ROWS = CB_TOK * TOP_K
CB_STEPS = N_TOK // CB_TOK


CB_PROMPT_STEPS = N_PROMPT // CB_TOK


def _combine_kernel(dest_ref, dest_next_ref, e_hbm, tg_ref, x_ref, gate_ref, gpost_ref, *rest, split_out):
    out_refs, (buf, sem) = rest[:-2], rest[-2:]
    i = pl.program_id(0)
    slot = i % 2

    def gather(idx_ref, dst_slot):
        def body(t, carry):
            for k in range(TOP_K):
                pltpu.make_async_copy(e_hbm.at[pl.ds(idx_ref[0, t * TOP_K + k], 1)],
                                      buf.at[dst_slot, pl.ds(k * CB_TOK + t, 1)],
                                      sem.at[dst_slot]).start(priority=k % 2)
            return carry
        lax.fori_loop(0, CB_TOK, body, 0, unroll=2)

    @pl.when(i == 0)
    def _():
        gather(dest_ref, 0)

    @pl.when(i + 1 < CB_STEPS)
    def _():
        gather(dest_next_ref, 1 - slot)

    pltpu.make_async_copy(e_hbm.at[pl.ds(0, CB_ROWS)], buf.at[slot], sem.at[slot]).wait()
    tg = tg_ref[...]
    y = buf[slot, 0:CB_TOK, :] * tg[:, 0:1]
    for k in range(1, TOP_K):
        y = y + buf[slot, k * CB_TOK:(k + 1) * CB_TOK, :] * tg[:, k:k + 1]
    x_new = x_ref[...] + gate_ref[...] * (_rms(y) * gpost_ref[...])
    if split_out:
        @pl.when(i < CB_PROMPT_STEPS)
        def _():
            out_refs[0][...] = x_new

        @pl.when(i >= CB_PROMPT_STEPS)
        def _():
            out_refs[1][...] = x_new
    else:
        out_refs[0][...] = x_new


def _combine_post_ffn(expert_out, dest, top_gate, x, gate, g_post, split_out):
    row = pl.BlockSpec((CB_TOK, D_MODEL), lambda i: (i, 0))
    dest3 = dest.reshape(CB_STEPS, 1, CB_ROWS)
    if split_out:
        out_specs = [pl.BlockSpec((CB_TOK, D_MODEL), lambda i: (jnp.minimum(i, CB_PROMPT_STEPS - 1), 0)),
                     pl.BlockSpec((CB_TOK, D_MODEL), lambda i: (jnp.maximum(i - CB_PROMPT_STEPS, 0), 0))]
        out_shape = [jax.ShapeDtypeStruct((N_PROMPT, D_MODEL), F32), jax.ShapeDtypeStruct((N_SAMPLE, D_MODEL), F32)]
    else:
        out_specs = row
        out_shape = jax.ShapeDtypeStruct((N_TOK, D_MODEL), F32)
    return pl.pallas_call(
        functools.partial(_combine_kernel, split_out=split_out),
        name="moe_combine",
        grid=(CB_STEPS,),
        in_specs=[
            pl.BlockSpec((None, 1, CB_ROWS), lambda i: (i, 0, 0), memory_space=pltpu.SMEM),
            pl.BlockSpec((None, 1, CB_ROWS), lambda i: (jnp.minimum(i + 1, CB_STEPS - 1), 0, 0),
                         memory_space=pltpu.SMEM),
            pl.BlockSpec(memory_space=pl.ANY),
            pl.BlockSpec((CB_TOK, LANES), lambda i: (i, 0)),
            row,
            pl.BlockSpec((None, 1, D_MODEL), lambda i: (i * CB_TOK // PB_TOK, 0, 0)),
            pl.BlockSpec((1, D_MODEL), lambda i: (0, 0)),
        ],
        out_specs=out_specs,
        out_shape=out_shape,
        scratch_shapes=[pltpu.VMEM((2, CB_ROWS, D_MODEL), F32), pltpu.SemaphoreType.DMA((2,))],
        compiler_params=_params(("arbitrary",), 40),
    )(dest3, dest3, expert_out, top_gate, x, gate, g_post.reshape(1, D_MODEL))


def _for_valid_rows(nsub, first, rest):
    pair = 2 * MOE_SUB

    @pl.when(nsub == 1)
    def _():
        first(0, MOE_SUB)

    @pl.when(nsub >= 2)
    def _():
        first(0, pair)

        def body(p, carry):
            rest(pl.multiple_of(p * pair, pair), pair)
            return carry

        lax.fori_loop(1, nsub // 2, body, 0)

        @pl.when(nsub % 2 == 1)
        def _():
            rest(pl.multiple_of((nsub - 1) * MOE_SUB, MOE_SUB), MOE_SUB)


GATHER_UNROLL = 8


def _moe_up_kernel(sbe_ref, sbi_ref, nsub_ref, tok_ref, tok_next_ref, h_hbm, wg_ref, wl_ref, bg_ref, bl_ref,
                   o_ref, xg, xb, wgb, wlb, sem):
    s = pl.program_id(0)
    c = pl.program_id(1)
    nsub = nsub_ref[s]

    def gather(idx_ref, n_sub_blocks):
        def body(j, carry):
            for u in range(GATHER_UNROLL):
                a = j * GATHER_UNROLL + u
                pltpu.make_async_copy(h_hbm.at[pl.ds(idx_ref[0, a], 1)], xg.at[pl.ds(a, 1)],
                                      sem).start(priority=u % 2)
            return carry
        lax.fori_loop(0, n_sub_blocks * (MOE_SUB // GATHER_UNROLL), body, 0)

    @pl.when((nsub > 0) & (c == 0))
    def _():
        @pl.when(s == 0)
        def _():
            gather(tok_ref, nsub)

        def wait_rows(r, carry):
            pltpu.make_async_copy(h_hbm.at[pl.ds(0, MOE_SUB)], xg.at[pl.ds(0, MOE_SUB)], sem).wait()
            return carry

        lax.fori_loop(0, nsub, wait_rows, 0)

        def unpack(r, carry):
            rows = pl.ds(pl.multiple_of(r * MOE_SUB, MOE_SUB), MOE_SUB)
            w = xg[rows, :]
            xb[rows, 0:HALF_D] = pltpu.bitcast(w << SHIFT16, F32).astype(BF16)
            xb[rows, HALF_D:D_MODEL] = pltpu.bitcast(w & HI_MASK, F32).astype(BF16)
            return carry

        lax.fori_loop(0, nsub, unpack, 0)

        s_next = jnp.minimum(s + 1, MOE_NSB - 1)
        n_next = jnp.where(s + 1 < MOE_NSB, nsub_ref[s_next], 0)

        @pl.when(n_next > 0)
        def _():
            gather(tok_next_ref, n_next)

    def swiglu_store(rows, xs, wg, wl):
        glu = jnp.dot(xs, wg, preferred_element_type=F32) + bg_ref[...]
        lin = jnp.dot(xs, wl, preferred_element_type=F32) + bl_ref[...]
        glu = jnp.minimum(glu, SWIGLU_LIMIT)
        lin = jnp.clip(lin, -SWIGLU_LIMIT, SWIGLU_LIMIT)
        act = glu / (1.0 + jnp.exp(-SWIGLU_ALPHA * glu)) * (lin + 1.0)
        o_ref[rows, :] = act.astype(BF16)

    def first(r0, n):
        wg = wg_ref[...].astype(BF16)
        wl = wl_ref[...].astype(BF16)
        wgb[...] = wg
        wlb[...] = wl
        rows = pl.ds(r0, n)
        swiglu_store(rows, xb[rows, :], wg, wl)

    def rest(r0, n):
        rows = pl.ds(r0, n)
        swiglu_store(rows, xb[rows, :], wgb[...], wlb[...])

    @pl.when(nsub > 0)
    def _():
        _for_valid_rows(nsub, first, rest)


def _moe_up(h_packed, row_tok, sbe, sbi, nsub, w_gate_up, b_gate_up, layer):
    n_c = MOE_D_FF // MOE_FC
    last_c = n_c - 1

    def c_eff(s, c, nsub_ref):
        return jnp.where(nsub_ref[s] > 0, c, last_c)

    grid_spec = pltpu.PrefetchScalarGridSpec(
        num_scalar_prefetch=3,
        grid=(MOE_NSB, n_c),
        in_specs=[
            pl.BlockSpec((None, 1, MOE_R), lambda s, c, e, i, n: (i[s], 0, 0), memory_space=pltpu.SMEM),
            pl.BlockSpec((None, 1, MOE_R), lambda s, c, e, i, n: (i[jnp.minimum(s + 1, MOE_NSB - 1)], 0, 0),
                         memory_space=pltpu.SMEM),
            pl.BlockSpec(memory_space=pl.ANY),
            pl.BlockSpec((None, None, D_MODEL, MOE_FC), lambda s, c, e, i, n: (layer, e[s], 0, c_eff(s, c, n))),
            pl.BlockSpec((None, None, D_MODEL, MOE_FC),
                         lambda s, c, e, i, n: (layer, e[s], 0, n_c + c_eff(s, c, n))),
            pl.BlockSpec((None, None, 1, MOE_FC), lambda s, c, e, i, n: (layer, e[s], 0, c_eff(s, c, n))),
            pl.BlockSpec((None, None, 1, MOE_FC), lambda s, c, e, i, n: (layer, e[s], 0, n_c + c_eff(s, c, n))),
        ],
        out_specs=pl.BlockSpec((MOE_R, MOE_FC), lambda s, c, e, i, n: (i[s], c_eff(s, c, n))),
        scratch_shapes=[pltpu.VMEM((MOE_R, HALF_D), U32), pltpu.VMEM((MOE_R, D_MODEL), BF16),
                        pltpu.VMEM((D_MODEL, MOE_FC), BF16), pltpu.VMEM((D_MODEL, MOE_FC), BF16),
                        pltpu.SemaphoreType.DMA],
    )
    bgu = b_gate_up.reshape(DEPTH, N_EXPERTS, 1, 2 * MOE_D_FF)
    tok3 = row_tok.reshape(MOE_NSB, 1, MOE_R)
    return pl.pallas_call(
        _moe_up_kernel,
        name="moe_up",
        grid_spec=grid_spec,
        out_shape=jax.ShapeDtypeStruct((MOE_CAP, MOE_D_FF), BF16),
        compiler_params=_params(("arbitrary", "arbitrary"), 58),
    )(sbe, sbi, nsub, tok3, tok3, h_packed, w_gate_up, w_gate_up, bgu, bgu)


def _moe_down_kernel(sbe_ref, sbi_ref, nsub_ref, h_ref, w_ref, b_ref, o_ref, wb):
    s = pl.program_id(0)
    nsub = nsub_ref[s]

    def first(r0, n):
        w = w_ref[...].astype(BF16)
        wb[...] = w
        rows = pl.ds(r0, n)
        o_ref[rows, :] = jnp.dot(h_ref[rows, :], w, preferred_element_type=F32) + b_ref[...]

    def rest(r0, n):
        rows = pl.ds(r0, n)
        o_ref[rows, :] = jnp.dot(h_ref[rows, :], wb[...], preferred_element_type=F32) + b_ref[...]

    @pl.when(nsub > 0)
    def _():
        _for_valid_rows(nsub, first, rest)


def _moe_down(h_sorted, sbe, sbi, nsub, w_down, b_down, layer):
    n_c = D_MODEL // MOE_NC
    last_c = n_c - 1

    def c_eff(s, c, nsub_ref):
        return jnp.where(nsub_ref[s] > 0, c, last_c)

    grid_spec = pltpu.PrefetchScalarGridSpec(
        num_scalar_prefetch=3,
        grid=(MOE_NSB, n_c),
        in_specs=[
            pl.BlockSpec((MOE_R, MOE_D_FF), lambda s, c, e, i, n: (i[s], 0)),
            pl.BlockSpec((None, None, MOE_D_FF, MOE_NC), lambda s, c, e, i, n: (layer, e[s], 0, c_eff(s, c, n))),
            pl.BlockSpec((None, None, 1, MOE_NC), lambda s, c, e, i, n: (layer, e[s], 0, c_eff(s, c, n))),
        ],
        out_specs=pl.BlockSpec((MOE_R, MOE_NC), lambda s, c, e, i, n: (i[s], c_eff(s, c, n))),
        scratch_shapes=[pltpu.VMEM((MOE_D_FF, MOE_NC), BF16)],
    )
    return pl.pallas_call(
        _moe_down_kernel,
        name="moe_down",
        grid_spec=grid_spec,
        out_shape=jax.ShapeDtypeStruct((MOE_CAP, D_MODEL), F32),
        compiler_params=_params(("arbitrary", "arbitrary"), 58),
    )(sbe, sbi, nsub, h_sorted, w_down, b_down.reshape(DEPTH, N_EXPERTS, 1, D_MODEL))


def _route(top_idx):
    flat_e = top_idx.reshape(-1)
    onehot = (flat_e[:, None] == jnp.arange(N_EXPERTS, dtype=I32)[None, :]).astype(I32)
    csum = jnp.cumsum(onehot, axis=0)
    rank = jnp.take_along_axis(csum, flat_e[:, None], axis=1)[:, 0] - 1
    counts = csum[-1]
    nsb = (counts + MOE_R - 1) // MOE_R
    sb_end = jnp.cumsum(nsb)
    sb_off = sb_end - nsb
    dest = sb_off[flat_e] * MOE_R + rank
    n_real = sb_end[-1]
    s = jnp.arange(MOE_NSB, dtype=I32)
    s_eff = jnp.minimum(s, n_real - 1)
    sbe = jnp.minimum(jnp.searchsorted(sb_end, s_eff, side='right'), N_EXPERTS - 1).astype(I32)
    valid = jnp.clip(counts[sbe] - (s_eff - sb_off[sbe]) * MOE_R, 0, MOE_R)
    valid = jnp.where(s < n_real, valid, 0)
    nsub = ((valid + MOE_SUB - 1) // MOE_SUB).astype(I32)
    dest = dest.astype(I32)
    row_tok = jnp.zeros((MOE_CAP,), I32).at[dest].set(jnp.arange(N_ASSIGN, dtype=I32) // TOP_K)
    return dest, row_tok, sbe, s_eff.astype(I32), nsub


def _moe_experts(h_packed, top_idx, w_gate_up, b_gate_up, w_down, b_down, layer):
    dest, row_tok, sbe, sbi, nsub = _route(top_idx)
    h_sorted = _moe_up(h_packed, row_tok, sbe, sbi, nsub, w_gate_up, b_gate_up, layer)
    return _moe_down(h_sorted, sbe, sbi, nsub, w_down, b_down, layer), dest


def _pb_rows(mod_piece):
    idx = jnp.array([0] * N_PROMPT_PB + list(range(1, 1 + DEC_BATCH)), I32)
    return mod_piece[idx][:, None, :]


def kernel(x_prompt, x_sample, cache_a_k, cache_a_v, cache_b_k, cache_b_v, cache_c_k, cache_c_v, c, c_ctx, w_ada, b_ada, g_pre_mix, g_post_mix, g_pre_ffn, g_post_ffn, w_in_even, w_out_even, rpb_a, q_norm_b, k_norm_b, w_in_odd, w_out_odd, sink_c, w_router, b_router, w_gate_up, b_gate_up, w_down, b_down):
    x = jnp.concatenate([x_prompt.reshape(N_PROMPT, D_MODEL), x_sample.reshape(N_SAMPLE, D_MODEL)], axis=0)
    cond8 = jnp.concatenate([c_ctx[None, :], c, jnp.zeros((8 - 1 - DEC_BATCH, D_MODEL), F32)], axis=0)
    mods = _ada_mod(cond8, w_ada, b_ada)
    cos_tab, sin_tab = _rope_tables()
    no_sink = jnp.full((max(A_HEADS, B_HEADS),), -jnp.inf, F32)
    states = {}

    for l in range(DEPTH):
        j = l // 2
        m = [_pb_rows(mods[l, :, k * D_MODEL:(k + 1) * D_MODEL]) for k in range(6)]
        if l % 2 == 0:
            w_in = w_in_even[j].astype(BF16)
            w_out = w_out_even[j].astype(BF16)
            nb = EVEN_IN // IN_BN
            qb0 = 3 * A_HEADS * HEAD_DIM // IN_BN
            kb0 = qb0 + B_HEADS * HEAD_DIM // IN_BN
            vb0 = kb0 + B_KV_HEADS * HEAD_DIM // IN_BN
            modes = jnp.array([MODE_PLAIN] * qb0 + [MODE_NORM_Q] * (kb0 - qb0) + [MODE_NORM_K] * (vb0 - kb0)
                              + [MODE_PLAIN] * (nb - vb0), I32)
            gains = jnp.concatenate([q_norm_b[j][None], k_norm_b[j][None], jnp.ones((6, HEAD_DIM), F32)], axis=0)
        else:
            w_in = w_in_odd[j].astype(BF16)
            w_out = w_out_odd[j].astype(BF16)
            nb = ODD_IN // IN_BN
            vc0 = (C_HEADS + C_KV_HEADS) * HEAD_DIM // IN_BN
            modes = jnp.array([MODE_ROPE] * vc0 + [MODE_PLAIN] * (nb - vc0), I32)
            gains = jnp.ones((8, HEAD_DIM), F32)

        proj = _in_proj(x, g_pre_mix[l], m[0], m[1], w_in, modes, gains, cos_tab, sin_tab)

        if l % 2 == 0:
            ah = A_HEADS * HEAD_DIM
            ctx_a = _ctx_attention(proj, no_sink, 0, ah, 2 * ah, A_HEADS, A_HEADS, 8)
            ctx_b = _ctx_attention(proj, no_sink, 3 * ah, 3 * ah + B_HEADS * HEAD_DIM,
                                   3 * ah + (B_HEADS + B_KV_HEADS) * HEAD_DIM, B_HEADS, B_KV_HEADS, B_KV_HEADS)
            lat_a = _na_attention(proj, cache_a_k, cache_a_v, rpb_a[j], j)
            lat_b = _dense_attention(proj, cache_b_k, cache_b_v, j)
            prompt_parts, latent_parts = [ctx_a, ctx_b], [lat_a, lat_b]
            pp = proj[:N_PROMPT]
            kb_c = 3 * ah + B_HEADS * HEAD_DIM
            vb_c = kb_c + B_KV_HEADS * HEAD_DIM
            states.setdefault('ak', []).append(pp[:, ah:2 * ah].reshape(BATCH, SEQ, A_HEADS, HEAD_DIM))
            states.setdefault('av', []).append(pp[:, 2 * ah:3 * ah].reshape(BATCH, SEQ, A_HEADS, HEAD_DIM))
            states.setdefault('bk', []).append(pp[:, kb_c:vb_c].reshape(BATCH, SEQ, B_KV_HEADS, HEAD_DIM))
            states.setdefault('bv', []).append(pp[:, vb_c:].reshape(BATCH, SEQ, B_KV_HEADS, HEAD_DIM))
        else:
            qc = C_HEADS * HEAD_DIM
            kvw = C_KV_HEADS * HEAD_DIM
            ctx_c = _ctx_attention(proj, sink_c[j], 0, qc, qc + kvw, C_HEADS, C_KV_HEADS, C_KV_HEADS)
            lat_c = _window_attention(proj, cache_c_k, cache_c_v, sink_c[j], j)
            prompt_parts, latent_parts = [ctx_c], [lat_c]
            pp = proj[:N_PROMPT]
            states.setdefault('ck', []).append(pp[:, qc:qc + kvw].reshape(BATCH, SEQ, C_KV_HEADS, HEAD_DIM))
            states.setdefault('cv', []).append(pp[:, qc + kvw:].reshape(BATCH, SEQ, C_KV_HEADS, HEAD_DIM))

        y = _out_proj(prompt_parts, latent_parts, w_out)
        wr = jnp.pad(w_router[l], ((0, 0), (0, LANES - N_EXPERTS)))
        wr_hi = wr.astype(BF16)
        wr_lo = (wr - wr_hi.astype(F32)).astype(BF16)
        br = jnp.concatenate([b_router[l], jnp.full((LANES - N_EXPERTS,), NEG_BIG, F32)])[None, :]
        x, h_ffn, top_i, top_g = _post_mix(y, x, m[2], g_post_mix[l], g_pre_ffn[l], m[3], m[4], wr_hi, wr_lo, br)
        expert_out, dest = _moe_experts(h_ffn, top_i[:, :TOP_K], w_gate_up, b_gate_up, w_down, b_down, l)
        x = _combine_post_ffn(expert_out, dest, top_g, x, m[5], g_post_ffn[l], split_out=(l == DEPTH - 1))

    y_prompt = x[0].reshape(BATCH, SEQ, D_MODEL)
    y_sample = x[1].reshape(DEC_BATCH, DEC_SEQ, D_MODEL)
    st = {k: jnp.stack(v, axis=1) for k, v in states.items()}
    return (y_prompt, y_sample, st['ak'], st['av'], st['bk'], st['bv'], st['ck'], st['cv'])
```
